```python
import math
import jax, jax.numpy as jnp
from jax import lax
import numpy as np

D_MODEL = 1024
BATCH = 2
SEQ = 8192
DEPTH = 2

N_MIXERS = 2
DIFF_HEADS = 8
DIFF_HEAD_DIM = D_MODEL // DIFF_HEADS // 2
MOBA_HEADS = 16
MOBA_HEAD_DIM = D_MODEL // MOBA_HEADS
MOBA_BLOCK = 256
MOBA_TOPK = 3
D_FF = 4 * D_MODEL
Q_BLOCK = 128
MOBA_Q_CHUNK = 32
RMS_EPS = 1e-6
NEG_INF = -1e30
N_DIFF_LAYERS = (DEPTH + N_MIXERS - 1) // N_MIXERS

kernel_name = "diffattn_moba_interleaved_hybrid"


def alibi_slopes(n_heads):
    return np.array([2.0 ** (-8.0 * (h + 1) / n_heads) for h in range(n_heads)], dtype=np.float32)


def lambda_init_fn(layer_idx):
    return 0.8 - 0.6 * math.exp(-0.3 * layer_idx)


def rmsnorm(x, g):
    xf = x.astype(jnp.float32)
    y = xf * lax.rsqrt(jnp.mean(xf * xf, axis=-1, keepdims=True) + RMS_EPS) * g.astype(jnp.float32)
    return y.astype(x.dtype)


def squared_relu_mlp(h, w1, w2):
    return jnp.square(jax.nn.relu(h @ w1)) @ w2


def diff_attention(h, w_in, w_out, lam_params, subln_w, lam_init):
    B, T, _ = h.shape
    H, d = DIFF_HEADS, DIFF_HEAD_DIM
    q, k, v = jnp.split(h @ w_in, 3, axis=-1)
    q = q.reshape(B, T, H, 2, d).transpose(3, 0, 2, 1, 4)
    kf = k.reshape(B, T, H, 2, d).transpose(3, 0, 2, 1, 4).astype(jnp.float32)
    v = v.reshape(B, T, H, 2 * d).transpose(0, 2, 1, 3)
    lf = lam_params.astype(jnp.float32)
    lam = jnp.exp(jnp.sum(lf[0] * lf[1])) - jnp.exp(jnp.sum(lf[2] * lf[3])) + lam_init
    slopes = jnp.asarray(alibi_slopes(H))
    kpos = jnp.arange(T)
    scale = d ** -0.5

    def block(i):
        t0 = i * Q_BLOCK
        qb = lax.dynamic_slice_in_dim(q, t0, Q_BLOCK, axis=3).astype(jnp.float32)
        s = jnp.einsum('mbhqd,mbhkd->mbhqk', qb, kf) * scale
        dist = (t0 + jnp.arange(Q_BLOCK))[:, None] - kpos[None, :]
        bias = -slopes[:, None, None] * dist.astype(jnp.float32)
        s = jnp.where(dist >= 0, s + bias, NEG_INF)
        p = jax.nn.softmax(s, axis=-1)
        a = p[0] - lam * p[1]
        return jnp.einsum('bhqk,bhke->bhqe', a.astype(v.dtype), v)

    o = lax.map(block, jnp.arange(T // Q_BLOCK))
    o = o.transpose(1, 0, 3, 2, 4).reshape(B, T, H, 2 * d)
    o = rmsnorm(o, subln_w) * (1.0 - lam_init)
    return o.reshape(B, T, H * 2 * d) @ w_out


def moba_attention(h, w_in, w_out):
    B, T, _ = h.shape
    H, dh, BLK, QC = MOBA_HEADS, MOBA_HEAD_DIM, MOBA_BLOCK, MOBA_Q_CHUNK
    q, k, v = jnp.split(h @ w_in, 3, axis=-1)
    q = q.reshape(B, T, H, dh).transpose(0, 2, 1, 3)
    k = k.reshape(B, T, H, dh).transpose(0, 2, 1, 3)
    v = v.reshape(B, T, H, dh).transpose(0, 2, 1, 3)
    NB = -(-T // BLK)
    pad = NB * BLK - T
    kp = jnp.pad(k, ((0, 0), (0, 0), (0, pad), (0, 0)))
    vp = jnp.pad(v, ((0, 0), (0, 0), (0, pad), (0, 0)))
    kb = kp.reshape(B, H, NB, BLK, dh)
    vb = vp.reshape(B, H, NB, BLK, dh)
    kmean = jnp.mean(kb.astype(jnp.float32), axis=3)
    K_SEL = min(MOBA_TOPK, NB)
    slopes = jnp.asarray(alibi_slopes(H))
    scale = dh ** -0.5
    bi = jnp.arange(B)[:, None, None, None]
    hi = jnp.arange(H)[None, :, None, None]

    def chunk(c):
        t0 = c * QC
        j = t0 // BLK
        qc = lax.dynamic_slice_in_dim(q, t0, QC, axis=2).astype(jnp.float32)
        qpos = t0 + jnp.arange(QC)
        gate = jnp.einsum('bhqd,bhnd->bhqn', qc, kmean)
        gate = jnp.where(jnp.arange(NB) < j, gate, NEG_INF)
        _, idx = lax.top_k(gate, K_SEL)
        valid = jnp.arange(K_SEL) < j
        s0 = j * BLK
        k_own = lax.dynamic_slice_in_dim(kp, s0, BLK, axis=2).astype(jnp.float32)
        v_own = lax.dynamic_slice_in_dim(vp, s0, BLK, axis=2)
        d_own = (qpos[:, None] - (s0 + jnp.arange(BLK))[None, :]).astype(jnp.float32)
        sc_own = jnp.einsum('bhqd,bhkd->bhqk', qc, k_own) * scale - slopes[:, None, None] * d_own
        sc_own = jnp.where(d_own >= 0, sc_own, NEG_INF)
        k_sel = kb[bi, hi, idx]
        v_sel = vb[bi, hi, idx]
        sel_pos = idx[..., None] * BLK + jnp.arange(BLK)
        d_sel = (qpos[None, None, :, None, None] - sel_pos).astype(jnp.float32)
        sc_sel = (jnp.einsum('bhqd,bhqkld->bhqkl', qc, k_sel.astype(jnp.float32)) * scale
                  - slopes[None, :, None, None, None] * d_sel)
        sc_sel = jnp.where(valid[:, None], sc_sel, NEG_INF)
        sc = jnp.concatenate([sc_own, sc_sel.reshape(B, H, QC, K_SEL * BLK)], axis=-1)
        p = jax.nn.softmax(sc, axis=-1)
        p_own = p[..., :BLK].astype(v.dtype)
        p_sel = p[..., BLK:].reshape(B, H, QC, K_SEL, BLK).astype(v.dtype)
        return (jnp.einsum('bhqk,bhkd->bhqd', p_own, v_own)
                + jnp.einsum('bhqkl,bhqkld->bhqd', p_sel, v_sel))

    o = lax.map(chunk, jnp.arange(T // QC))
    o = o.transpose(1, 0, 3, 2, 4).reshape(B, T, H * dh)
    return o @ w_out


def setup_inputs(seed: int = 0) -> dict:
    key = jax.random.key(seed)
    ks = jax.random.split(key, 10)
    f32 = jnp.float32
    x = jax.random.normal(ks[0], (BATCH, SEQ, D_MODEL), f32)
    attn_norm = 1.0 + 0.02 * jax.random.normal(ks[1], (DEPTH, D_MODEL), f32)
    w_in = jax.random.normal(ks[2], (DEPTH, D_MODEL, 3 * D_MODEL), f32) * D_MODEL ** -0.5
    w_out = jax.random.normal(ks[3], (DEPTH, D_MODEL, D_MODEL), f32) * D_MODEL ** -0.5
    diff_lambda = 0.1 * jax.random.normal(ks[4], (N_DIFF_LAYERS, 4, DIFF_HEAD_DIM), f32)
    diff_subln = 1.0 + 0.02 * jax.random.normal(ks[5], (N_DIFF_LAYERS, 2 * DIFF_HEAD_DIM), f32)
    mlp_norm = 1.0 + 0.02 * jax.random.normal(ks[6], (DEPTH, D_MODEL), f32)
    w_ff1 = jax.random.normal(ks[7], (DEPTH, D_MODEL, D_FF), f32) * D_MODEL ** -0.5
    w_ff2 = jax.random.normal(ks[8], (DEPTH, D_FF, D_MODEL), f32) * D_FF ** -0.5
    final_norm = 1.0 + 0.02 * jax.random.normal(ks[9], (D_MODEL,), f32)
    return {"x": x, "attn_norm": attn_norm, "w_in": w_in, "w_out": w_out,
            "diff_lambda": diff_lambda, "diff_subln": diff_subln, "mlp_norm": mlp_norm,
            "w_ff1": w_ff1, "w_ff2": w_ff2, "final_norm": final_norm}


def reference(x, attn_norm, w_in, w_out, diff_lambda, diff_subln, mlp_norm, w_ff1, w_ff2, final_norm):
    h = x
    for i in range(DEPTH):
        hn = rmsnorm(h, attn_norm[i])
        if i % N_MIXERS == 0:
            li = i // N_MIXERS
            m = diff_attention(hn, w_in[i], w_out[i], diff_lambda[li], diff_subln[li], lambda_init_fn(i))
        else:
            m = moba_attention(hn, w_in[i], w_out[i])
        h = h + m
        h = h + squared_relu_mlp(rmsnorm(h, mlp_norm[i]), w_ff1[i], w_ff2[i])
    return rmsnorm(h, final_norm)
```

```python
import functools
import math

import jax
import jax.numpy as jnp
import numpy as np
from jax import lax
from jax.experimental import pallas as pl
from jax.experimental.pallas import tpu as pltpu

D_MODEL = 1024
DEPTH = 2
N_MIXERS = 2
DIFF_HEADS = 8
MOBA_HEADS = 16
HEAD_DIM = 64
PAIR = 2 * HEAD_DIM
N_PAIRS = D_MODEL // PAIR
MOBA_BLOCK = 256
MOBA_TOPK = 3
D_FF = 4 * D_MODEL
RMS_EPS = 1e-6
NEG_INF = -1e30

V7X_VMEM_LIMIT_BYTES = 56 * 1024 * 1024

ROW_TILE = 512
COL_TILE = 1024
ATT_TILE = 256

BF16 = jnp.bfloat16
F32 = jnp.float32


def _alibi_slopes(n_heads):
    return np.array([2.0 ** (-8.0 * (h + 1) / n_heads) for h in range(n_heads)], dtype=np.float32)


def _lambda_init(layer_idx):
    return 0.8 - 0.6 * math.exp(-0.3 * layer_idx)


def _rmsnorm_rows(x, g):
    return x * lax.rsqrt(jnp.mean(x * x, axis=-1, keepdims=True) + RMS_EPS) * g


def _norm_matmul_kernel(x_ref, g_ref, w_ref, o_ref, *, q_cols, q_scale, relu_sq):
    xn = _rmsnorm_rows(x_ref[...], g_ref[...]).astype(BF16)
    n = w_ref.shape[1]
    for c0 in range(0, n, COL_TILE):
        acc = jnp.dot(xn, w_ref[:, c0:c0 + COL_TILE], preferred_element_type=F32)
        if relu_sq:
            acc = jnp.square(jnp.maximum(acc, 0.0))
        if c0 < q_cols:
            acc = acc * q_scale
        o_ref[:, c0:c0 + COL_TILE] = acc.astype(o_ref.dtype)


def _norm_matmul(x, g, w, *, q_cols=0, q_scale=1.0, relu_sq=False):
    m, k = x.shape
    n = w.shape[1]
    return pl.pallas_call(
        functools.partial(_norm_matmul_kernel, q_cols=q_cols, q_scale=q_scale, relu_sq=relu_sq),
        grid=(m // ROW_TILE,),
        in_specs=[
            pl.BlockSpec((ROW_TILE, k), lambda i: (i, 0)),
            pl.BlockSpec((1, k), lambda i: (0, 0)),
            pl.BlockSpec((k, n), lambda i: (0, 0)),
        ],
        out_specs=pl.BlockSpec((ROW_TILE, n), lambda i: (i, 0)),
        out_shape=jax.ShapeDtypeStruct((m, n), BF16),
        compiler_params=pltpu.CompilerParams(
            dimension_semantics=("arbitrary",), vmem_limit_bytes=V7X_VMEM_LIMIT_BYTES),
    )(x, g.reshape(1, k), w)


def _matmul_residual_kernel(a_ref, w_ref, r_ref, g_ref, o_ref, *, final_norm):
    h = r_ref[...] + jnp.dot(a_ref[...], w_ref[...], preferred_element_type=F32)
    if final_norm:
        h = _rmsnorm_rows(h, g_ref[...])
    o_ref[...] = h


def _matmul_residual(a, w, r, g, *, final_norm):
    m, k = a.shape
    n = w.shape[1]
    return pl.pallas_call(
        functools.partial(_matmul_residual_kernel, final_norm=final_norm),
        grid=(m // ROW_TILE,),
        in_specs=[
            pl.BlockSpec((ROW_TILE, k), lambda i: (i, 0)),
            pl.BlockSpec((k, n), lambda i: (0, 0)),
            pl.BlockSpec((ROW_TILE, n), lambda i: (i, 0)),
            pl.BlockSpec((1, n), lambda i: (0, 0)),
        ],
        out_specs=pl.BlockSpec((ROW_TILE, n), lambda i: (i, 0)),
        out_shape=jax.ShapeDtypeStruct((m, n), F32),
        compiler_params=pltpu.CompilerParams(
            dimension_semantics=("arbitrary",), vmem_limit_bytes=V7X_VMEM_LIMIT_BYTES),
    )(a, w, r, g.reshape(1, n))


def _attention_kernel(slope_ref, q_ref, k_ref, v_ref, lam_ref, subln_ref, o_ref,
                      m_ref, l_ref, acc_ref, km_ref, *, moba, lam_init):
    pair = pl.program_id(1)
    qi = pl.program_id(2)
    t = ATT_TILE
    lane = lax.broadcasted_iota(jnp.int32, (t, PAIR), 1)
    in_a = lane < HEAD_DIM
    sub_lanes = (in_a, jnp.logical_not(in_a))
    sel_base = (HEAD_DIM, 0)

    q = q_ref[0]
    zero = jnp.zeros_like(q)

    if moba:
        slopes = (slope_ref[2 * pair], slope_ref[2 * pair + 1])

        n_blocks = k_ref.shape[1] // MOBA_BLOCK

        @pl.when(qi == 0)
        def _():
            km_ref[...] = jnp.zeros_like(km_ref)
            lane_row = lax.broadcasted_iota(jnp.int32, (1, PAIR), 1)
            for n in range(n_blocks):
                k_blk = k_ref[0, n * MOBA_BLOCK:(n + 1) * MOBA_BLOCK, :].astype(F32)
                kmean = jnp.mean(k_blk, axis=0, keepdims=True)
                r0, r1 = sel_base[0] + n, sel_base[1] + n
                km_ref[0, r0:r0 + 1, :] = jnp.where(lane_row < HEAD_DIM, kmean, 0.0)
                km_ref[1, r1:r1 + 1, :] = jnp.where(lane_row < HEAD_DIM, 0.0, kmean)

        lane_f = lane.astype(F32)
        q_sub = []
        for s in range(2):
            gate = lax.dot_general(q.astype(F32), km_ref[s], (((1,), (1,)), ((), ())),
                                   precision=lax.Precision.HIGHEST,
                                   preferred_element_type=F32)
            blk = lane - sel_base[s]
            cand = jnp.logical_and(blk >= 0, blk < qi)
            g = jnp.where(cand, gate, -jnp.inf)
            selected = jnp.zeros((t, PAIR), jnp.bool_)
            for pick in range(MOBA_TOPK):
                mx = jnp.max(g, axis=-1, keepdims=True)
                idx = jnp.min(jnp.where(g == mx, lane_f, F32(4 * PAIR)), axis=-1, keepdims=True)
                hit = jnp.logical_and(lane_f == idx, pick < qi)
                selected = jnp.logical_or(selected, hit)
                g = jnp.where(lane_f == idx, -jnp.inf, g)
            bias = jnp.where(selected, 0.0, NEG_INF)
            bias = jnp.where(jnp.logical_and(blk >= 0, blk < n_blocks), bias, 0.0)
            q_sub.append(jnp.where(sub_lanes[s], q, bias.astype(BF16)))
    else:
        slopes = (slope_ref[pair], slope_ref[pair])
        q_sub = [jnp.where(sub_lanes[s], q, zero) for s in range(2)]

    col = lax.broadcasted_iota(jnp.int32, (1, t), 1)
    row_i = lax.broadcasted_iota(jnp.int32, (t, t), 0)
    col_i = lax.broadcasted_iota(jnp.int32, (t, t), 1)
    causal = col_i <= row_i

    def scores(s, k_t, n, diagonal):
        if moba and not diagonal:
            onehot = jnp.where(lane == sel_base[s] + n, 1.0, 0.0).astype(BF16)
            k_t = jnp.where(sub_lanes[s], k_t, onehot)
        elif moba:
            k_t = jnp.where(sub_lanes[s], k_t, zero)
        sc = lax.dot_general(q_sub[s], k_t, (((1,), (1,)), ((), ())),
                             preferred_element_type=F32)
        sc = sc + slopes[s] * (col + (n - qi) * t).astype(F32)
        if diagonal:
            sc = jnp.where(causal, sc, NEG_INF)
        return sc

    def first_tile():
        k_t = k_ref[0, pl.ds(pl.multiple_of(qi * t, t), t), :]
        v_t = v_ref[0, pl.ds(pl.multiple_of(qi * t, t), t), :]
        for s in range(2):
            sc = scores(s, k_t, qi, True)
            m = jnp.max(sc, axis=-1, keepdims=True)
            p = jnp.exp(sc - m)
            m_ref[s] = jnp.broadcast_to(m, (t, PAIR))
            l_ref[s] = jnp.broadcast_to(jnp.sum(p, axis=-1, keepdims=True), (t, PAIR))
            acc_ref[s] = jnp.dot(p.astype(BF16), v_t, preferred_element_type=F32)

    def past_tile(n, carry):
        k_t = k_ref[0, pl.ds(pl.multiple_of(n * t, t), t), :]
        v_t = v_ref[0, pl.ds(pl.multiple_of(n * t, t), t), :]
        for s in range(2):
            sc = scores(s, k_t, n, False)
            m_prev = m_ref[s]
            m_next = jnp.maximum(m_prev, jnp.max(sc, axis=-1, keepdims=True))
            alpha = jnp.exp(m_prev - m_next)
            p = jnp.exp(sc - pltpu.repeat(m_next, t // PAIR, axis=1))
            m_ref[s] = m_next
            l_ref[s] = alpha * l_ref[s] + jnp.sum(p, axis=-1, keepdims=True)
            acc_ref[s] = alpha * acc_ref[s] + jnp.dot(p.astype(BF16), v_t,
                                                      preferred_element_type=F32)
        return carry

    first_tile()
    lax.fori_loop(0, qi, past_tile, 0)

    out_a = acc_ref[0] / l_ref[0]
    out_b = acc_ref[1] / l_ref[1]
    if moba:
        o_ref[0] = jnp.where(in_a, out_a, out_b).astype(o_ref.dtype)
    else:
        lp = lam_ref[...]
        lam = (jnp.exp(jnp.sum(lp[0:1] * lp[1:2], axis=-1, keepdims=True))
               - jnp.exp(jnp.sum(lp[2:3] * lp[3:4], axis=-1, keepdims=True)) + lam_init)
        o = out_a - lam * out_b
        o = _rmsnorm_rows(o, subln_ref[...]) * (1.0 - lam_init)
        o_ref[0] = o.astype(o_ref.dtype)


def _attention(qkv, slopes, lam_params, subln_w, *, moba, lam_init):
    b, t, _ = qkv.shape
    nq = t // ATT_TILE
    if moba:
        assert ATT_TILE == MOBA_BLOCK and t // MOBA_BLOCK <= HEAD_DIM
    kv_spec = lambda off: pl.BlockSpec((1, t, PAIR), lambda bi, p, i: (bi, 0, off + p))
    return pl.pallas_call(
        functools.partial(_attention_kernel, moba=moba, lam_init=lam_init),
        grid=(b, N_PAIRS, nq),
        in_specs=[
            pl.BlockSpec(memory_space=pltpu.SMEM),
            pl.BlockSpec((1, ATT_TILE, PAIR), lambda bi, p, i: (bi, i, p)),
            kv_spec(N_PAIRS),
            kv_spec(2 * N_PAIRS),
            pl.BlockSpec((4, HEAD_DIM), lambda bi, p, i: (0, 0)),
            pl.BlockSpec((1, PAIR), lambda bi, p, i: (0, 0)),
        ],
        out_specs=pl.BlockSpec((1, ATT_TILE, PAIR), lambda bi, p, i: (bi, i, p)),
        out_shape=jax.ShapeDtypeStruct((b, t, D_MODEL), BF16),
        scratch_shapes=[
            pltpu.VMEM((2, ATT_TILE, PAIR), F32),
            pltpu.VMEM((2, ATT_TILE, PAIR), F32),
            pltpu.VMEM((2, ATT_TILE, PAIR), F32),
            pltpu.VMEM((2, PAIR, PAIR), F32),
        ],
        compiler_params=pltpu.CompilerParams(
            dimension_semantics=("arbitrary", "arbitrary", "arbitrary"),
            vmem_limit_bytes=V7X_VMEM_LIMIT_BYTES),
    )(slopes, qkv, qkv, qkv, lam_params, subln_w.reshape(1, PAIR))


def kernel(x, attn_norm, w_in, w_out, diff_lambda, diff_subln, mlp_norm, w_ff1, w_ff2, final_norm):
    b, t, d = x.shape
    h = x.reshape(b * t, d)
    w_in_b, w_out_b = w_in.astype(BF16), w_out.astype(BF16)
    w_ff1_b, w_ff2_b = w_ff1.astype(BF16), w_ff2.astype(BF16)
    scale = HEAD_DIM ** -0.5
    for i in range(DEPTH):
        moba = i % N_MIXERS == 1
        qkv = _norm_matmul(h, attn_norm[i], w_in_b[i], q_cols=D_MODEL, q_scale=scale)
        li = i // N_MIXERS
        slopes = jnp.asarray(_alibi_slopes(MOBA_HEADS if moba else DIFF_HEADS))
        mix = _attention(qkv.reshape(b, t, 3 * d), slopes, diff_lambda[li], diff_subln[li],
                         moba=moba, lam_init=_lambda_init(i))
        h = _matmul_residual(mix.reshape(b * t, d), w_out_b[i], h, final_norm, final_norm=False)
        u = _norm_matmul(h, mlp_norm[i], w_ff1_b[i], relu_sq=True)
        h = _matmul_residual(u, w_ff2_b[i], h, final_norm, final_norm=(i == DEPTH - 1))
    return h.reshape(b, t, d)
```

```python
import functools
import math

import jax
import jax.numpy as jnp
import numpy as np
from jax import lax
from jax.experimental import pallas as pl
from jax.experimental.pallas import tpu as pltpu

D_MODEL = 1024
DEPTH = 2
N_MIXERS = 2
DIFF_HEADS = 8
MOBA_HEADS = 16
HEAD_DIM = 64
PAIR = 2 * HEAD_DIM
N_PAIRS = D_MODEL // PAIR
MOBA_BLOCK = 256
MOBA_TOPK = 3
D_FF = 4 * D_MODEL
RMS_EPS = 1e-6
NEG_INF = -1e30
LOG2E = math.log2(math.e)

V7X_VMEM_LIMIT_BYTES = 56 * 1024 * 1024

ROW_TILE = 512
COL_TILE = 1024
ATT_TILE = 256
ATT_CHUNK_TILES = 2
ATT_CHUNK = ATT_CHUNK_TILES * ATT_TILE
ATT_GROUP = 2

AUG_BASE = (HEAD_DIM, 0)
AUG_BLOCKS = 32
AUG_POS = AUG_BLOCKS
AUG_SLOPE_PIECES = 3
AUG_ROWS = 16

BF16 = jnp.bfloat16
F32 = jnp.float32


def _alibi_slope_pieces(n_heads):
    slopes = np.array([2.0 ** (-8.0 * (h + 1) / n_heads) for h in range(n_heads)], np.float64)
    rest = slopes * LOG2E
    pieces = []
    for _ in range(AUG_SLOPE_PIECES):
        p = rest.astype(np.float32).astype(BF16).astype(np.float64)
        pieces.append(p)
        rest = rest - p
    pieces = np.stack(pieces, axis=1)
    table = np.concatenate([pieces, pieces.sum(axis=1, keepdims=True)], axis=1)
    return table.astype(np.float32).reshape(-1)


def _lambda_init(layer_idx):
    return 0.8 - 0.6 * math.exp(-0.3 * layer_idx)


def _rmsnorm_rows(x, g):
    return x * lax.rsqrt(jnp.mean(x * x, axis=-1, keepdims=True) + RMS_EPS) * g


_NT = (((1,), (1,)), ((), ()))


def _qkv_kernel(x_ref, g_ref, wqt_ref, wk_ref, wvt_ref, qt_ref, ka_ref, vt_ref, *,
                q_scale, steps_per_seq):
    xn = _rmsnorm_rows(x_ref[...], g_ref[...]).astype(BF16)
    for c0 in range(0, D_MODEL, 256):
        vt = lax.dot_general(wvt_ref[c0:c0 + 256, :], xn, _NT, preferred_element_type=F32)
        vt_ref[0, 0, c0:c0 + 256, :] = vt.astype(BF16)
        for j in range(ROW_TILE // ATT_TILE):
            qt = lax.dot_general(wqt_ref[c0:c0 + 256, :], xn[j * ATT_TILE:(j + 1) * ATT_TILE],
                                 _NT, preferred_element_type=F32)
            qt_ref[0, j, c0:c0 + 256, :] = (qt * q_scale).astype(BF16)
    k = jnp.dot(xn, wk_ref[...], preferred_element_type=F32).astype(BF16)
    lane = lax.broadcasted_iota(jnp.int32, (ROW_TILE, D_MODEL), 1) % PAIR
    row = lax.broadcasted_iota(jnp.int32, (ROW_TILE, D_MODEL), 0)
    tile = (pl.program_id(0) % steps_per_seq) * (ROW_TILE // ATT_TILE) + row // ATT_TILE
    pos = (row % ATT_TILE).astype(F32)
    tile_pos = ((tile % ATT_CHUNK_TILES) * ATT_TILE).astype(F32)
    for s in range(2):
        rel = lane - AUG_BASE[s]
        aug = jnp.where(rel == tile, 1.0, 0.0)
        aug = jnp.where(jnp.logical_and(rel >= AUG_POS, rel < AUG_POS + AUG_SLOPE_PIECES), pos, aug)
        aug = jnp.where(jnp.logical_and(rel >= AUG_POS + AUG_SLOPE_PIECES,
                                        rel < AUG_POS + 2 * AUG_SLOPE_PIECES), tile_pos, aug)
        own = jnp.logical_and(rel >= -HEAD_DIM, rel < 0) if s == 0 else rel >= HEAD_DIM
        ka_ref[s] = jnp.where(own, k, aug.astype(BF16))


def _qkv_projection(x, g, wq_t, wk, wv_t, *, batch, q_scale):
    m, d = x.shape
    t = m // batch
    spb = t // ROW_TILE
    spc = ATT_CHUNK // ROW_TILE
    assert t // ATT_TILE <= AUG_BLOCKS and ATT_CHUNK % ROW_TILE == 0 and t % ATT_CHUNK == 0
    w_spec = pl.BlockSpec((d, d), lambda i: (0, 0))
    return pl.pallas_call(
        functools.partial(_qkv_kernel, q_scale=q_scale, steps_per_seq=spb),
        grid=(m // ROW_TILE,),
        in_specs=[
            pl.BlockSpec((ROW_TILE, d), lambda i: (i, 0)),
            pl.BlockSpec((1, d), lambda i: (0, 0)),
            w_spec, w_spec, w_spec,
        ],
        out_specs=[
            pl.BlockSpec((1, ROW_TILE // ATT_TILE, d, ATT_TILE), lambda i: (i // spb, i % spb, 0, 0)),
            pl.BlockSpec((2, ROW_TILE, d), lambda i: (0, i, 0)),
            pl.BlockSpec((1, 1, d, ROW_TILE), lambda i: (i // spb, (i % spb) // spc, 0, i % spc)),
        ],
        out_shape=[
            jax.ShapeDtypeStruct((batch, t // ATT_TILE, d, ATT_TILE), BF16),
            jax.ShapeDtypeStruct((2, m, d), BF16),
            jax.ShapeDtypeStruct((batch, t // ATT_CHUNK, d, ATT_CHUNK), BF16),
        ],
        compiler_params=pltpu.CompilerParams(
            dimension_semantics=("arbitrary",), vmem_limit_bytes=V7X_VMEM_LIMIT_BYTES),
        name="qkv_projection",
    )(x, g.reshape(1, d), wq_t, wk, wv_t)


def _norm_matmul_kernel(x_ref, g_ref, w_ref, o_ref):
    xn = _rmsnorm_rows(x_ref[...], g_ref[...]).astype(BF16)
    for c0 in range(0, w_ref.shape[1], COL_TILE):
        acc = jnp.dot(xn, w_ref[:, c0:c0 + COL_TILE], preferred_element_type=F32)
        o_ref[:, c0:c0 + COL_TILE] = jnp.square(jnp.maximum(acc, 0.0)).astype(o_ref.dtype)


def _norm_matmul(x, g, w):
    m, k = x.shape
    n = w.shape[1]
    return pl.pallas_call(
        _norm_matmul_kernel,
        grid=(m // ROW_TILE,),
        in_specs=[
            pl.BlockSpec((ROW_TILE, k), lambda i: (i, 0)),
            pl.BlockSpec((1, k), lambda i: (0, 0)),
            pl.BlockSpec((k, n), lambda i: (0, 0)),
        ],
        out_specs=pl.BlockSpec((ROW_TILE, n), lambda i: (i, 0)),
        out_shape=jax.ShapeDtypeStruct((m, n), BF16),
        compiler_params=pltpu.CompilerParams(
            dimension_semantics=("arbitrary",), vmem_limit_bytes=V7X_VMEM_LIMIT_BYTES),
        name="mlp_up",
    )(x, g.reshape(1, k), w)


def _matmul_residual_kernel(a_ref, w_ref, r_ref, g_ref, o_ref, *, final_norm):
    h = r_ref[...] + jnp.dot(a_ref[...], w_ref[...], preferred_element_type=F32)
    if final_norm:
        h = _rmsnorm_rows(h, g_ref[...])
    o_ref[...] = h


def _matmul_residual(a, w, r, g, *, final_norm, name):
    m, k = a.shape
    n = w.shape[1]
    return pl.pallas_call(
        functools.partial(_matmul_residual_kernel, final_norm=final_norm),
        grid=(m // ROW_TILE,),
        in_specs=[
            pl.BlockSpec((ROW_TILE, k), lambda i: (i, 0)),
            pl.BlockSpec((k, n), lambda i: (0, 0)),
            pl.BlockSpec((ROW_TILE, n), lambda i: (i, 0)),
            pl.BlockSpec((1, n), lambda i: (0, 0)),
        ],
        out_specs=pl.BlockSpec((ROW_TILE, n), lambda i: (i, 0)),
        out_shape=jax.ShapeDtypeStruct((m, n), F32),
        compiler_params=pltpu.CompilerParams(
            dimension_semantics=("arbitrary",), vmem_limit_bytes=V7X_VMEM_LIMIT_BYTES),
        name=name,
    )(a, w, r, g.reshape(1, n))


def _attention_kernel(slope_ref, qt_ref, ka_ref, vt_ref, lam_ref, subln_ref, o_ref,
                      qa_ref, m_ref, l_ref, acc_ref, km_ref, s_ref, p_ref, mc_ref, *, moba, lam_init):
    group = pl.program_id(1)
    qi = pl.program_id(2)
    t = ATT_TILE
    n_tab = AUG_SLOPE_PIECES + 1
    n_blocks = ka_ref.shape[1] // MOBA_BLOCK

    def head_of(g, s):
        pair = group * ATT_GROUP + g
        return 2 * pair + s if moba else pair

    if moba:
        @pl.when(qi == 0)
        def _():
            lane_row = lax.broadcasted_iota(jnp.int32, (1, PAIR), 1)
            km_ref[...] = jnp.zeros_like(km_ref)
            for g in range(ATT_GROUP):
                for n in range(n_blocks):
                    rows = slice(n * MOBA_BLOCK, (n + 1) * MOBA_BLOCK)
                    lanes = slice(g * PAIR, (g + 1) * PAIR)
                    mean_a = jnp.mean(ka_ref[0, rows, lanes].astype(F32), axis=0, keepdims=True)
                    mean_b = jnp.mean(ka_ref[1, rows, lanes].astype(F32), axis=0, keepdims=True)
                    km_ref[g, n:n + 1, :] = jnp.where(lane_row < HEAD_DIM, mean_a, mean_b)

    row16 = lax.broadcasted_iota(jnp.int32, (AUG_ROWS, t), 0)
    own_rows = lax.broadcasted_iota(jnp.int32, (PAIR, t), 0) < HEAD_DIM
    blk = lax.broadcasted_iota(jnp.int32, (AUG_BLOCKS, t), 0)
    blk_f = blk.astype(F32)
    for g in range(ATT_GROUP):
        qt = qt_ref[0, 0, g * PAIR:(g + 1) * PAIR, :].astype(F32)
        for s in range(2):
            h = head_of(g, s)
            pos_rows = jnp.zeros((AUG_ROWS, t), F32)
            for piece in range(AUG_SLOPE_PIECES):
                hit = jnp.logical_or(row16 == piece, row16 == piece + AUG_SLOPE_PIECES)
                pos_rows = jnp.where(hit, slope_ref[h * n_tab + piece], pos_rows)
            if moba:
                q_own = jnp.where(own_rows if s == 0 else jnp.logical_not(own_rows), qt, 0.0)
                gate = jnp.dot(km_ref[g], q_own, precision=lax.Precision.HIGHEST,
                               preferred_element_type=F32)
                gt = jnp.where(blk < qi, gate, -jnp.inf)
                selected = blk == qi
                for pick in range(MOBA_TOPK):
                    mx = jnp.max(gt, axis=0, keepdims=True)
                    idx = jnp.min(jnp.where(gt == mx, blk_f, F32(AUG_BLOCKS)), axis=0,
                                  keepdims=True)
                    hit = jnp.logical_and(blk_f == idx, pick < qi)
                    selected = jnp.logical_or(selected, hit)
                    gt = jnp.where(blk_f == idx, -jnp.inf, gt)
                sel_rows = jnp.where(selected, 0.0, NEG_INF)
            else:
                sel_rows = jnp.zeros((AUG_BLOCKS, t), F32)
            own = qt[0:HEAD_DIM] if s == 0 else qt[HEAD_DIM:PAIR]
            tail = jnp.zeros((HEAD_DIM - AUG_BLOCKS - AUG_ROWS, t), F32)
            parts = [sel_rows, pos_rows, tail]
            parts = [own] + parts if s == 0 else parts + [own]
            qa_ref[g, s] = jnp.concatenate(parts, axis=0).astype(BF16)

    chains = [(g, s) for g in range(ATT_GROUP) for s in range(2)]
    n_full = qi // ATT_CHUNK_TILES

    def chunk_offset(g, s, c):
        return (slope_ref[head_of(g, s) * n_tab + AUG_SLOPE_PIECES]
                * ((c * ATT_CHUNK_TILES - qi) * t).astype(F32))

    ones_rows = jnp.ones((AUG_ROWS, ATT_CHUNK), BF16)

    def scores(g, s, c, slot, causal=None):
        r0 = pl.multiple_of(c * ATT_CHUNK, ATT_CHUNK)
        ka_c = ka_ref[s, pl.ds(r0, ATT_CHUNK), g * PAIR:(g + 1) * PAIR]
        st = jnp.dot(ka_c, qa_ref[g, s], preferred_element_type=F32)
        if causal is not None:
            st = jnp.where(causal, st, NEG_INF)
        s_ref[slot, g, s] = st
        mc_ref[slot, g, s] = jnp.max(st, axis=0, keepdims=True) + chunk_offset(g, s, c)

    def accumulate(g, s, c, slot):
        m_prev = m_ref[g, s]
        m_next = jnp.maximum(m_prev, mc_ref[slot, g, s])
        alpha = jnp.exp2(m_prev - m_next)
        p = jnp.exp2(s_ref[slot, g, s] - (m_next - chunk_offset(g, s, c)))
        p_ref[g, s] = p.astype(BF16)
        m_ref[g, s] = m_next
        vt_c = jnp.concatenate([vt_ref[0, c, g * PAIR:(g + 1) * PAIR, :], ones_rows], axis=0)
        pv = jnp.dot(vt_c, p_ref[g, s], preferred_element_type=F32)
        l_ref[g, s] = alpha * l_ref[g, s] + pv[PAIR:PAIR + 1]
        acc_ref[g, s] = alpha * acc_ref[g, s] + pv[0:PAIR]

    def chunk_of(i):
        return jnp.where(i == 0, n_full, i - 1)

    def step(i, slot):
        for g, s in chains:
            scores(g, s, i, 1 - slot)
            accumulate(g, s, chunk_of(i), slot)

    row_i = lax.broadcasted_iota(jnp.int32, (ATT_CHUNK, t), 0)
    col_i = lax.broadcasted_iota(jnp.int32, (ATT_CHUNK, t), 1)
    causal = row_i - (qi % ATT_CHUNK_TILES) * t <= col_i
    m_ref[...] = jnp.full_like(m_ref, NEG_INF)
    l_ref[...] = jnp.zeros_like(l_ref)
    acc_ref[...] = jnp.zeros_like(acc_ref)
    for g, s in chains:
        scores(g, s, n_full, 0, causal)

    def two_steps(j, carry):
        step(2 * j, 0)
        step(2 * j + 1, 1)
        return carry

    lax.fori_loop(0, n_full // 2, two_steps, 0)

    @pl.when(n_full % 2 == 1)
    def _():
        step(n_full - 1, 0)
        for g, s in chains:
            accumulate(g, s, chunk_of(n_full), 1)

    @pl.when(n_full % 2 == 0)
    def _():
        for g, s in chains:
            accumulate(g, s, chunk_of(n_full), 0)

    for g in range(ATT_GROUP):
        out_a = acc_ref[g, 0] / l_ref[g, 0]
        out_b = acc_ref[g, 1] / l_ref[g, 1]
        if moba:
            o = jnp.concatenate([out_a[0:HEAD_DIM], out_b[HEAD_DIM:PAIR]], axis=0).T
        else:
            lp = lam_ref[...]
            lam = (jnp.exp(jnp.sum(lp[0:1] * lp[1:2], axis=-1, keepdims=True))
                   - jnp.exp(jnp.sum(lp[2:3] * lp[3:4], axis=-1, keepdims=True)) + lam_init)
            o = (out_a - lam * out_b).T
            o = _rmsnorm_rows(o, subln_ref[...]) * (1.0 - lam_init)
        o_ref[0, :, g * PAIR:(g + 1) * PAIR] = o.astype(o_ref.dtype)


def _attention(qt, ka, vt, slope_table, lam_params, subln_w, *, moba, lam_init):
    b, nq, d, _ = qt.shape
    t = nq * ATT_TILE
    n_blocks = t // MOBA_BLOCK
    assert ATT_TILE == MOBA_BLOCK and n_blocks <= AUG_BLOCKS
    assert AUG_POS + 2 * AUG_SLOPE_PIECES <= AUG_BLOCKS + AUG_ROWS <= HEAD_DIM
    gl = ATT_GROUP * PAIR
    return pl.pallas_call(
        functools.partial(_attention_kernel, moba=moba, lam_init=lam_init),
        grid=(b, N_PAIRS // ATT_GROUP, nq),
        in_specs=[
            pl.BlockSpec(memory_space=pltpu.SMEM),
            pl.BlockSpec((1, 1, gl, ATT_TILE), lambda bi, p, i: (bi, i, p, 0)),
            pl.BlockSpec((2, None, t, gl), lambda bi, p, i: (0, bi, 0, p)),
            pl.BlockSpec((1, t // ATT_CHUNK, gl, ATT_CHUNK), lambda bi, p, i: (bi, 0, p, 0)),
            pl.BlockSpec((4, HEAD_DIM), lambda bi, p, i: (0, 0)),
            pl.BlockSpec((1, PAIR), lambda bi, p, i: (0, 0)),
        ],
        out_specs=pl.BlockSpec((1, ATT_TILE, gl), lambda bi, p, i: (bi, i, p)),
        out_shape=jax.ShapeDtypeStruct((b, t, d), BF16),
        scratch_shapes=[
            pltpu.VMEM((ATT_GROUP, 2, PAIR, ATT_TILE), BF16),
            pltpu.VMEM((ATT_GROUP, 2, 1, ATT_TILE), F32),
            pltpu.VMEM((ATT_GROUP, 2, 1, ATT_TILE), F32),
            pltpu.VMEM((ATT_GROUP, 2, PAIR, ATT_TILE), F32),
            pltpu.VMEM((ATT_GROUP, AUG_BLOCKS, PAIR), F32),
            pltpu.VMEM((2, ATT_GROUP, 2, ATT_CHUNK, ATT_TILE), F32),
            pltpu.VMEM((ATT_GROUP, 2, ATT_CHUNK, ATT_TILE), BF16),
            pltpu.VMEM((2, ATT_GROUP, 2, 1, ATT_TILE), F32),
        ],
        compiler_params=pltpu.CompilerParams(
            dimension_semantics=("arbitrary", "arbitrary", "arbitrary"),
            vmem_limit_bytes=V7X_VMEM_LIMIT_BYTES),
        name="moba_attention" if moba else "diff_attention",
    )(slope_table, qt, ka.reshape(2, b, t, d), vt, lam_params, subln_w.reshape(1, PAIR))


def kernel(x, attn_norm, w_in, w_out, diff_lambda, diff_subln, mlp_norm, w_ff1, w_ff2, final_norm):
    b, t, d = x.shape
    h = x.reshape(b * t, d)
    wq_t = jnp.swapaxes(w_in[:, :, :d], 1, 2).astype(BF16)
    wk = w_in[:, :, d:2 * d].astype(BF16)
    wv_t = jnp.swapaxes(w_in[:, :, 2 * d:], 1, 2).astype(BF16)
    w_out_b = w_out.astype(BF16)
    w_ff1_b, w_ff2_b = w_ff1.astype(BF16), w_ff2.astype(BF16)
    q_scale = HEAD_DIM ** -0.5 * LOG2E
    for i in range(DEPTH):
        moba = i % N_MIXERS == 1
        li = i // N_MIXERS
        qt, ka, vt = _qkv_projection(h, attn_norm[i], wq_t[i], wk[i], wv_t[i],
                                     batch=b, q_scale=q_scale)
        slope_table = jnp.asarray(_alibi_slope_pieces(MOBA_HEADS if moba else DIFF_HEADS))
        mix = _attention(qt, ka, vt, slope_table, diff_lambda[li], diff_subln[li],
                         moba=moba, lam_init=_lambda_init(i))
        h = _matmul_residual(mix.reshape(b * t, d), w_out_b[i], h, final_norm,
                             final_norm=False, name="attn_out")
        u = _norm_matmul(h, mlp_norm[i], w_ff1_b[i])
        h = _matmul_residual(u, w_ff2_b[i], h, final_norm,
                             final_norm=(i == DEPTH - 1), name="mlp_down")
    return h.reshape(b, t, d)
```

```python
import functools
import math

import jax
import jax.numpy as jnp
import numpy as np
from jax import lax
from jax.experimental import pallas as pl
from jax.experimental.pallas import tpu as pltpu

D_MODEL = 1024
DEPTH = 2
N_MIXERS = 2
DIFF_HEADS = 8
MOBA_HEADS = 16
HEAD_DIM = 64
PAIR = 2 * HEAD_DIM
N_PAIRS = D_MODEL // PAIR
MOBA_BLOCK = 256
MOBA_TOPK = 3
D_FF = 4 * D_MODEL
RMS_EPS = 1e-6
NEG_INF = -1e30
LOG2E = math.log2(math.e)

V7X_VMEM_LIMIT_BYTES = 56 * 1024 * 1024

ROW_TILE = 512
COL_TILE = 1024
ATT_TILE = 256
ATT_CHUNK_TILES = 2
ATT_CHUNK = ATT_CHUNK_TILES * ATT_TILE
ATT_GROUP = 2

AUG_BASE = (HEAD_DIM, 0)
AUG_BLOCKS = 32
AUG_POS = AUG_BLOCKS
AUG_SLOPE_PIECES = 3
AUG_ROWS = 16

BF16 = jnp.bfloat16
F32 = jnp.float32


def _alibi_slope_pieces(n_heads):
    slopes = np.array([2.0 ** (-8.0 * (h + 1) / n_heads) for h in range(n_heads)], np.float64)
    rest = slopes * LOG2E
    pieces = []
    for _ in range(AUG_SLOPE_PIECES):
        p = rest.astype(np.float32).astype(BF16).astype(np.float64)
        pieces.append(p)
        rest = rest - p
    pieces = np.stack(pieces, axis=1)
    table = np.concatenate([pieces, pieces.sum(axis=1, keepdims=True)], axis=1)
    return table.astype(np.float32).reshape(-1)


def _lambda_init(layer_idx):
    return 0.8 - 0.6 * math.exp(-0.3 * layer_idx)


def _rmsnorm_rows(x, g):
    return x * lax.rsqrt(jnp.mean(x * x, axis=-1, keepdims=True) + RMS_EPS) * g


_NT = (((1,), (1,)), ((), ()))


def _qkv_kernel(x_ref, g_ref, wqt_ref, wk_ref, wvt_ref, qt_ref, ka_ref, vt_ref, *,
                q_scale, steps_per_seq):
    xn = _rmsnorm_rows(x_ref[...], g_ref[...]).astype(BF16)
    for c0 in range(0, D_MODEL, 256):
        vt = lax.dot_general(wvt_ref[c0:c0 + 256, :], xn, _NT, preferred_element_type=F32)
        vt_ref[0, 0, c0:c0 + 256, :] = vt.astype(BF16)
        for j in range(ROW_TILE // ATT_TILE):
            qt = lax.dot_general(wqt_ref[c0:c0 + 256, :], xn[j * ATT_TILE:(j + 1) * ATT_TILE],
                                 _NT, preferred_element_type=F32)
            qt_ref[0, j, c0:c0 + 256, :] = (qt * q_scale).astype(BF16)
    k = jnp.dot(xn, wk_ref[...], preferred_element_type=F32).astype(BF16)
    lane = lax.broadcasted_iota(jnp.int32, (ROW_TILE, D_MODEL), 1) % PAIR
    row = lax.broadcasted_iota(jnp.int32, (ROW_TILE, D_MODEL), 0)
    tile = (pl.program_id(0) % steps_per_seq) * (ROW_TILE // ATT_TILE) + row // ATT_TILE
    pos = (row % ATT_TILE).astype(F32)
    tile_pos = ((tile % ATT_CHUNK_TILES) * ATT_TILE).astype(F32)
    for s in range(2):
        rel = lane - AUG_BASE[s]
        aug = jnp.where(rel == tile, 1.0, 0.0)
        aug = jnp.where(jnp.logical_and(rel >= AUG_POS, rel < AUG_POS + AUG_SLOPE_PIECES), pos, aug)
        aug = jnp.where(jnp.logical_and(rel >= AUG_POS + AUG_SLOPE_PIECES,
                                        rel < AUG_POS + 2 * AUG_SLOPE_PIECES), tile_pos, aug)
        own = jnp.logical_and(rel >= -HEAD_DIM, rel < 0) if s == 0 else rel >= HEAD_DIM
        ka_ref[s] = jnp.where(own, k, aug.astype(BF16))


def _qkv_projection(x, g, wq_t, wk, wv_t, *, batch, q_scale):
    m, d = x.shape
    t = m // batch
    spb = t // ROW_TILE
    spc = ATT_CHUNK // ROW_TILE
    assert t // ATT_TILE <= AUG_BLOCKS and ATT_CHUNK % ROW_TILE == 0 and t % ATT_CHUNK == 0
    w_spec = pl.BlockSpec((d, d), lambda i: (0, 0))
    return pl.pallas_call(
        functools.partial(_qkv_kernel, q_scale=q_scale, steps_per_seq=spb),
        grid=(m // ROW_TILE,),
        in_specs=[
            pl.BlockSpec((ROW_TILE, d), lambda i: (i, 0)),
            pl.BlockSpec((1, d), lambda i: (0, 0)),
            w_spec, w_spec, w_spec,
        ],
        out_specs=[
            pl.BlockSpec((1, ROW_TILE // ATT_TILE, d, ATT_TILE), lambda i: (i // spb, i % spb, 0, 0)),
            pl.BlockSpec((2, ROW_TILE, d), lambda i: (0, i, 0)),
            pl.BlockSpec((1, 1, d, ROW_TILE), lambda i: (i // spb, (i % spb) // spc, 0, i % spc)),
        ],
        out_shape=[
            jax.ShapeDtypeStruct((batch, t // ATT_TILE, d, ATT_TILE), BF16),
            jax.ShapeDtypeStruct((2, m, d), BF16),
            jax.ShapeDtypeStruct((batch, t // ATT_CHUNK, d, ATT_CHUNK), BF16),
        ],
        compiler_params=pltpu.CompilerParams(
            dimension_semantics=("arbitrary",), vmem_limit_bytes=V7X_VMEM_LIMIT_BYTES),
        name="qkv_projection",
    )(x, g.reshape(1, d), wq_t, wk, wv_t)


def _norm_matmul_kernel(x_ref, g_ref, w_ref, o_ref):
    xn = _rmsnorm_rows(x_ref[...], g_ref[...]).astype(BF16)
    for c0 in range(0, w_ref.shape[1], COL_TILE):
        acc = jnp.dot(xn, w_ref[:, c0:c0 + COL_TILE], preferred_element_type=F32)
        o_ref[:, c0:c0 + COL_TILE] = jnp.square(jnp.maximum(acc, 0.0)).astype(o_ref.dtype)


def _norm_matmul(x, g, w):
    m, k = x.shape
    n = w.shape[1]
    return pl.pallas_call(
        _norm_matmul_kernel,
        grid=(m // ROW_TILE,),
        in_specs=[
            pl.BlockSpec((ROW_TILE, k), lambda i: (i, 0)),
            pl.BlockSpec((1, k), lambda i: (0, 0)),
            pl.BlockSpec((k, n), lambda i: (0, 0)),
        ],
        out_specs=pl.BlockSpec((ROW_TILE, n), lambda i: (i, 0)),
        out_shape=jax.ShapeDtypeStruct((m, n), BF16),
        compiler_params=pltpu.CompilerParams(
            dimension_semantics=("arbitrary",), vmem_limit_bytes=V7X_VMEM_LIMIT_BYTES),
        name="mlp_up",
    )(x, g.reshape(1, k), w)


def _matmul_residual_kernel(a_ref, w_ref, r_ref, g_ref, o_ref, *, final_norm):
    h = r_ref[...] + jnp.dot(a_ref[...], w_ref[...], preferred_element_type=F32)
    if final_norm:
        h = _rmsnorm_rows(h, g_ref[...])
    o_ref[...] = h


def _matmul_residual(a, w, r, g, *, final_norm, name):
    m, k = a.shape
    n = w.shape[1]
    return pl.pallas_call(
        functools.partial(_matmul_residual_kernel, final_norm=final_norm),
        grid=(m // ROW_TILE,),
        in_specs=[
            pl.BlockSpec((ROW_TILE, k), lambda i: (i, 0)),
            pl.BlockSpec((k, n), lambda i: (0, 0)),
            pl.BlockSpec((ROW_TILE, n), lambda i: (i, 0)),
            pl.BlockSpec((1, n), lambda i: (0, 0)),
        ],
        out_specs=pl.BlockSpec((ROW_TILE, n), lambda i: (i, 0)),
        out_shape=jax.ShapeDtypeStruct((m, n), F32),
        compiler_params=pltpu.CompilerParams(
            dimension_semantics=("arbitrary",), vmem_limit_bytes=V7X_VMEM_LIMIT_BYTES),
        name=name,
    )(a, w, r, g.reshape(1, n))


L_MIN = 2.0 ** -60
UB_MARGIN = 1.01


def _attention_kernel(slope_ref, qt_ref, ka_ref, vt_ref, lam_ref, subln_ref, o_ref,
                      qa_ref, ub_ref, l_ref, acc_ref, km_ref, kn_ref, p_ref, ps_ref, m_ref, *, moba,
                      lam_init):
    group = pl.program_id(1)
    qi = pl.program_id(2)
    t = ATT_TILE
    n_tab = AUG_SLOPE_PIECES + 1
    n_blocks = ka_ref.shape[1] // MOBA_BLOCK
    n_chunks = ka_ref.shape[1] // ATT_CHUNK

    def head_of(g, s):
        pair = group * ATT_GROUP + g
        return 2 * pair + s if moba else pair

    @pl.when(qi == 0)
    def _():
        lane8 = lax.broadcasted_iota(jnp.int32, (8, PAIR), 1)
        for g in range(ATT_GROUP):
            for s in range(2):
                own = (lane8 < HEAD_DIM) if s == 0 else (lane8 >= HEAD_DIM)
                ones_own = jnp.where(own, 1.0, 0.0).astype(BF16)

                def chunk_norms(c, best):
                    r0 = pl.multiple_of(c * ATT_CHUNK, ATT_CHUNK)
                    k_c = ka_ref[s, pl.ds(r0, ATT_CHUNK), g * PAIR:(g + 1) * PAIR].astype(F32)
                    sq = lax.dot_general(ones_own, (k_c * k_c).astype(BF16), _NT,
                                         preferred_element_type=F32)
                    return jnp.maximum(best, sq)

                best = lax.fori_loop(0, n_chunks, chunk_norms, jnp.zeros((8, ATT_CHUNK), F32))
                kn_ref[g, s] = jnp.broadcast_to(
                    jnp.sqrt(jnp.max(best, axis=1, keepdims=True)), (8, PAIR))

    if moba:
        @pl.when(qi == 0)
        def _():
            lane_row = lax.broadcasted_iota(jnp.int32, (1, PAIR), 1)
            km_ref[...] = jnp.zeros_like(km_ref)
            for g in range(ATT_GROUP):
                for n in range(n_blocks):
                    rows = slice(n * MOBA_BLOCK, (n + 1) * MOBA_BLOCK)
                    lanes = slice(g * PAIR, (g + 1) * PAIR)
                    mean_a = jnp.mean(ka_ref[0, rows, lanes].astype(F32), axis=0, keepdims=True)
                    mean_b = jnp.mean(ka_ref[1, rows, lanes].astype(F32), axis=0, keepdims=True)
                    km_ref[g, n:n + 1, :] = jnp.where(lane_row < HEAD_DIM, mean_a, mean_b)

    row16 = lax.broadcasted_iota(jnp.int32, (AUG_ROWS, t), 0)
    own_rows = lax.broadcasted_iota(jnp.int32, (PAIR, t), 0) < HEAD_DIM
    blk = lax.broadcasted_iota(jnp.int32, (AUG_BLOCKS, t), 0)
    blk_f = blk.astype(F32)
    q_pos = lax.broadcasted_iota(jnp.int32, (1, t), 1).astype(F32)
    for g in range(ATT_GROUP):
        qt = qt_ref[0, 0, g * PAIR:(g + 1) * PAIR, :].astype(F32)
        for s in range(2):
            h = head_of(g, s)
            pos_rows = jnp.zeros((AUG_ROWS, t), F32)
            for piece in range(AUG_SLOPE_PIECES):
                hit = jnp.logical_or(row16 == piece, row16 == piece + AUG_SLOPE_PIECES)
                pos_rows = jnp.where(hit, slope_ref[h * n_tab + piece], pos_rows)
            if moba:
                q_own = jnp.where(own_rows if s == 0 else jnp.logical_not(own_rows), qt, 0.0)
                gate = jnp.dot(km_ref[g], q_own, precision=lax.Precision.HIGHEST,
                               preferred_element_type=F32)
                gt = jnp.where(blk < qi, gate, -jnp.inf)
                selected = blk == qi
                for pick in range(MOBA_TOPK):
                    mx = jnp.max(gt, axis=0, keepdims=True)
                    idx = jnp.min(jnp.where(gt == mx, blk_f, F32(AUG_BLOCKS)), axis=0,
                                  keepdims=True)
                    hit = jnp.logical_and(blk_f == idx, pick < qi)
                    selected = jnp.logical_or(selected, hit)
                    gt = jnp.where(blk_f == idx, -jnp.inf, gt)
                sel_rows = jnp.where(selected, 0.0, NEG_INF)
            else:
                sel_rows = jnp.zeros((AUG_BLOCKS, t), F32)
            own = qt[0:HEAD_DIM] if s == 0 else qt[HEAD_DIM:PAIR]
            q_norm = jnp.sqrt(jnp.sum(own * own, axis=0, keepdims=True))
            ub_ref[g, s] = (q_norm * kn_ref[g, s][0:1, 0:1] * UB_MARGIN
                            + slope_ref[h * n_tab + AUG_SLOPE_PIECES] * q_pos)
            tail = jnp.zeros((HEAD_DIM - AUG_BLOCKS - AUG_ROWS, t), F32)
            parts = [sel_rows, pos_rows, tail]
            parts = [own] + parts if s == 0 else parts + [own]
            qa_ref[g, s] = jnp.concatenate(parts, axis=0).astype(BF16)

    chains = [(g, s) for g in range(ATT_GROUP) for s in range(2)]
    n_full = qi // ATT_CHUNK_TILES

    def chunk_offset(g, s, c):
        return (slope_ref[head_of(g, s) * n_tab + AUG_SLOPE_PIECES]
                * ((c * ATT_CHUNK_TILES - qi) * t).astype(F32))

    v_rows = HEAD_DIM if moba else PAIR
    ones_rows = jnp.ones((AUG_ROWS, ATT_CHUNK), BF16)

    def key_rows(g, s, c):
        r0 = pl.multiple_of(c * ATT_CHUNK, ATT_CHUNK)
        return ka_ref[s, pl.ds(r0, ATT_CHUNK), g * PAIR:(g + 1) * PAIR]

    def value_rows(g, s, c):
        v0 = g * PAIR + (s * HEAD_DIM if moba else 0)
        return jnp.concatenate([vt_ref[0, c, v0:v0 + v_rows, :], ones_rows], axis=0)

    def probabilities(g, s, c, slot, causal=None):
        st = jnp.dot(key_rows(g, s, c), qa_ref[g, s], preferred_element_type=F32)
        if causal is not None:
            st = jnp.where(causal, st, NEG_INF)
        p_ref[slot, g, s] = jnp.exp2(st - (ub_ref[g, s] - chunk_offset(g, s, c))).astype(BF16)

    def values(g, s, c, slot):
        pv = jnp.dot(value_rows(g, s, c), p_ref[slot, g, s], preferred_element_type=F32)
        l_ref[g, s] += pv[v_rows:v_rows + 1]
        acc_ref[g, s] += pv[0:v_rows]

    def chunk_of(i):
        return jnp.where(i == 0, n_full, i - 1)

    def step(i, slot):
        for g, s in chains:
            probabilities(g, s, i, 1 - slot)
            values(g, s, chunk_of(i), slot)

    row_i = lax.broadcasted_iota(jnp.int32, (ATT_CHUNK, t), 0)
    col_i = lax.broadcasted_iota(jnp.int32, (ATT_CHUNK, t), 1)
    causal = row_i - (qi % ATT_CHUNK_TILES) * t <= col_i
    l_ref[...] = jnp.zeros_like(l_ref)
    acc_ref[...] = jnp.zeros_like(acc_ref)
    for g, s in chains:
        probabilities(g, s, n_full, 0, causal)

    def two_steps(j, carry):
        step(2 * j, 0)
        step(2 * j + 1, 1)
        return carry

    lax.fori_loop(0, n_full // 2, two_steps, 0)

    @pl.when(n_full % 2 == 1)
    def _():
        step(n_full - 1, 0)
        for g, s in chains:
            values(g, s, chunk_of(n_full), 1)

    @pl.when(n_full % 2 == 0)
    def _():
        for g, s in chains:
            values(g, s, chunk_of(n_full), 0)

    def safe_path():
        m_ref[...] = jnp.full_like(m_ref, NEG_INF)
        l_ref[...] = jnp.zeros_like(l_ref)
        acc_ref[...] = jnp.zeros_like(acc_ref)

        def fold(g, s, c, mask):
            st = jnp.dot(key_rows(g, s, c), qa_ref[g, s], preferred_element_type=F32)
            if mask is not None:
                st = jnp.where(mask, st, NEG_INF)
            off = chunk_offset(g, s, c)
            m_prev = m_ref[g, s]
            m_next = jnp.maximum(m_prev, jnp.max(st, axis=0, keepdims=True) + off)
            alpha = jnp.exp2(m_prev - m_next)
            ps_ref[...] = jnp.exp2(st - (m_next - off)).astype(BF16)
            pv = jnp.dot(value_rows(g, s, c), ps_ref[...], preferred_element_type=F32)
            m_ref[g, s] = m_next
            l_ref[g, s] = alpha * l_ref[g, s] + pv[v_rows:v_rows + 1]
            acc_ref[g, s] = alpha * acc_ref[g, s] + pv[0:v_rows]

        for g, s in chains:
            fold(g, s, n_full, causal)

        def past(c, carry):
            for g, s in chains:
                fold(g, s, c, None)
            return carry

        lax.fori_loop(0, n_full, past, 0)

    l_min = l_ref[0, 0]
    for g, s in chains[1:]:
        l_min = jnp.minimum(l_min, l_ref[g, s])
    pl.when(jnp.min(l_min) < L_MIN)(safe_path)

    for g in range(ATT_GROUP):
        out_a = acc_ref[g, 0] / l_ref[g, 0]
        out_b = acc_ref[g, 1] / l_ref[g, 1]
        if moba:
            o = jnp.concatenate([out_a, out_b], axis=0).T
        else:
            lp = lam_ref[...]
            lam = (jnp.exp(jnp.sum(lp[0:1] * lp[1:2], axis=-1, keepdims=True))
                   - jnp.exp(jnp.sum(lp[2:3] * lp[3:4], axis=-1, keepdims=True)) + lam_init)
            o = (out_a - lam * out_b).T
            o = _rmsnorm_rows(o, subln_ref[...]) * (1.0 - lam_init)
        o_ref[0, :, g * PAIR:(g + 1) * PAIR] = o.astype(o_ref.dtype)


def _attention(qt, ka, vt, slope_table, lam_params, subln_w, *, moba, lam_init):
    b, nq, d, _ = qt.shape
    t = nq * ATT_TILE
    n_blocks = t // MOBA_BLOCK
    assert ATT_TILE == MOBA_BLOCK and n_blocks <= AUG_BLOCKS
    assert AUG_POS + 2 * AUG_SLOPE_PIECES <= AUG_BLOCKS + AUG_ROWS <= HEAD_DIM
    gl = ATT_GROUP * PAIR
    return pl.pallas_call(
        functools.partial(_attention_kernel, moba=moba, lam_init=lam_init),
        grid=(b, N_PAIRS // ATT_GROUP, nq),
        in_specs=[
            pl.BlockSpec(memory_space=pltpu.SMEM),
            pl.BlockSpec((1, 1, gl, ATT_TILE), lambda bi, p, i: (bi, i, p, 0)),
            pl.BlockSpec((2, None, t, gl), lambda bi, p, i: (0, bi, 0, p)),
            pl.BlockSpec((1, t // ATT_CHUNK, gl, ATT_CHUNK), lambda bi, p, i: (bi, 0, p, 0)),
            pl.BlockSpec((4, HEAD_DIM), lambda bi, p, i: (0, 0)),
            pl.BlockSpec((1, PAIR), lambda bi, p, i: (0, 0)),
        ],
        out_specs=pl.BlockSpec((1, ATT_TILE, gl), lambda bi, p, i: (bi, i, p)),
        out_shape=jax.ShapeDtypeStruct((b, t, d), BF16),
        scratch_shapes=[
            pltpu.VMEM((ATT_GROUP, 2, PAIR, ATT_TILE), BF16),
            pltpu.VMEM((ATT_GROUP, 2, 1, ATT_TILE), F32),
            pltpu.VMEM((ATT_GROUP, 2, 1, ATT_TILE), F32),
            pltpu.VMEM((ATT_GROUP, 2, HEAD_DIM if moba else PAIR, ATT_TILE), F32),
            pltpu.VMEM((ATT_GROUP, AUG_BLOCKS, PAIR), F32),
            pltpu.VMEM((ATT_GROUP, 2, 8, PAIR), F32),
            pltpu.VMEM((2, ATT_GROUP, 2, ATT_CHUNK, ATT_TILE), BF16),
            pltpu.VMEM((ATT_CHUNK, ATT_TILE), BF16),
            pltpu.VMEM((ATT_GROUP, 2, 1, ATT_TILE), F32),
        ],
        compiler_params=pltpu.CompilerParams(
            dimension_semantics=("arbitrary", "arbitrary", "arbitrary"),
            vmem_limit_bytes=V7X_VMEM_LIMIT_BYTES),
        name="moba_attention" if moba else "diff_attention",
    )(slope_table, qt, ka.reshape(2, b, t, d), vt, lam_params, subln_w.reshape(1, PAIR))


def kernel(x, attn_norm, w_in, w_out, diff_lambda, diff_subln, mlp_norm, w_ff1, w_ff2, final_norm):
    b, t, d = x.shape
    h = x.reshape(b * t, d)
    wq_t = jnp.swapaxes(w_in[:, :, :d], 1, 2).astype(BF16)
    wk = w_in[:, :, d:2 * d].astype(BF16)
    wv_t = jnp.swapaxes(w_in[:, :, 2 * d:], 1, 2).astype(BF16)
    w_out_b = w_out.astype(BF16)
    w_ff1_b, w_ff2_b = w_ff1.astype(BF16), w_ff2.astype(BF16)
    q_scale = HEAD_DIM ** -0.5 * LOG2E
    for i in range(DEPTH):
        moba = i % N_MIXERS == 1
        li = i // N_MIXERS
        qt, ka, vt = _qkv_projection(h, attn_norm[i], wq_t[i], wk[i], wv_t[i],
                                     batch=b, q_scale=q_scale)
        slope_table = jnp.asarray(_alibi_slope_pieces(MOBA_HEADS if moba else DIFF_HEADS))
        mix = _attention(qt, ka, vt, slope_table, diff_lambda[li], diff_subln[li],
                         moba=moba, lam_init=_lambda_init(i))
        h = _matmul_residual(mix.reshape(b * t, d), w_out_b[i], h, final_norm,
                             final_norm=False, name="attn_out")
        u = _norm_matmul(h, mlp_norm[i], w_ff1_b[i])
        h = _matmul_residual(u, w_ff2_b[i], h, final_norm,
                             final_norm=(i == DEPTH - 1), name="mlp_down")
    return h.reshape(b, t, d)
```

```python
import functools
import math

import jax
import jax.numpy as jnp
import numpy as np
from jax import lax
from jax.experimental import pallas as pl
from jax.experimental.pallas import tpu as pltpu

D_MODEL = 1024
DEPTH = 2
N_MIXERS = 2
DIFF_HEADS = 8
MOBA_HEADS = 16
HEAD_DIM = 64
PAIR = 2 * HEAD_DIM
N_PAIRS = D_MODEL // PAIR
MOBA_BLOCK = 256
MOBA_TOPK = 3
D_FF = 4 * D_MODEL
RMS_EPS = 1e-6
NEG_INF = -1e30
LOG2E = math.log2(math.e)

V7X_VMEM_LIMIT_BYTES = 56 * 1024 * 1024

ROW_TILE = 512
COL_TILE = 1024
KEY_TILE = MOBA_BLOCK
ATT_CHUNK_TILES = 2
ATT_CHUNK = ATT_CHUNK_TILES * KEY_TILE
ATT_TILE = ATT_CHUNK
ATT_GROUP = 2

AUG_BASE = (HEAD_DIM, 0)
AUG_BLOCKS = 32
AUG_POS = AUG_BLOCKS
AUG_SLOPE_PIECES = 3
AUG_ROWS = 16

BF16 = jnp.bfloat16
F32 = jnp.float32


def _alibi_slope_pieces(n_heads):
    slopes = np.array([2.0 ** (-8.0 * (h + 1) / n_heads) for h in range(n_heads)], np.float64)
    rest = slopes * LOG2E
    pieces = []
    for _ in range(AUG_SLOPE_PIECES):
        p = rest.astype(np.float32).astype(BF16).astype(np.float64)
        pieces.append(p)
        rest = rest - p
    pieces = np.stack(pieces, axis=1)
    table = np.concatenate([pieces, pieces.sum(axis=1, keepdims=True)], axis=1)
    return table.astype(np.float32).reshape(-1)


def _lambda_init(layer_idx):
    return 0.8 - 0.6 * math.exp(-0.3 * layer_idx)


def _rmsnorm_rows(x, g):
    return x * lax.rsqrt(jnp.mean(x * x, axis=-1, keepdims=True) + RMS_EPS) * g


_NT = (((1,), (1,)), ((), ()))


def _qkv_kernel(x_ref, g_ref, wqt_ref, wk_ref, wvt_ref, qt_ref, ka_ref, vt_ref, *,
                q_scale, steps_per_seq):
    xn = _rmsnorm_rows(x_ref[...], g_ref[...]).astype(BF16)
    for c0 in range(0, D_MODEL, 256):
        vt = lax.dot_general(wvt_ref[c0:c0 + 256, :], xn, _NT, preferred_element_type=F32)
        vt_ref[0, 0, c0:c0 + 256, :] = vt.astype(BF16)
        for j in range(ROW_TILE // ATT_TILE):
            qt = lax.dot_general(wqt_ref[c0:c0 + 256, :], xn[j * ATT_TILE:(j + 1) * ATT_TILE],
                                 _NT, preferred_element_type=F32)
            qt_ref[0, j, c0:c0 + 256, :] = (qt * q_scale).astype(BF16)
    k = jnp.dot(xn, wk_ref[...], preferred_element_type=F32).astype(BF16)
    lane = lax.broadcasted_iota(jnp.int32, (ROW_TILE, D_MODEL), 1) % PAIR
    row = lax.broadcasted_iota(jnp.int32, (ROW_TILE, D_MODEL), 0)
    tile = (pl.program_id(0) % steps_per_seq) * (ROW_TILE // KEY_TILE) + row // KEY_TILE
    pos = (row % KEY_TILE).astype(F32)
    tile_pos = ((tile % ATT_CHUNK_TILES) * KEY_TILE).astype(F32)
    for s in range(2):
        rel = lane - AUG_BASE[s]
        aug = jnp.where(rel == tile, 1.0, 0.0)
        aug = jnp.where(jnp.logical_and(rel >= AUG_POS, rel < AUG_POS + AUG_SLOPE_PIECES), pos, aug)
        aug = jnp.where(jnp.logical_and(rel >= AUG_POS + AUG_SLOPE_PIECES,
                                        rel < AUG_POS + 2 * AUG_SLOPE_PIECES), tile_pos, aug)
        own = jnp.logical_and(rel >= -HEAD_DIM, rel < 0) if s == 0 else rel >= HEAD_DIM
        ka_ref[s] = jnp.where(own, k, aug.astype(BF16))


def _qkv_projection(x, g, wq_t, wk, wv_t, *, batch, q_scale):
    m, d = x.shape
    t = m // batch
    spb = t // ROW_TILE
    spc = ATT_CHUNK // ROW_TILE
    assert t // KEY_TILE <= AUG_BLOCKS and ATT_CHUNK % ROW_TILE == 0 and t % ATT_CHUNK == 0
    assert ROW_TILE % ATT_TILE == 0
    w_spec = pl.BlockSpec((d, d), lambda i: (0, 0))
    return pl.pallas_call(
        functools.partial(_qkv_kernel, q_scale=q_scale, steps_per_seq=spb),
        grid=(m // ROW_TILE,),
        in_specs=[
            pl.BlockSpec((ROW_TILE, d), lambda i: (i, 0)),
            pl.BlockSpec((1, d), lambda i: (0, 0)),
            w_spec, w_spec, w_spec,
        ],
        out_specs=[
            pl.BlockSpec((1, ROW_TILE // ATT_TILE, d, ATT_TILE), lambda i: (i // spb, i % spb, 0, 0)),
            pl.BlockSpec((2, ROW_TILE, d), lambda i: (0, i, 0)),
            pl.BlockSpec((1, 1, d, ROW_TILE), lambda i: (i // spb, (i % spb) // spc, 0, i % spc)),
        ],
        out_shape=[
            jax.ShapeDtypeStruct((batch, t // ATT_TILE, d, ATT_TILE), BF16),
            jax.ShapeDtypeStruct((2, m, d), BF16),
            jax.ShapeDtypeStruct((batch, t // ATT_CHUNK, d, ATT_CHUNK), BF16),
        ],
        compiler_params=pltpu.CompilerParams(
            dimension_semantics=("arbitrary",), vmem_limit_bytes=V7X_VMEM_LIMIT_BYTES),
        name="qkv_projection",
    )(x, g.reshape(1, d), wq_t, wk, wv_t)


def _norm_matmul_kernel(x_ref, g_ref, w_ref, o_ref):
    xn = _rmsnorm_rows(x_ref[...], g_ref[...]).astype(BF16)
    for c0 in range(0, w_ref.shape[1], COL_TILE):
        acc = jnp.dot(xn, w_ref[:, c0:c0 + COL_TILE], preferred_element_type=F32)
        o_ref[:, c0:c0 + COL_TILE] = jnp.square(jnp.maximum(acc, 0.0)).astype(o_ref.dtype)


def _norm_matmul(x, g, w):
    m, k = x.shape
    n = w.shape[1]
    return pl.pallas_call(
        _norm_matmul_kernel,
        grid=(m // ROW_TILE,),
        in_specs=[
            pl.BlockSpec((ROW_TILE, k), lambda i: (i, 0)),
            pl.BlockSpec((1, k), lambda i: (0, 0)),
            pl.BlockSpec((k, n), lambda i: (0, 0)),
        ],
        out_specs=pl.BlockSpec((ROW_TILE, n), lambda i: (i, 0)),
        out_shape=jax.ShapeDtypeStruct((m, n), BF16),
        compiler_params=pltpu.CompilerParams(
            dimension_semantics=("arbitrary",), vmem_limit_bytes=V7X_VMEM_LIMIT_BYTES),
        name="mlp_up",
    )(x, g.reshape(1, k), w)


def _matmul_residual_kernel(a_ref, w_ref, r_ref, g_ref, o_ref, *, final_norm):
    h = r_ref[...] + jnp.dot(a_ref[...], w_ref[...], preferred_element_type=F32)
    if final_norm:
        h = _rmsnorm_rows(h, g_ref[...])
    o_ref[...] = h


def _matmul_residual(a, w, r, g, *, final_norm, name):
    m, k = a.shape
    n = w.shape[1]
    return pl.pallas_call(
        functools.partial(_matmul_residual_kernel, final_norm=final_norm),
        grid=(m // ROW_TILE,),
        in_specs=[
            pl.BlockSpec((ROW_TILE, k), lambda i: (i, 0)),
            pl.BlockSpec((k, n), lambda i: (0, 0)),
            pl.BlockSpec((ROW_TILE, n), lambda i: (i, 0)),
            pl.BlockSpec((1, n), lambda i: (0, 0)),
        ],
        out_specs=pl.BlockSpec((ROW_TILE, n), lambda i: (i, 0)),
        out_shape=jax.ShapeDtypeStruct((m, n), F32),
        compiler_params=pltpu.CompilerParams(
            dimension_semantics=("arbitrary",), vmem_limit_bytes=V7X_VMEM_LIMIT_BYTES),
        name=name,
    )(a, w, r, g.reshape(1, n))


L_MIN = 2.0 ** -60
UB_MARGIN = 1.01


def _attention_kernel(slope_ref, qt_ref, ka_ref, vt_ref, lam_ref, subln_ref, o_ref,
                      qa_ref, ub_ref, l_ref, acc_ref, km_ref, kn_ref, p_ref, ps_ref, m_ref, *, moba,
                      lam_init):
    group = pl.program_id(1)
    qi = pl.program_id(2)
    t = ATT_TILE
    n_tab = AUG_SLOPE_PIECES + 1
    n_blocks = ka_ref.shape[1] // MOBA_BLOCK
    n_chunks = ka_ref.shape[1] // ATT_CHUNK

    def head_of(g, s):
        pair = group * ATT_GROUP + g
        return 2 * pair + s if moba else pair

    @pl.when(qi == 0)
    def _():
        lane8 = lax.broadcasted_iota(jnp.int32, (8, PAIR), 1)
        for g in range(ATT_GROUP):
            for s in range(2):
                own = (lane8 < HEAD_DIM) if s == 0 else (lane8 >= HEAD_DIM)
                ones_own = jnp.where(own, 1.0, 0.0).astype(BF16)

                def chunk_norms(c, best):
                    r0 = pl.multiple_of(c * ATT_CHUNK, ATT_CHUNK)
                    k_c = ka_ref[s, pl.ds(r0, ATT_CHUNK), g * PAIR:(g + 1) * PAIR].astype(F32)
                    sq = lax.dot_general(ones_own, (k_c * k_c).astype(BF16), _NT,
                                         preferred_element_type=F32)
                    return jnp.maximum(best, sq)

                best = lax.fori_loop(0, n_chunks, chunk_norms, jnp.zeros((8, ATT_CHUNK), F32))
                kn_ref[g, s] = jnp.broadcast_to(
                    jnp.sqrt(jnp.max(best, axis=1, keepdims=True)), (8, PAIR))

    if moba:
        @pl.when(qi == 0)
        def _():
            lane_row = lax.broadcasted_iota(jnp.int32, (1, PAIR), 1)
            km_ref[...] = jnp.zeros_like(km_ref)
            for g in range(ATT_GROUP):
                for n in range(n_blocks):
                    rows = slice(n * MOBA_BLOCK, (n + 1) * MOBA_BLOCK)
                    lanes = slice(g * PAIR, (g + 1) * PAIR)
                    mean_a = jnp.mean(ka_ref[0, rows, lanes].astype(F32), axis=0, keepdims=True)
                    mean_b = jnp.mean(ka_ref[1, rows, lanes].astype(F32), axis=0, keepdims=True)
                    km_ref[g, n:n + 1, :] = jnp.where(lane_row < HEAD_DIM, mean_a, mean_b)

    row16 = lax.broadcasted_iota(jnp.int32, (AUG_ROWS, t), 0)
    own_rows = lax.broadcasted_iota(jnp.int32, (PAIR, t), 0) < HEAD_DIM
    blk = lax.broadcasted_iota(jnp.int32, (AUG_BLOCKS, t), 0)
    blk_f = blk.astype(F32)
    q_idx = lax.broadcasted_iota(jnp.int32, (1, t), 1)
    q_pos = q_idx.astype(F32)
    own_blk = qi * (t // MOBA_BLOCK) + q_idx // MOBA_BLOCK
    for g in range(ATT_GROUP):
        qt = qt_ref[0, 0, g * PAIR:(g + 1) * PAIR, :].astype(F32)
        for s in range(2):
            h = head_of(g, s)
            pos_rows = jnp.zeros((AUG_ROWS, t), F32)
            for piece in range(AUG_SLOPE_PIECES):
                hit = jnp.logical_or(row16 == piece, row16 == piece + AUG_SLOPE_PIECES)
                pos_rows = jnp.where(hit, slope_ref[h * n_tab + piece], pos_rows)
            if moba:
                q_own = jnp.where(own_rows if s == 0 else jnp.logical_not(own_rows), qt, 0.0)
                gate = jnp.dot(km_ref[g], q_own, precision=lax.Precision.HIGHEST,
                               preferred_element_type=F32)
                gt = jnp.where(blk < own_blk, gate, -jnp.inf)
                selected = blk == own_blk
                for pick in range(MOBA_TOPK):
                    mx = jnp.max(gt, axis=0, keepdims=True)
                    idx = jnp.min(jnp.where(gt == mx, blk_f, F32(AUG_BLOCKS)), axis=0,
                                  keepdims=True)
                    hit = jnp.logical_and(blk_f == idx, pick < own_blk)
                    selected = jnp.logical_or(selected, hit)
                    gt = jnp.where(blk_f == idx, -jnp.inf, gt)
                sel_rows = jnp.where(selected, 0.0, NEG_INF)
            else:
                sel_rows = jnp.zeros((AUG_BLOCKS, t), F32)
            own = qt[0:HEAD_DIM] if s == 0 else qt[HEAD_DIM:PAIR]
            q_norm = jnp.sqrt(jnp.sum(own * own, axis=0, keepdims=True))
            ub_ref[g, s] = (q_norm * kn_ref[g, s][0:1, 0:1] * UB_MARGIN
                            + slope_ref[h * n_tab + AUG_SLOPE_PIECES] * q_pos)
            tail = jnp.zeros((HEAD_DIM - AUG_BLOCKS - AUG_ROWS, t), F32)
            parts = [sel_rows, pos_rows, tail]
            parts = [own] + parts if s == 0 else parts + [own]
            qa_ref[g, s] = jnp.concatenate(parts, axis=0).astype(BF16)

    chains = [(g, s) for g in range(ATT_GROUP) for s in range(2)]
    n_full = qi

    def chunk_offset(g, s, c):
        return (slope_ref[head_of(g, s) * n_tab + AUG_SLOPE_PIECES]
                * ((c - qi) * ATT_CHUNK).astype(F32))

    v_rows = HEAD_DIM if moba else PAIR
    ones_rows = jnp.ones((AUG_ROWS, ATT_CHUNK), BF16)

    def key_rows(g, s, c):
        r0 = pl.multiple_of(c * ATT_CHUNK, ATT_CHUNK)
        return ka_ref[s, pl.ds(r0, ATT_CHUNK), g * PAIR:(g + 1) * PAIR]

    def value_rows(g, s, c):
        v0 = g * PAIR + (s * HEAD_DIM if moba else 0)
        return jnp.concatenate([vt_ref[0, c, v0:v0 + v_rows, :], ones_rows], axis=0)

    def probabilities(g, s, c, slot, causal=None):
        st = jnp.dot(key_rows(g, s, c), qa_ref[g, s], preferred_element_type=F32)
        if causal is not None:
            st = jnp.where(causal, st, NEG_INF)
        p_ref[slot, g, s] = jnp.exp2(st - (ub_ref[g, s] - chunk_offset(g, s, c))).astype(BF16)

    def values(g, s, c, slot):
        pv = jnp.dot(value_rows(g, s, c), p_ref[slot, g, s], preferred_element_type=F32)
        l_ref[g, s] += pv[v_rows:v_rows + 1]
        acc_ref[g, s] += pv[0:v_rows]

    def chunk_of(i):
        return jnp.where(i == 0, n_full, i - 1)

    def step(i, slot):
        for g, s in chains:
            probabilities(g, s, i, 1 - slot)
            values(g, s, chunk_of(i), slot)

    row_i = lax.broadcasted_iota(jnp.int32, (ATT_CHUNK, t), 0)
    col_i = lax.broadcasted_iota(jnp.int32, (ATT_CHUNK, t), 1)
    causal = row_i <= col_i
    l_ref[...] = jnp.zeros_like(l_ref)
    acc_ref[...] = jnp.zeros_like(acc_ref)
    for g, s in chains:
        probabilities(g, s, n_full, 0, causal)

    def two_steps(j, carry):
        step(2 * j, 0)
        step(2 * j + 1, 1)
        return carry

    lax.fori_loop(0, n_full // 2, two_steps, 0)

    @pl.when(n_full % 2 == 1)
    def _():
        step(n_full - 1, 0)
        for g, s in chains:
            values(g, s, chunk_of(n_full), 1)

    @pl.when(n_full % 2 == 0)
    def _():
        for g, s in chains:
            values(g, s, chunk_of(n_full), 0)

    def safe_path():
        m_ref[...] = jnp.full_like(m_ref, NEG_INF)
        l_ref[...] = jnp.zeros_like(l_ref)
        acc_ref[...] = jnp.zeros_like(acc_ref)

        def fold(g, s, c, mask):
            st = jnp.dot(key_rows(g, s, c), qa_ref[g, s], preferred_element_type=F32)
            if mask is not None:
                st = jnp.where(mask, st, NEG_INF)
            off = chunk_offset(g, s, c)
            m_prev = m_ref[g, s]
            m_next = jnp.maximum(m_prev, jnp.max(st, axis=0, keepdims=True) + off)
            alpha = jnp.exp2(m_prev - m_next)
            ps_ref[...] = jnp.exp2(st - (m_next - off)).astype(BF16)
            pv = jnp.dot(value_rows(g, s, c), ps_ref[...], preferred_element_type=F32)
            m_ref[g, s] = m_next
            l_ref[g, s] = alpha * l_ref[g, s] + pv[v_rows:v_rows + 1]
            acc_ref[g, s] = alpha * acc_ref[g, s] + pv[0:v_rows]

        for g, s in chains:
            fold(g, s, n_full, causal)

        def past(c, carry):
            for g, s in chains:
                fold(g, s, c, None)
            return carry

        lax.fori_loop(0, n_full, past, 0)

    l_min = l_ref[0, 0]
    for g, s in chains[1:]:
        l_min = jnp.minimum(l_min, l_ref[g, s])
    pl.when(jnp.min(l_min) < L_MIN)(safe_path)

    for g in range(ATT_GROUP):
        out_a = acc_ref[g, 0] / l_ref[g, 0]
        out_b = acc_ref[g, 1] / l_ref[g, 1]
        if moba:
            o = jnp.concatenate([out_a, out_b], axis=0).T
        else:
            lp = lam_ref[...]
            lam = (jnp.exp(jnp.sum(lp[0:1] * lp[1:2], axis=-1, keepdims=True))
                   - jnp.exp(jnp.sum(lp[2:3] * lp[3:4], axis=-1, keepdims=True)) + lam_init)
            o = (out_a - lam * out_b).T
            o = _rmsnorm_rows(o, subln_ref[...]) * (1.0 - lam_init)
        o_ref[0, :, g * PAIR:(g + 1) * PAIR] = o.astype(o_ref.dtype)


def _attention(qt, ka, vt, slope_table, lam_params, subln_w, *, moba, lam_init):
    b, nq, d, _ = qt.shape
    t = nq * ATT_TILE
    n_blocks = t // MOBA_BLOCK
    assert ATT_TILE == ATT_CHUNK and ATT_TILE % MOBA_BLOCK == 0 and n_blocks <= AUG_BLOCKS
    assert AUG_POS + 2 * AUG_SLOPE_PIECES <= AUG_BLOCKS + AUG_ROWS <= HEAD_DIM
    gl = ATT_GROUP * PAIR
    return pl.pallas_call(
        functools.partial(_attention_kernel, moba=moba, lam_init=lam_init),
        grid=(b, N_PAIRS // ATT_GROUP, nq),
        in_specs=[
            pl.BlockSpec(memory_space=pltpu.SMEM),
            pl.BlockSpec((1, 1, gl, ATT_TILE), lambda bi, p, i: (bi, i, p, 0)),
            pl.BlockSpec((2, None, t, gl), lambda bi, p, i: (0, bi, 0, p)),
            pl.BlockSpec((1, t // ATT_CHUNK, gl, ATT_CHUNK), lambda bi, p, i: (bi, 0, p, 0)),
            pl.BlockSpec((4, HEAD_DIM), lambda bi, p, i: (0, 0)),
            pl.BlockSpec((1, PAIR), lambda bi, p, i: (0, 0)),
        ],
        out_specs=pl.BlockSpec((1, ATT_TILE, gl), lambda bi, p, i: (bi, i, p)),
        out_shape=jax.ShapeDtypeStruct((b, t, d), BF16),
        scratch_shapes=[
            pltpu.VMEM((ATT_GROUP, 2, PAIR, ATT_TILE), BF16),
            pltpu.VMEM((ATT_GROUP, 2, 1, ATT_TILE), F32),
            pltpu.VMEM((ATT_GROUP, 2, 1, ATT_TILE), F32),
            pltpu.VMEM((ATT_GROUP, 2, HEAD_DIM if moba else PAIR, ATT_TILE), F32),
            pltpu.VMEM((ATT_GROUP, AUG_BLOCKS, PAIR), F32),
            pltpu.VMEM((ATT_GROUP, 2, 8, PAIR), F32),
            pltpu.VMEM((2, ATT_GROUP, 2, ATT_CHUNK, ATT_TILE), BF16),
            pltpu.VMEM((ATT_CHUNK, ATT_TILE), BF16),
            pltpu.VMEM((ATT_GROUP, 2, 1, ATT_TILE), F32),
        ],
        compiler_params=pltpu.CompilerParams(
            dimension_semantics=("arbitrary", "arbitrary", "arbitrary"),
            vmem_limit_bytes=V7X_VMEM_LIMIT_BYTES),
        name="moba_attention" if moba else "diff_attention",
    )(slope_table, qt, ka.reshape(2, b, t, d), vt, lam_params, subln_w.reshape(1, PAIR))


def kernel(x, attn_norm, w_in, w_out, diff_lambda, diff_subln, mlp_norm, w_ff1, w_ff2, final_norm):
    b, t, d = x.shape
    h = x.reshape(b * t, d)
    wq_t = jnp.swapaxes(w_in[:, :, :d], 1, 2).astype(BF16)
    wk = w_in[:, :, d:2 * d].astype(BF16)
    wv_t = jnp.swapaxes(w_in[:, :, 2 * d:], 1, 2).astype(BF16)
    w_out_b = w_out.astype(BF16)
    w_ff1_b, w_ff2_b = w_ff1.astype(BF16), w_ff2.astype(BF16)
    q_scale = HEAD_DIM ** -0.5 * LOG2E
    for i in range(DEPTH):
        moba = i % N_MIXERS == 1
        li = i // N_MIXERS
        qt, ka, vt = _qkv_projection(h, attn_norm[i], wq_t[i], wk[i], wv_t[i],
                                     batch=b, q_scale=q_scale)
        slope_table = jnp.asarray(_alibi_slope_pieces(MOBA_HEADS if moba else DIFF_HEADS))
        mix = _attention(qt, ka, vt, slope_table, diff_lambda[li], diff_subln[li],
                         moba=moba, lam_init=_lambda_init(i))
        h = _matmul_residual(mix.reshape(b * t, d), w_out_b[i], h, final_norm,
                             final_norm=False, name="attn_out")
        u = _norm_matmul(h, mlp_norm[i], w_ff1_b[i])
        h = _matmul_residual(u, w_ff2_b[i], h, final_norm,
                             final_norm=(i == DEPTH - 1), name="mlp_down")
    return h.reshape(b, t, d)
```

```python
import functools
import math

import jax
import jax.numpy as jnp
import numpy as np
from jax import lax
from jax.experimental import pallas as pl
from jax.experimental.pallas import tpu as pltpu

D_MODEL = 1024
DEPTH = 2
N_MIXERS = 2
DIFF_HEADS = 8
MOBA_HEADS = 16
HEAD_DIM = 64
PAIR = 2 * HEAD_DIM
N_PAIRS = D_MODEL // PAIR
MOBA_BLOCK = 256
MOBA_TOPK = 3
D_FF = 4 * D_MODEL
RMS_EPS = 1e-6
NEG_INF = -1e30
LOG2E = math.log2(math.e)

V7X_VMEM_LIMIT_BYTES = 56 * 1024 * 1024

ROW_TILE = 512
COL_TILE = 1024
COL_CHUNK = 256
KEY_TILE = MOBA_BLOCK
ATT_CHUNK_TILES = 2
ATT_CHUNK = ATT_CHUNK_TILES * KEY_TILE
ATT_TILE = ATT_CHUNK
ATT_GROUP = 2

AUG_BASE = (HEAD_DIM, 0)
AUG_BLOCKS = 32
AUG_POS = AUG_BLOCKS
AUG_SLOPE_PIECES = 3
AUG_ROWS = 16

BF16 = jnp.bfloat16
F32 = jnp.float32


def _alibi_slope_pieces(n_heads):
    slopes = np.array([2.0 ** (-8.0 * (h + 1) / n_heads) for h in range(n_heads)], np.float64)
    rest = slopes * LOG2E
    pieces = []
    for _ in range(AUG_SLOPE_PIECES):
        p = rest.astype(np.float32).astype(BF16).astype(np.float64)
        pieces.append(p)
        rest = rest - p
    pieces = np.stack(pieces, axis=1)
    table = np.concatenate([pieces, pieces.sum(axis=1, keepdims=True)], axis=1)
    return table.astype(np.float32).reshape(-1)


def _lambda_init(layer_idx):
    return 0.8 - 0.6 * math.exp(-0.3 * layer_idx)


def _rmsnorm_rows(x, g):
    return x * lax.rsqrt(jnp.mean(x * x, axis=-1, keepdims=True) + RMS_EPS) * g


_NT = (((1,), (1,)), ((), ()))


def _layer_weight_spec(w, layer):
    return pl.BlockSpec((None,) + w.shape[1:], lambda i: (layer, 0, 0),
                        pipeline_mode=pl.Buffered(1))


def _cast_columns(dst_ref, src_ref, c0, c1, *, transpose=False):
    for j in range(c0, c1, COL_CHUNK):
        blk = src_ref[:, j:j + COL_CHUNK]
        if transpose:
            dst_ref[j - c0:j - c0 + COL_CHUNK, :] = blk.T.astype(BF16)
        else:
            dst_ref[:, j - c0:j - c0 + COL_CHUNK] = blk.astype(BF16)


def _qkv_kernel(x_ref, g_ref, w_ref, qt_ref, ka_ref, vt_ref, wqt_ref, wk_ref, wvt_ref, *,
                q_scale, steps_per_seq):
    @pl.when(pl.program_id(0) == 0)
    def _():
        _cast_columns(wqt_ref, w_ref, 0, D_MODEL, transpose=True)
        _cast_columns(wk_ref, w_ref, D_MODEL, 2 * D_MODEL)
        _cast_columns(wvt_ref, w_ref, 2 * D_MODEL, 3 * D_MODEL, transpose=True)

    xn = _rmsnorm_rows(x_ref[...], g_ref[...]).astype(BF16)
    for c0 in range(0, D_MODEL, 256):
        vt = lax.dot_general(wvt_ref[c0:c0 + 256, :], xn, _NT, preferred_element_type=F32)
        vt_ref[0, 0, c0:c0 + 256, :] = vt.astype(BF16)
        for j in range(ROW_TILE // ATT_TILE):
            qt = lax.dot_general(wqt_ref[c0:c0 + 256, :], xn[j * ATT_TILE:(j + 1) * ATT_TILE],
                                 _NT, preferred_element_type=F32)
            qt_ref[0, j, c0:c0 + 256, :] = (qt * q_scale).astype(BF16)
    k = jnp.dot(xn, wk_ref[...], preferred_element_type=F32).astype(BF16)
    lane = lax.broadcasted_iota(jnp.int32, (ROW_TILE, PAIR), 1)
    row = lax.broadcasted_iota(jnp.int32, (ROW_TILE, PAIR), 0)
    tile = (pl.program_id(0) % steps_per_seq) * (ROW_TILE // KEY_TILE) + row // KEY_TILE
    pos = (row % KEY_TILE).astype(F32)
    tile_pos = ((tile % ATT_CHUNK_TILES) * KEY_TILE).astype(F32)
    for s in range(2):
        rel = lane - AUG_BASE[s]
        aug = jnp.where(rel == tile, 1.0, 0.0)
        aug = jnp.where(jnp.logical_and(rel >= AUG_POS, rel < AUG_POS + AUG_SLOPE_PIECES), pos, aug)
        aug = jnp.where(jnp.logical_and(rel >= AUG_POS + AUG_SLOPE_PIECES,
                                        rel < AUG_POS + 2 * AUG_SLOPE_PIECES), tile_pos, aug)
        aug = aug.astype(BF16)
        own = jnp.logical_and(rel >= -HEAD_DIM, rel < 0) if s == 0 else rel >= HEAD_DIM
        for c0 in range(0, D_MODEL, PAIR):
            ka_ref[s, :, c0:c0 + PAIR] = jnp.where(own, k[:, c0:c0 + PAIR], aug)


def _qkv_projection(x, g, w_in, layer, *, batch, q_scale):
    m, d = x.shape
    t = m // batch
    spb = t // ROW_TILE
    spc = ATT_CHUNK // ROW_TILE
    assert t // KEY_TILE <= AUG_BLOCKS and ATT_CHUNK % ROW_TILE == 0 and t % ATT_CHUNK == 0
    assert ROW_TILE % ATT_TILE == 0
    return pl.pallas_call(
        functools.partial(_qkv_kernel, q_scale=q_scale, steps_per_seq=spb),
        grid=(m // ROW_TILE,),
        in_specs=[
            pl.BlockSpec((ROW_TILE, d), lambda i: (i, 0)),
            pl.BlockSpec((1, d), lambda i: (0, 0)),
            _layer_weight_spec(w_in, layer),
        ],
        out_specs=[
            pl.BlockSpec((1, ROW_TILE // ATT_TILE, d, ATT_TILE), lambda i: (i // spb, i % spb, 0, 0)),
            pl.BlockSpec((2, ROW_TILE, d), lambda i: (0, i, 0)),
            pl.BlockSpec((1, 1, d, ROW_TILE), lambda i: (i // spb, (i % spb) // spc, 0, i % spc)),
        ],
        out_shape=[
            jax.ShapeDtypeStruct((batch, t // ATT_TILE, d, ATT_TILE), BF16),
            jax.ShapeDtypeStruct((2, m, d), BF16),
            jax.ShapeDtypeStruct((batch, t // ATT_CHUNK, d, ATT_CHUNK), BF16),
        ],
        scratch_shapes=[pltpu.VMEM((d, d), BF16)] * 3,
        compiler_params=pltpu.CompilerParams(
            dimension_semantics=("arbitrary",), vmem_limit_bytes=V7X_VMEM_LIMIT_BYTES),
        name="qkv_projection",
    )(x, g.reshape(1, d), w_in)


def _attn_out_mlp_up_kernel(a_ref, wo_ref, r_ref, g_ref, w1_ref, h_ref, u_ref, wo_b_ref, w1_b_ref):
    @pl.when(pl.program_id(0) == 0)
    def _():
        _cast_columns(wo_b_ref, wo_ref, 0, wo_ref.shape[1])
        _cast_columns(w1_b_ref, w1_ref, 0, w1_ref.shape[1])

    h = r_ref[...] + jnp.dot(a_ref[...], wo_b_ref[...], preferred_element_type=F32)
    h_ref[...] = h
    xn = _rmsnorm_rows(h, g_ref[...]).astype(BF16)
    for c0 in range(0, w1_b_ref.shape[1], COL_TILE):
        acc = jnp.dot(xn, w1_b_ref[:, c0:c0 + COL_TILE], preferred_element_type=F32)
        u_ref[:, c0:c0 + COL_TILE] = jnp.square(jnp.maximum(acc, 0.0)).astype(u_ref.dtype)


def _attn_out_mlp_up(a, w_out, r, g, w_ff1, layer):
    m, d = r.shape
    n = w_ff1.shape[2]
    rows = lambda width: pl.BlockSpec((ROW_TILE, width), lambda i: (i, 0))
    return pl.pallas_call(
        _attn_out_mlp_up_kernel,
        grid=(m // ROW_TILE,),
        in_specs=[rows(d), _layer_weight_spec(w_out, layer), rows(d),
                  pl.BlockSpec((1, d), lambda i: (0, 0)), _layer_weight_spec(w_ff1, layer)],
        out_specs=[rows(d), rows(n)],
        out_shape=[jax.ShapeDtypeStruct((m, d), F32), jax.ShapeDtypeStruct((m, n), BF16)],
        scratch_shapes=[pltpu.VMEM((d, d), BF16), pltpu.VMEM((d, n), BF16)],
        compiler_params=pltpu.CompilerParams(
            dimension_semantics=("arbitrary",), vmem_limit_bytes=V7X_VMEM_LIMIT_BYTES),
        name="attn_out_mlp_up",
    )(a, w_out, r, g.reshape(1, d), w_ff1)


def _mlp_down_kernel(a_ref, w_ref, r_ref, g_ref, o_ref, w_b_ref, *, final_norm):
    @pl.when(pl.program_id(0) == 0)
    def _():
        _cast_columns(w_b_ref, w_ref, 0, w_ref.shape[1])

    h = r_ref[...] + jnp.dot(a_ref[...], w_b_ref[...], preferred_element_type=F32)
    if final_norm:
        h = _rmsnorm_rows(h, g_ref[...])
    o_ref[...] = h


def _mlp_down(a, w_ff2, r, g, layer, *, final_norm):
    m, k = a.shape
    n = w_ff2.shape[2]
    return pl.pallas_call(
        functools.partial(_mlp_down_kernel, final_norm=final_norm),
        grid=(m // ROW_TILE,),
        in_specs=[
            pl.BlockSpec((ROW_TILE, k), lambda i: (i, 0)),
            _layer_weight_spec(w_ff2, layer),
            pl.BlockSpec((ROW_TILE, n), lambda i: (i, 0)),
            pl.BlockSpec((1, n), lambda i: (0, 0)),
        ],
        out_specs=pl.BlockSpec((ROW_TILE, n), lambda i: (i, 0)),
        out_shape=jax.ShapeDtypeStruct((m, n), F32),
        scratch_shapes=[pltpu.VMEM((k, n), BF16)],
        compiler_params=pltpu.CompilerParams(
            dimension_semantics=("arbitrary",), vmem_limit_bytes=V7X_VMEM_LIMIT_BYTES),
        name="mlp_down",
    )(a, w_ff2, r, g.reshape(1, n))


L_MIN = 2.0 ** -60
UB_MARGIN = 1.01


def _attention_kernel(slope_ref, qt_ref, ka_ref, vt_ref, lam_ref, subln_ref, o_ref,
                      qa_ref, ub_ref, l_ref, acc_ref, km_ref, kn_ref, p_ref, ps_ref, m_ref, *, moba,
                      lam_init):
    group = pl.program_id(1)
    qi = pl.program_id(2)
    t = ATT_TILE
    n_tab = AUG_SLOPE_PIECES + 1
    n_blocks = ka_ref.shape[1] // MOBA_BLOCK
    n_chunks = ka_ref.shape[1] // ATT_CHUNK

    def head_of(g, s):
        pair = group * ATT_GROUP + g
        return 2 * pair + s if moba else pair

    @pl.when(qi == 0)
    def _():
        lane8 = lax.broadcasted_iota(jnp.int32, (8, PAIR), 1)
        for g in range(ATT_GROUP):
            for s in range(2):
                own = (lane8 < HEAD_DIM) if s == 0 else (lane8 >= HEAD_DIM)
                ones_own = jnp.where(own, 1.0, 0.0).astype(BF16)

                def chunk_norms(c, best):
                    r0 = pl.multiple_of(c * ATT_CHUNK, ATT_CHUNK)
                    k_c = ka_ref[s, pl.ds(r0, ATT_CHUNK), g * PAIR:(g + 1) * PAIR].astype(F32)
                    sq = lax.dot_general(ones_own, (k_c * k_c).astype(BF16), _NT,
                                         preferred_element_type=F32)
                    return jnp.maximum(best, sq)

                best = lax.fori_loop(0, n_chunks, chunk_norms, jnp.zeros((8, ATT_CHUNK), F32))
                kn_ref[g, s] = jnp.broadcast_to(
                    jnp.sqrt(jnp.max(best, axis=1, keepdims=True)), (8, PAIR))

    if moba:
        @pl.when(qi == 0)
        def _():
            lane_row = lax.broadcasted_iota(jnp.int32, (1, PAIR), 1)
            km_ref[...] = jnp.zeros_like(km_ref)
            for g in range(ATT_GROUP):
                for n in range(n_blocks):
                    rows = slice(n * MOBA_BLOCK, (n + 1) * MOBA_BLOCK)
                    lanes = slice(g * PAIR, (g + 1) * PAIR)
                    mean_a = jnp.mean(ka_ref[0, rows, lanes].astype(F32), axis=0, keepdims=True)
                    mean_b = jnp.mean(ka_ref[1, rows, lanes].astype(F32), axis=0, keepdims=True)
                    km_ref[g, n:n + 1, :] = jnp.where(lane_row < HEAD_DIM, mean_a, mean_b)

    row16 = lax.broadcasted_iota(jnp.int32, (AUG_ROWS, t), 0)
    own_rows = lax.broadcasted_iota(jnp.int32, (PAIR, t), 0) < HEAD_DIM
    blk = lax.broadcasted_iota(jnp.int32, (AUG_BLOCKS, t), 0)
    blk_f = blk.astype(F32)
    q_idx = lax.broadcasted_iota(jnp.int32, (1, t), 1)
    q_pos = q_idx.astype(F32)
    own_blk = qi * (t // MOBA_BLOCK) + q_idx // MOBA_BLOCK
    for g in range(ATT_GROUP):
        qt = qt_ref[0, 0, g * PAIR:(g + 1) * PAIR, :].astype(F32)
        for s in range(2):
            h = head_of(g, s)
            pos_rows = jnp.zeros((AUG_ROWS, t), F32)
            for piece in range(AUG_SLOPE_PIECES):
                hit = jnp.logical_or(row16 == piece, row16 == piece + AUG_SLOPE_PIECES)
                pos_rows = jnp.where(hit, slope_ref[h * n_tab + piece], pos_rows)
            if moba:
                q_own = jnp.where(own_rows if s == 0 else jnp.logical_not(own_rows), qt, 0.0)
                gate = jnp.dot(km_ref[g], q_own, precision=lax.Precision.HIGHEST,
                               preferred_element_type=F32)
                gt = jnp.where(blk < own_blk, gate, -jnp.inf)
                selected = blk == own_blk
                for pick in range(MOBA_TOPK):
                    mx = jnp.max(gt, axis=0, keepdims=True)
                    idx = jnp.min(jnp.where(gt == mx, blk_f, F32(AUG_BLOCKS)), axis=0,
                                  keepdims=True)
                    hit = jnp.logical_and(blk_f == idx, pick < own_blk)
                    selected = jnp.logical_or(selected, hit)
                    gt = jnp.where(blk_f == idx, -jnp.inf, gt)
                sel_rows = jnp.where(selected, 0.0, NEG_INF)
            else:
                sel_rows = jnp.zeros((AUG_BLOCKS, t), F32)
            own = qt[0:HEAD_DIM] if s == 0 else qt[HEAD_DIM:PAIR]
            q_norm = jnp.sqrt(jnp.sum(own * own, axis=0, keepdims=True))
            ub_ref[g, s] = (q_norm * kn_ref[g, s][0:1, 0:1] * UB_MARGIN
                            + slope_ref[h * n_tab + AUG_SLOPE_PIECES] * q_pos)
            tail = jnp.zeros((HEAD_DIM - AUG_BLOCKS - AUG_ROWS, t), F32)
            parts = [sel_rows, pos_rows, tail]
            parts = [own] + parts if s == 0 else parts + [own]
            qa_ref[g, s] = jnp.concatenate(parts, axis=0).astype(BF16)

    chains = [(g, s) for g in range(ATT_GROUP) for s in range(2)]
    n_full = qi

    def chunk_offset(g, s, c):
        return (slope_ref[head_of(g, s) * n_tab + AUG_SLOPE_PIECES]
                * ((c - qi) * ATT_CHUNK).astype(F32))

    v_rows = HEAD_DIM if moba else PAIR
    ones_rows = jnp.ones((AUG_ROWS, ATT_CHUNK), BF16)

    def key_rows(g, s, c):
        r0 = pl.multiple_of(c * ATT_CHUNK, ATT_CHUNK)
        return ka_ref[s, pl.ds(r0, ATT_CHUNK), g * PAIR:(g + 1) * PAIR]

    def value_rows(g, s, c):
        v0 = g * PAIR + (s * HEAD_DIM if moba else 0)
        return jnp.concatenate([vt_ref[0, c, v0:v0 + v_rows, :], ones_rows], axis=0)

    def probabilities(g, s, c, slot, causal=None):
        st = jnp.dot(key_rows(g, s, c), qa_ref[g, s], preferred_element_type=F32)
        if causal is not None:
            st = jnp.where(causal, st, NEG_INF)
        p = jnp.exp2(st - (ub_ref[g, s] - chunk_offset(g, s, c)))
        p_ref[slot, g, s] = p.astype(BF16)
        l_ref[g, s] += jnp.sum(p, axis=0, keepdims=True)

    def values(g, s, c, slot):
        v0 = g * PAIR + (s * HEAD_DIM if moba else 0)
        acc_ref[g, s] += jnp.dot(vt_ref[0, c, v0:v0 + v_rows, :], p_ref[slot, g, s],
                                 preferred_element_type=F32)

    def chunk_of(i):
        return jnp.where(i == 0, n_full, i - 1)

    def step(i, slot):
        for g, s in chains:
            probabilities(g, s, i, 1 - slot)
            values(g, s, chunk_of(i), slot)

    row_i = lax.broadcasted_iota(jnp.int32, (ATT_CHUNK, t), 0)
    col_i = lax.broadcasted_iota(jnp.int32, (ATT_CHUNK, t), 1)
    causal = row_i <= col_i
    l_ref[...] = jnp.zeros_like(l_ref)
    acc_ref[...] = jnp.zeros_like(acc_ref)
    for g, s in chains:
        probabilities(g, s, n_full, 0, causal)

    def two_steps(j, carry):
        step(2 * j, 0)
        step(2 * j + 1, 1)
        return carry

    lax.fori_loop(0, n_full // 2, two_steps, 0)

    @pl.when(n_full % 2 == 1)
    def _():
        step(n_full - 1, 0)
        for g, s in chains:
            values(g, s, chunk_of(n_full), 1)

    @pl.when(n_full % 2 == 0)
    def _():
        for g, s in chains:
            values(g, s, chunk_of(n_full), 0)

    def safe_path():
        m_ref[...] = jnp.full_like(m_ref, NEG_INF)
        l_ref[...] = jnp.zeros_like(l_ref)
        acc_ref[...] = jnp.zeros_like(acc_ref)

        def fold(g, s, c, mask):
            st = jnp.dot(key_rows(g, s, c), qa_ref[g, s], preferred_element_type=F32)
            if mask is not None:
                st = jnp.where(mask, st, NEG_INF)
            off = chunk_offset(g, s, c)
            m_prev = m_ref[g, s]
            m_next = jnp.maximum(m_prev, jnp.max(st, axis=0, keepdims=True) + off)
            alpha = jnp.exp2(m_prev - m_next)
            ps_ref[...] = jnp.exp2(st - (m_next - off)).astype(BF16)
            pv = jnp.dot(value_rows(g, s, c), ps_ref[...], preferred_element_type=F32)
            m_ref[g, s] = m_next
            l_ref[g, s] = alpha * l_ref[g, s] + pv[v_rows:v_rows + 1]
            acc_ref[g, s] = alpha * acc_ref[g, s] + pv[0:v_rows]

        for g, s in chains:
            fold(g, s, n_full, causal)

        def past(c, carry):
            for g, s in chains:
                fold(g, s, c, None)
            return carry

        lax.fori_loop(0, n_full, past, 0)

    l_min = l_ref[0, 0]
    for g, s in chains[1:]:
        l_min = jnp.minimum(l_min, l_ref[g, s])
    pl.when(jnp.min(l_min) < L_MIN)(safe_path)

    for g in range(ATT_GROUP):
        out_a = acc_ref[g, 0] / l_ref[g, 0]
        out_b = acc_ref[g, 1] / l_ref[g, 1]
        if moba:
            o = jnp.concatenate([out_a, out_b], axis=0).T
        else:
            lp = lam_ref[...]
            lam = (jnp.exp(jnp.sum(lp[0:1] * lp[1:2], axis=-1, keepdims=True))
                   - jnp.exp(jnp.sum(lp[2:3] * lp[3:4], axis=-1, keepdims=True)) + lam_init)
            o = (out_a - lam * out_b).T
            o = _rmsnorm_rows(o, subln_ref[...]) * (1.0 - lam_init)
        o_ref[0, :, g * PAIR:(g + 1) * PAIR] = o.astype(o_ref.dtype)


def _attention(qt, ka, vt, slope_table, lam_params, subln_w, *, moba, lam_init):
    b, nq, d, _ = qt.shape
    t = nq * ATT_TILE
    n_blocks = t // MOBA_BLOCK
    assert ATT_TILE == ATT_CHUNK and ATT_TILE % MOBA_BLOCK == 0 and n_blocks <= AUG_BLOCKS
    assert AUG_POS + 2 * AUG_SLOPE_PIECES <= AUG_BLOCKS + AUG_ROWS <= HEAD_DIM
    gl = ATT_GROUP * PAIR
    return pl.pallas_call(
        functools.partial(_attention_kernel, moba=moba, lam_init=lam_init),
        grid=(b, N_PAIRS // ATT_GROUP, nq),
        in_specs=[
            pl.BlockSpec(memory_space=pltpu.SMEM),
            pl.BlockSpec((1, 1, gl, ATT_TILE), lambda bi, p, i: (bi, i, p, 0)),
            pl.BlockSpec((2, None, t, gl), lambda bi, p, i: (0, bi, 0, p)),
            pl.BlockSpec((1, t // ATT_CHUNK, gl, ATT_CHUNK), lambda bi, p, i: (bi, 0, p, 0)),
            pl.BlockSpec((4, HEAD_DIM), lambda bi, p, i: (0, 0)),
            pl.BlockSpec((1, PAIR), lambda bi, p, i: (0, 0)),
        ],
        out_specs=pl.BlockSpec((1, ATT_TILE, gl), lambda bi, p, i: (bi, i, p)),
        out_shape=jax.ShapeDtypeStruct((b, t, d), BF16),
        scratch_shapes=[
            pltpu.VMEM((ATT_GROUP, 2, PAIR, ATT_TILE), BF16),
            pltpu.VMEM((ATT_GROUP, 2, 1, ATT_TILE), F32),
            pltpu.VMEM((ATT_GROUP, 2, 1, ATT_TILE), F32),
            pltpu.VMEM((ATT_GROUP, 2, HEAD_DIM if moba else PAIR, ATT_TILE), F32),
            pltpu.VMEM((ATT_GROUP, AUG_BLOCKS, PAIR), F32),
            pltpu.VMEM((ATT_GROUP, 2, 8, PAIR), F32),
            pltpu.VMEM((2, ATT_GROUP, 2, ATT_CHUNK, ATT_TILE), BF16),
            pltpu.VMEM((ATT_CHUNK, ATT_TILE), BF16),
            pltpu.VMEM((ATT_GROUP, 2, 1, ATT_TILE), F32),
        ],
        compiler_params=pltpu.CompilerParams(
            dimension_semantics=("arbitrary", "arbitrary", "arbitrary"),
            vmem_limit_bytes=V7X_VMEM_LIMIT_BYTES),
        name="moba_attention" if moba else "diff_attention",
    )(slope_table, qt, ka.reshape(2, b, t, d), vt, lam_params, subln_w.reshape(1, PAIR))


def kernel(x, attn_norm, w_in, w_out, diff_lambda, diff_subln, mlp_norm, w_ff1, w_ff2, final_norm):
    b, t, d = x.shape
    h = x.reshape(b * t, d)
    q_scale = HEAD_DIM ** -0.5 * LOG2E
    for i in range(DEPTH):
        moba = i % N_MIXERS == 1
        li = i // N_MIXERS
        qt, ka, vt = _qkv_projection(h, attn_norm[i], w_in, i, batch=b, q_scale=q_scale)
        slope_table = jnp.asarray(_alibi_slope_pieces(MOBA_HEADS if moba else DIFF_HEADS))
        mix = _attention(qt, ka, vt, slope_table, diff_lambda[li], diff_subln[li],
                         moba=moba, lam_init=_lambda_init(i))
        h, u = _attn_out_mlp_up(mix.reshape(b * t, d), w_out, h, mlp_norm[i], w_ff1, i)
        h = _mlp_down(u, w_ff2, h, final_norm, i, final_norm=(i == DEPTH - 1))
    return h.reshape(b, t, d)
```

```python
import functools
import math

import jax
import jax.numpy as jnp
import numpy as np
from jax import lax
from jax.experimental import pallas as pl
from jax.experimental.pallas import tpu as pltpu

D_MODEL = 1024
DEPTH = 2
N_MIXERS = 2
DIFF_HEADS = 8
MOBA_HEADS = 16
HEAD_DIM = 64
PAIR = 2 * HEAD_DIM
N_PAIRS = D_MODEL // PAIR
MOBA_BLOCK = 256
MOBA_TOPK = 3
D_FF = 4 * D_MODEL
RMS_EPS = 1e-6
NEG_INF = -1e30
LOG2E = math.log2(math.e)

V7X_VMEM_LIMIT_BYTES = 56 * 1024 * 1024

ROW_TILE = 512
COL_TILE = 1024
COL_CHUNK = 256
KEY_TILE = MOBA_BLOCK
ATT_CHUNK_TILES = 2
ATT_CHUNK = ATT_CHUNK_TILES * KEY_TILE
ATT_TILE = ATT_CHUNK
ATT_GROUP = 2

AUG_BASE = (HEAD_DIM, 0)
AUG_BLOCKS = 32
AUG_POS = AUG_BLOCKS
AUG_SLOPE_PIECES = 3
AUG_ROWS = 16

BF16 = jnp.bfloat16
F32 = jnp.float32


def _alibi_slope_pieces(n_heads):
    slopes = np.array([2.0 ** (-8.0 * (h + 1) / n_heads) for h in range(n_heads)], np.float64)
    rest = slopes * LOG2E
    pieces = []
    for _ in range(AUG_SLOPE_PIECES):
        p = rest.astype(np.float32).astype(BF16).astype(np.float64)
        pieces.append(p)
        rest = rest - p
    pieces = np.stack(pieces, axis=1)
    table = np.concatenate([pieces, pieces.sum(axis=1, keepdims=True)], axis=1)
    return table.astype(np.float32).reshape(-1)


def _lambda_init(layer_idx):
    return 0.8 - 0.6 * math.exp(-0.3 * layer_idx)


def _rmsnorm_rows(x, g):
    return x * lax.rsqrt(jnp.mean(x * x, axis=-1, keepdims=True) + RMS_EPS) * g


_NT = (((1,), (1,)), ((), ()))


def _layer_weight_spec(w, layer):
    return pl.BlockSpec((None,) + w.shape[1:], lambda i: (layer, 0, 0),
                        pipeline_mode=pl.Buffered(1))


def _cast_columns(dst_ref, src_ref, c0, c1, *, transpose=False):
    for j in range(c0, c1, COL_CHUNK):
        blk = src_ref[:, j:j + COL_CHUNK]
        if transpose:
            dst_ref[j - c0:j - c0 + COL_CHUNK, :] = blk.T.astype(BF16)
        else:
            dst_ref[:, j - c0:j - c0 + COL_CHUNK] = blk.astype(BF16)


def _qkv_kernel(x_ref, g_ref, w_ref, qt_ref, ka_ref, vt_ref, wqt_ref, wk_ref, wvt_ref, *,
                q_scale, steps_per_seq):
    @pl.when(pl.program_id(0) == 0)
    def _():
        _cast_columns(wqt_ref, w_ref, 0, D_MODEL, transpose=True)
        _cast_columns(wk_ref, w_ref, D_MODEL, 2 * D_MODEL)
        _cast_columns(wvt_ref, w_ref, 2 * D_MODEL, 3 * D_MODEL, transpose=True)

    xn = _rmsnorm_rows(x_ref[...], g_ref[...]).astype(BF16)
    for c0 in range(0, D_MODEL, 256):
        vt = lax.dot_general(wvt_ref[c0:c0 + 256, :], xn, _NT, preferred_element_type=F32)
        vt_ref[0, 0, c0:c0 + 256, :] = vt.astype(BF16)
        for j in range(ROW_TILE // ATT_TILE):
            qt = lax.dot_general(wqt_ref[c0:c0 + 256, :], xn[j * ATT_TILE:(j + 1) * ATT_TILE],
                                 _NT, preferred_element_type=F32)
            qt_ref[0, j, c0:c0 + 256, :] = (qt * q_scale).astype(BF16)
    k = jnp.dot(xn, wk_ref[...], preferred_element_type=F32).astype(BF16)
    lane = lax.broadcasted_iota(jnp.int32, (ROW_TILE, PAIR), 1)
    row = lax.broadcasted_iota(jnp.int32, (ROW_TILE, PAIR), 0)
    tile = (pl.program_id(0) % steps_per_seq) * (ROW_TILE // KEY_TILE) + row // KEY_TILE
    pos = (row % KEY_TILE).astype(F32)
    tile_pos = ((tile % ATT_CHUNK_TILES) * KEY_TILE).astype(F32)
    for s in range(2):
        rel = lane - AUG_BASE[s]
        aug = jnp.where(rel == tile, 1.0, 0.0)
        aug = jnp.where(jnp.logical_and(rel >= AUG_POS, rel < AUG_POS + AUG_SLOPE_PIECES), pos, aug)
        aug = jnp.where(jnp.logical_and(rel >= AUG_POS + AUG_SLOPE_PIECES,
                                        rel < AUG_POS + 2 * AUG_SLOPE_PIECES), tile_pos, aug)
        aug = aug.astype(BF16)
        own = jnp.logical_and(rel >= -HEAD_DIM, rel < 0) if s == 0 else rel >= HEAD_DIM
        for c0 in range(0, D_MODEL, PAIR):
            ka_ref[s, :, c0:c0 + PAIR] = jnp.where(own, k[:, c0:c0 + PAIR], aug)


def _qkv_projection(x, g, w_in, layer, *, batch, q_scale):
    m, d = x.shape
    t = m // batch
    spb = t // ROW_TILE
    spc = ATT_CHUNK // ROW_TILE
    assert t // KEY_TILE <= AUG_BLOCKS and ATT_CHUNK % ROW_TILE == 0 and t % ATT_CHUNK == 0
    assert ROW_TILE % ATT_TILE == 0
    return pl.pallas_call(
        functools.partial(_qkv_kernel, q_scale=q_scale, steps_per_seq=spb),
        grid=(m // ROW_TILE,),
        in_specs=[
            pl.BlockSpec((ROW_TILE, d), lambda i: (i, 0)),
            pl.BlockSpec((1, d), lambda i: (0, 0)),
            _layer_weight_spec(w_in, layer),
        ],
        out_specs=[
            pl.BlockSpec((1, ROW_TILE // ATT_TILE, d, ATT_TILE), lambda i: (i // spb, i % spb, 0, 0)),
            pl.BlockSpec((2, ROW_TILE, d), lambda i: (0, i, 0)),
            pl.BlockSpec((1, 1, d, ROW_TILE), lambda i: (i // spb, (i % spb) // spc, 0, i % spc)),
        ],
        out_shape=[
            jax.ShapeDtypeStruct((batch, t // ATT_TILE, d, ATT_TILE), BF16),
            jax.ShapeDtypeStruct((2, m, d), BF16),
            jax.ShapeDtypeStruct((batch, t // ATT_CHUNK, d, ATT_CHUNK), BF16),
        ],
        scratch_shapes=[pltpu.VMEM((d, d), BF16)] * 3,
        compiler_params=pltpu.CompilerParams(
            dimension_semantics=("arbitrary",), vmem_limit_bytes=V7X_VMEM_LIMIT_BYTES),
        name="qkv_projection",
    )(x, g.reshape(1, d), w_in)


def _attn_out_mlp_up_kernel(a_ref, wo_ref, r_ref, g_ref, w1_ref, h_ref, u_ref, wo_b_ref, w1_b_ref):
    @pl.when(pl.program_id(0) == 0)
    def _():
        _cast_columns(wo_b_ref, wo_ref, 0, wo_ref.shape[1])
        _cast_columns(w1_b_ref, w1_ref, 0, w1_ref.shape[1])

    h = r_ref[...] + lax.dot_general(a_ref[...], wo_b_ref[...], (((0,), (0,)), ((), ())),
                                     preferred_element_type=F32)
    h_ref[...] = h
    xn = _rmsnorm_rows(h, g_ref[...]).astype(BF16)
    for c0 in range(0, w1_b_ref.shape[1], COL_TILE):
        acc = jnp.dot(xn, w1_b_ref[:, c0:c0 + COL_TILE], preferred_element_type=F32)
        u_ref[:, c0:c0 + COL_TILE] = jnp.square(jnp.maximum(acc, 0.0)).astype(u_ref.dtype)


def _attn_out_mlp_up(a_t, w_out, r, g, w_ff1, layer):
    m, d = r.shape
    n = w_ff1.shape[2]
    spb = a_t.shape[1]
    assert a_t.shape[2:] == (d, ROW_TILE)
    rows = lambda width: pl.BlockSpec((ROW_TILE, width), lambda i: (i, 0))
    return pl.pallas_call(
        _attn_out_mlp_up_kernel,
        grid=(m // ROW_TILE,),
        in_specs=[pl.BlockSpec((None, None, d, ROW_TILE), lambda i: (i // spb, i % spb, 0, 0)),
                  _layer_weight_spec(w_out, layer), rows(d),
                  pl.BlockSpec((1, d), lambda i: (0, 0)), _layer_weight_spec(w_ff1, layer)],
        out_specs=[rows(d), rows(n)],
        out_shape=[jax.ShapeDtypeStruct((m, d), F32), jax.ShapeDtypeStruct((m, n), BF16)],
        scratch_shapes=[pltpu.VMEM((d, d), BF16), pltpu.VMEM((d, n), BF16)],
        compiler_params=pltpu.CompilerParams(
            dimension_semantics=("arbitrary",), vmem_limit_bytes=V7X_VMEM_LIMIT_BYTES),
        name="attn_out_mlp_up",
    )(a_t, w_out, r, g.reshape(1, d), w_ff1)


def _mlp_down_kernel(a_ref, w_ref, r_ref, g_ref, o_ref, w_b_ref, *, final_norm):
    @pl.when(pl.program_id(0) == 0)
    def _():
        _cast_columns(w_b_ref, w_ref, 0, w_ref.shape[1])

    h = r_ref[...] + jnp.dot(a_ref[...], w_b_ref[...], preferred_element_type=F32)
    if final_norm:
        h = _rmsnorm_rows(h, g_ref[...])
    o_ref[...] = h


def _mlp_down(a, w_ff2, r, g, layer, *, final_norm):
    m, k = a.shape
    n = w_ff2.shape[2]
    return pl.pallas_call(
        functools.partial(_mlp_down_kernel, final_norm=final_norm),
        grid=(m // ROW_TILE,),
        in_specs=[
            pl.BlockSpec((ROW_TILE, k), lambda i: (i, 0)),
            _layer_weight_spec(w_ff2, layer),
            pl.BlockSpec((ROW_TILE, n), lambda i: (i, 0)),
            pl.BlockSpec((1, n), lambda i: (0, 0)),
        ],
        out_specs=pl.BlockSpec((ROW_TILE, n), lambda i: (i, 0)),
        out_shape=jax.ShapeDtypeStruct((m, n), F32),
        scratch_shapes=[pltpu.VMEM((k, n), BF16)],
        compiler_params=pltpu.CompilerParams(
            dimension_semantics=("arbitrary",), vmem_limit_bytes=V7X_VMEM_LIMIT_BYTES),
        name="mlp_down",
    )(a, w_ff2, r, g.reshape(1, n))


SUM_ON_MXU = {False: False, True: True}
L_MIN = 2.0 ** -60
UB_MARGIN = 1.01


def _attention_kernel(slope_ref, qt_ref, ka_ref, vt_ref, lam_ref, subln_ref, o_ref,
                      qa_ref, ub_ref, l_ref, acc_ref, km_ref, kn_ref, p_ref, ps_ref, m_ref, *, moba,
                      lam_init):
    group = pl.program_id(1)
    qi = pl.program_id(2)
    t = ATT_TILE
    n_tab = AUG_SLOPE_PIECES + 1
    n_blocks = ka_ref.shape[1] // MOBA_BLOCK
    n_chunks = ka_ref.shape[1] // ATT_CHUNK

    def head_of(g, s):
        pair = group * ATT_GROUP + g
        return 2 * pair + s if moba else pair

    @pl.when(qi == 0)
    def _():
        lane8 = lax.broadcasted_iota(jnp.int32, (8, PAIR), 1)
        for g in range(ATT_GROUP):
            for s in range(2):
                own = (lane8 < HEAD_DIM) if s == 0 else (lane8 >= HEAD_DIM)
                ones_own = jnp.where(own, 1.0, 0.0).astype(BF16)

                def chunk_norms(c, best):
                    r0 = pl.multiple_of(c * ATT_CHUNK, ATT_CHUNK)
                    k_c = ka_ref[s, pl.ds(r0, ATT_CHUNK), g * PAIR:(g + 1) * PAIR].astype(F32)
                    sq = lax.dot_general(ones_own, (k_c * k_c).astype(BF16), _NT,
                                         preferred_element_type=F32)
                    return jnp.maximum(best, sq)

                best = lax.fori_loop(0, n_chunks, chunk_norms, jnp.zeros((8, ATT_CHUNK), F32))
                kn_ref[g, s] = jnp.broadcast_to(
                    jnp.sqrt(jnp.max(best, axis=1, keepdims=True)), (8, PAIR))

    if moba:
        @pl.when(qi == 0)
        def _():
            lane_row = lax.broadcasted_iota(jnp.int32, (1, PAIR), 1)
            km_ref[...] = jnp.zeros_like(km_ref)
            for g in range(ATT_GROUP):
                for n in range(n_blocks):
                    rows = slice(n * MOBA_BLOCK, (n + 1) * MOBA_BLOCK)
                    lanes = slice(g * PAIR, (g + 1) * PAIR)
                    mean_a = jnp.mean(ka_ref[0, rows, lanes].astype(F32), axis=0, keepdims=True)
                    mean_b = jnp.mean(ka_ref[1, rows, lanes].astype(F32), axis=0, keepdims=True)
                    km_ref[g, n:n + 1, :] = jnp.where(lane_row < HEAD_DIM, mean_a, mean_b)

    row16 = lax.broadcasted_iota(jnp.int32, (AUG_ROWS, t), 0)
    own_rows = lax.broadcasted_iota(jnp.int32, (PAIR, t), 0) < HEAD_DIM
    blk = lax.broadcasted_iota(jnp.int32, (AUG_BLOCKS, t), 0)
    blk_f = blk.astype(F32)
    q_idx = lax.broadcasted_iota(jnp.int32, (1, t), 1)
    q_pos = q_idx.astype(F32)
    own_blk = qi * (t // MOBA_BLOCK) + q_idx // MOBA_BLOCK
    for g in range(ATT_GROUP):
        qt = qt_ref[0, 0, g * PAIR:(g + 1) * PAIR, :].astype(F32)
        for s in range(2):
            h = head_of(g, s)
            pos_rows = jnp.zeros((AUG_ROWS, t), F32)
            for piece in range(AUG_SLOPE_PIECES):
                hit = jnp.logical_or(row16 == piece, row16 == piece + AUG_SLOPE_PIECES)
                pos_rows = jnp.where(hit, slope_ref[h * n_tab + piece], pos_rows)
            if moba:
                q_own = jnp.where(own_rows if s == 0 else jnp.logical_not(own_rows), qt, 0.0)
                gate = jnp.dot(km_ref[g], q_own, precision=lax.Precision.HIGHEST,
                               preferred_element_type=F32)
                gt = jnp.where(blk < own_blk, gate, -jnp.inf)
                selected = blk == own_blk
                for pick in range(MOBA_TOPK):
                    mx = jnp.max(gt, axis=0, keepdims=True)
                    idx = jnp.min(jnp.where(gt == mx, blk_f, F32(AUG_BLOCKS)), axis=0,
                                  keepdims=True)
                    hit = jnp.logical_and(blk_f == idx, pick < own_blk)
                    selected = jnp.logical_or(selected, hit)
                    gt = jnp.where(blk_f == idx, -jnp.inf, gt)
                sel_rows = jnp.where(selected, 0.0, NEG_INF)
            else:
                sel_rows = jnp.zeros((AUG_BLOCKS, t), F32)
            own = qt[0:HEAD_DIM] if s == 0 else qt[HEAD_DIM:PAIR]
            q_norm = jnp.sqrt(jnp.sum(own * own, axis=0, keepdims=True))
            ub_ref[g, s] = (q_norm * kn_ref[g, s][0:1, 0:1] * UB_MARGIN
                            + slope_ref[h * n_tab + AUG_SLOPE_PIECES] * q_pos)
            tail = jnp.zeros((HEAD_DIM - AUG_BLOCKS - AUG_ROWS, t), F32)
            parts = [sel_rows, pos_rows, tail]
            parts = [own] + parts if s == 0 else parts + [own]
            qa_ref[g, s] = jnp.concatenate(parts, axis=0).astype(BF16)

    chains = [(g, s) for g in range(ATT_GROUP) for s in range(2)]
    n_full = qi

    def chunk_offset(g, s, c):
        return (slope_ref[head_of(g, s) * n_tab + AUG_SLOPE_PIECES]
                * ((c - qi) * ATT_CHUNK).astype(F32))

    v_rows = HEAD_DIM if moba else PAIR
    ones_rows = jnp.ones((AUG_ROWS, ATT_CHUNK), BF16)

    def key_rows(g, s, c):
        r0 = pl.multiple_of(c * ATT_CHUNK, ATT_CHUNK)
        return ka_ref[s, pl.ds(r0, ATT_CHUNK), g * PAIR:(g + 1) * PAIR]

    def value_rows(g, s, c):
        v0 = g * PAIR + (s * HEAD_DIM if moba else 0)
        return jnp.concatenate([vt_ref[0, c, v0:v0 + v_rows, :], ones_rows], axis=0)

    def probabilities(g, s, c, slot, causal=None):
        st = jnp.dot(key_rows(g, s, c), qa_ref[g, s], preferred_element_type=F32)
        if causal is not None:
            st = jnp.where(causal, st, NEG_INF)
        p = jnp.exp2(st - (ub_ref[g, s] - chunk_offset(g, s, c)))
        p_ref[slot, g, s] = p.astype(BF16)
        if not SUM_ON_MXU[moba]:
            l_ref[g, s] += jnp.sum(p, axis=0, keepdims=True)

    def values(g, s, c, slot):
        if SUM_ON_MXU[moba]:
            pv = jnp.dot(value_rows(g, s, c), p_ref[slot, g, s], preferred_element_type=F32)
            l_ref[g, s] += pv[v_rows:v_rows + 1]
            acc_ref[g, s] += pv[0:v_rows]
        else:
            v0 = g * PAIR + (s * HEAD_DIM if moba else 0)
            acc_ref[g, s] += jnp.dot(vt_ref[0, c, v0:v0 + v_rows, :], p_ref[slot, g, s],
                                     preferred_element_type=F32)

    def chunk_of(i):
        return jnp.where(i == 0, n_full, i - 1)

    def step(i, slot):
        for g, s in chains:
            probabilities(g, s, i, 1 - slot)
            values(g, s, chunk_of(i), slot)

    row_i = lax.broadcasted_iota(jnp.int32, (ATT_CHUNK, t), 0)
    col_i = lax.broadcasted_iota(jnp.int32, (ATT_CHUNK, t), 1)
    causal = row_i <= col_i
    l_ref[...] = jnp.zeros_like(l_ref)
    acc_ref[...] = jnp.zeros_like(acc_ref)
    for g, s in chains:
        probabilities(g, s, n_full, 0, causal)

    def two_steps(j, carry):
        step(2 * j, 0)
        step(2 * j + 1, 1)
        return carry

    lax.fori_loop(0, n_full // 2, two_steps, 0)

    @pl.when(n_full % 2 == 1)
    def _():
        step(n_full - 1, 0)
        for g, s in chains:
            values(g, s, chunk_of(n_full), 1)

    @pl.when(n_full % 2 == 0)
    def _():
        for g, s in chains:
            values(g, s, chunk_of(n_full), 0)

    def safe_path():
        m_ref[...] = jnp.full_like(m_ref, NEG_INF)
        l_ref[...] = jnp.zeros_like(l_ref)
        acc_ref[...] = jnp.zeros_like(acc_ref)

        def fold(g, s, c, mask):
            st = jnp.dot(key_rows(g, s, c), qa_ref[g, s], preferred_element_type=F32)
            if mask is not None:
                st = jnp.where(mask, st, NEG_INF)
            off = chunk_offset(g, s, c)
            m_prev = m_ref[g, s]
            m_next = jnp.maximum(m_prev, jnp.max(st, axis=0, keepdims=True) + off)
            alpha = jnp.exp2(m_prev - m_next)
            ps_ref[...] = jnp.exp2(st - (m_next - off)).astype(BF16)
            pv = jnp.dot(value_rows(g, s, c), ps_ref[...], preferred_element_type=F32)
            m_ref[g, s] = m_next
            l_ref[g, s] = alpha * l_ref[g, s] + pv[v_rows:v_rows + 1]
            acc_ref[g, s] = alpha * acc_ref[g, s] + pv[0:v_rows]

        for g, s in chains:
            fold(g, s, n_full, causal)

        def past(c, carry):
            for g, s in chains:
                fold(g, s, c, None)
            return carry

        lax.fori_loop(0, n_full, past, 0)

    l_min = l_ref[0, 0]
    for g, s in chains[1:]:
        l_min = jnp.minimum(l_min, l_ref[g, s])
    pl.when(jnp.min(l_min) < L_MIN)(safe_path)

    for g in range(ATT_GROUP):
        out_a = acc_ref[g, 0] / l_ref[g, 0]
        out_b = acc_ref[g, 1] / l_ref[g, 1]
        if moba:
            o = jnp.concatenate([out_a, out_b], axis=0)
        else:
            lp = lam_ref[...]
            lam = (jnp.exp(jnp.sum(lp[0:1] * lp[1:2], axis=-1, keepdims=True))
                   - jnp.exp(jnp.sum(lp[2:3] * lp[3:4], axis=-1, keepdims=True)) + lam_init)
            o = out_a - lam * out_b
            rms = lax.rsqrt(jnp.mean(o * o, axis=0, keepdims=True) + RMS_EPS)
            o = o * rms * pltpu.repeat(subln_ref[...], t // PAIR, axis=1) * (1.0 - lam_init)
        o_ref[0, 0, g * PAIR:(g + 1) * PAIR, :] = o.astype(o_ref.dtype)


def _attention(qt, ka, vt, slope_table, lam_params, subln_w, *, moba, lam_init):
    b, nq, d, _ = qt.shape
    t = nq * ATT_TILE
    n_blocks = t // MOBA_BLOCK
    assert ATT_TILE == ATT_CHUNK and ATT_TILE % MOBA_BLOCK == 0 and n_blocks <= AUG_BLOCKS
    assert AUG_POS + 2 * AUG_SLOPE_PIECES <= AUG_BLOCKS + AUG_ROWS <= HEAD_DIM
    gl = ATT_GROUP * PAIR
    return pl.pallas_call(
        functools.partial(_attention_kernel, moba=moba, lam_init=lam_init),
        grid=(b, N_PAIRS // ATT_GROUP, nq),
        in_specs=[
            pl.BlockSpec(memory_space=pltpu.SMEM),
            pl.BlockSpec((1, 1, gl, ATT_TILE), lambda bi, p, i: (bi, i, p, 0)),
            pl.BlockSpec((2, None, t, gl), lambda bi, p, i: (0, bi, 0, p)),
            pl.BlockSpec((1, t // ATT_CHUNK, gl, ATT_CHUNK), lambda bi, p, i: (bi, 0, p, 0)),
            pl.BlockSpec((4, HEAD_DIM), lambda bi, p, i: (0, 0)),
            pl.BlockSpec((PAIR, PAIR), lambda bi, p, i: (0, 0)),
        ],
        out_specs=pl.BlockSpec((1, 1, gl, ATT_TILE), lambda bi, p, i: (bi, i, p, 0)),
        out_shape=jax.ShapeDtypeStruct((b, nq, d, ATT_TILE), BF16),
        scratch_shapes=[
            pltpu.VMEM((ATT_GROUP, 2, PAIR, ATT_TILE), BF16),
            pltpu.VMEM((ATT_GROUP, 2, 1, ATT_TILE), F32),
            pltpu.VMEM((ATT_GROUP, 2, 1, ATT_TILE), F32),
            pltpu.VMEM((ATT_GROUP, 2, HEAD_DIM if moba else PAIR, ATT_TILE), F32),
            pltpu.VMEM((ATT_GROUP, AUG_BLOCKS, PAIR), F32),
            pltpu.VMEM((ATT_GROUP, 2, 8, PAIR), F32),
            pltpu.VMEM((2, ATT_GROUP, 2, ATT_CHUNK, ATT_TILE), BF16),
            pltpu.VMEM((ATT_CHUNK, ATT_TILE), BF16),
            pltpu.VMEM((ATT_GROUP, 2, 1, ATT_TILE), F32),
        ],
        compiler_params=pltpu.CompilerParams(
            dimension_semantics=("arbitrary", "arbitrary", "arbitrary"),
            vmem_limit_bytes=V7X_VMEM_LIMIT_BYTES),
        name="moba_attention" if moba else "diff_attention",
    )(slope_table, qt, ka.reshape(2, b, t, d), vt, lam_params,
      jnp.broadcast_to(subln_w[:, None], (PAIR, PAIR)))


def kernel(x, attn_norm, w_in, w_out, diff_lambda, diff_subln, mlp_norm, w_ff1, w_ff2, final_norm):
    b, t, d = x.shape
    h = x.reshape(b * t, d)
    q_scale = HEAD_DIM ** -0.5 * LOG2E
    for i in range(DEPTH):
        moba = i % N_MIXERS == 1
        li = i // N_MIXERS
        qt, ka, vt = _qkv_projection(h, attn_norm[i], w_in, i, batch=b, q_scale=q_scale)
        slope_table = jnp.asarray(_alibi_slope_pieces(MOBA_HEADS if moba else DIFF_HEADS))
        mix = _attention(qt, ka, vt, slope_table, diff_lambda[li], diff_subln[li],
                         moba=moba, lam_init=_lambda_init(i))
        h, u = _attn_out_mlp_up(mix, w_out, h, mlp_norm[i], w_ff1, i)
        h = _mlp_down(u, w_ff2, h, final_norm, i, final_norm=(i == DEPTH - 1))
    return h.reshape(b, t, d)
```

```python
import functools
import math

import jax
import jax.numpy as jnp
import numpy as np
from jax import lax
from jax.experimental import pallas as pl
from jax.experimental.pallas import tpu as pltpu

D_MODEL = 1024
DEPTH = 2
N_MIXERS = 2
DIFF_HEADS = 8
MOBA_HEADS = 16
HEAD_DIM = 64
PAIR = 2 * HEAD_DIM
N_PAIRS = D_MODEL // PAIR
MOBA_BLOCK = 256
MOBA_TOPK = 3
D_FF = 4 * D_MODEL
RMS_EPS = 1e-6
NEG_INF = -1e30
LOG2E = math.log2(math.e)

V7X_VMEM_LIMIT_BYTES = 56 * 1024 * 1024

ROW_TILE = 512
COL_TILE = 1024
COL_CHUNK = 256
KEY_TILE = MOBA_BLOCK
ATT_CHUNK_TILES = 2
ATT_CHUNK = ATT_CHUNK_TILES * KEY_TILE
ATT_TILE = ATT_CHUNK
ATT_GROUP = 2

AUG_BASE = (HEAD_DIM, 0)
AUG_BLOCKS = 32
AUG_POS = AUG_BLOCKS
AUG_SLOPE_PIECES = 3
AUG_ROWS = 16

BF16 = jnp.bfloat16
F32 = jnp.float32


def _alibi_slope_pieces(n_heads):
    slopes = np.array([2.0 ** (-8.0 * (h + 1) / n_heads) for h in range(n_heads)], np.float64)
    rest = slopes * LOG2E
    pieces = []
    for _ in range(AUG_SLOPE_PIECES):
        p = rest.astype(np.float32).astype(BF16).astype(np.float64)
        pieces.append(p)
        rest = rest - p
    pieces = np.stack(pieces, axis=1)
    table = np.concatenate([pieces, pieces.sum(axis=1, keepdims=True)], axis=1)
    return table.astype(np.float32).reshape(-1)


def _lambda_init(layer_idx):
    return 0.8 - 0.6 * math.exp(-0.3 * layer_idx)


def _rmsnorm_rows(x, g):
    return x * lax.rsqrt(jnp.mean(x * x, axis=-1, keepdims=True) + RMS_EPS) * g


_NT = (((1,), (1,)), ((), ()))


def _layer_weight_spec(w, layer):
    return pl.BlockSpec((None,) + w.shape[1:], lambda i: (layer, 0, 0),
                        pipeline_mode=pl.Buffered(1))


def _cast_columns(dst_ref, src_ref, c0, c1, *, transpose=False):
    for j in range(c0, c1, COL_CHUNK):
        blk = src_ref[:, j:j + COL_CHUNK]
        if transpose:
            dst_ref[j - c0:j - c0 + COL_CHUNK, :] = blk.T.astype(BF16)
        else:
            dst_ref[:, j - c0:j - c0 + COL_CHUNK] = blk.astype(BF16)


def _qkv_kernel(x_ref, g_ref, w_ref, qt_ref, ka_ref, vt_ref, wqt_ref, wk_ref, wvt_ref, *,
                q_scale, steps_per_seq):
    @pl.when(pl.program_id(0) == 0)
    def _():
        _cast_columns(wqt_ref, w_ref, 0, D_MODEL, transpose=True)
        _cast_columns(wk_ref, w_ref, D_MODEL, 2 * D_MODEL)
        _cast_columns(wvt_ref, w_ref, 2 * D_MODEL, 3 * D_MODEL, transpose=True)

    lane = lax.broadcasted_iota(jnp.int32, (KEY_TILE, PAIR), 1)
    pos = lax.broadcasted_iota(jnp.int32, (KEY_TILE, PAIR), 0).astype(F32)
    for part in range(ROW_TILE // KEY_TILE):
        rows = slice(part * KEY_TILE, (part + 1) * KEY_TILE)
        xn = _rmsnorm_rows(x_ref[rows], g_ref[...]).astype(BF16)
        for c0 in range(0, D_MODEL, 256):
            vt = lax.dot_general(wvt_ref[c0:c0 + 256, :], xn, _NT, preferred_element_type=F32)
            vt_ref[0, 0, c0:c0 + 256, rows] = vt.astype(BF16)
            qt = lax.dot_general(wqt_ref[c0:c0 + 256, :], xn, _NT, preferred_element_type=F32)
            q_tile, q_col = divmod(part * KEY_TILE, ATT_TILE)
            qt_ref[0, q_tile, c0:c0 + 256, q_col:q_col + KEY_TILE] = (qt * q_scale).astype(BF16)
        k = jnp.dot(xn, wk_ref[...], preferred_element_type=F32).astype(BF16)
        tile = (pl.program_id(0) % steps_per_seq) * (ROW_TILE // KEY_TILE) + part
        tile_pos = float((part % ATT_CHUNK_TILES) * KEY_TILE)
        for s in range(2):
            rel = lane - AUG_BASE[s]
            aug = jnp.where(rel == tile, 1.0, 0.0)
            aug = jnp.where(jnp.logical_and(rel >= AUG_POS, rel < AUG_POS + AUG_SLOPE_PIECES), pos, aug)
            aug = jnp.where(jnp.logical_and(rel >= AUG_POS + AUG_SLOPE_PIECES,
                                            rel < AUG_POS + 2 * AUG_SLOPE_PIECES), tile_pos, aug)
            aug = aug.astype(BF16)
            own = jnp.logical_and(rel >= -HEAD_DIM, rel < 0) if s == 0 else rel >= HEAD_DIM
            for c0 in range(0, D_MODEL, PAIR):
                ka_ref[s, rows, c0:c0 + PAIR] = jnp.where(own, k[:, c0:c0 + PAIR], aug)


def _qkv_projection(x, g, w_in, layer, *, batch, q_scale):
    m, d = x.shape
    t = m // batch
    spb = t // ROW_TILE
    spc = ATT_CHUNK // ROW_TILE
    assert t // KEY_TILE <= AUG_BLOCKS and ATT_CHUNK % ROW_TILE == 0 and t % ATT_CHUNK == 0
    assert ROW_TILE % ATT_TILE == 0 and ROW_TILE % ATT_CHUNK == 0
    return pl.pallas_call(
        functools.partial(_qkv_kernel, q_scale=q_scale, steps_per_seq=spb),
        grid=(m // ROW_TILE,),
        in_specs=[
            pl.BlockSpec((ROW_TILE, d), lambda i: (i, 0)),
            pl.BlockSpec((1, d), lambda i: (0, 0)),
            _layer_weight_spec(w_in, layer),
        ],
        out_specs=[
            pl.BlockSpec((1, ROW_TILE // ATT_TILE, d, ATT_TILE), lambda i: (i // spb, i % spb, 0, 0)),
            pl.BlockSpec((2, ROW_TILE, d), lambda i: (0, i, 0)),
            pl.BlockSpec((1, 1, d, ROW_TILE), lambda i: (i // spb, (i % spb) // spc, 0, i % spc)),
        ],
        out_shape=[
            jax.ShapeDtypeStruct((batch, t // ATT_TILE, d, ATT_TILE), BF16),
            jax.ShapeDtypeStruct((2, m, d), BF16),
            jax.ShapeDtypeStruct((batch, t // ATT_CHUNK, d, ATT_CHUNK), BF16),
        ],
        scratch_shapes=[pltpu.VMEM((d, d), BF16)] * 3,
        compiler_params=pltpu.CompilerParams(
            dimension_semantics=("arbitrary",), vmem_limit_bytes=V7X_VMEM_LIMIT_BYTES),
        name="qkv_projection",
    )(x, g.reshape(1, d), w_in)


def _attn_out_mlp_up_kernel(a_ref, wo_ref, r_ref, g_ref, w1_ref, h_ref, u_ref, wo_b_ref, w1_b_ref):
    @pl.when(pl.program_id(0) == 0)
    def _():
        _cast_columns(wo_b_ref, wo_ref, 0, wo_ref.shape[1])
        _cast_columns(w1_b_ref, w1_ref, 0, w1_ref.shape[1])

    h = r_ref[...] + lax.dot_general(a_ref[...], wo_b_ref[...], (((0,), (0,)), ((), ())),
                                     preferred_element_type=F32)
    h_ref[...] = h
    xn = _rmsnorm_rows(h, g_ref[...]).astype(BF16)
    for c0 in range(0, w1_b_ref.shape[1], COL_TILE):
        acc = jnp.dot(xn, w1_b_ref[:, c0:c0 + COL_TILE], preferred_element_type=F32)
        u_ref[:, c0:c0 + COL_TILE] = jnp.square(jnp.maximum(acc, 0.0)).astype(u_ref.dtype)


def _attn_out_mlp_up(a_t, w_out, r, g, w_ff1, layer):
    m, d = r.shape
    n = w_ff1.shape[2]
    spb = a_t.shape[1]
    assert a_t.shape[2:] == (d, ROW_TILE)
    rows = lambda width: pl.BlockSpec((ROW_TILE, width), lambda i: (i, 0))
    return pl.pallas_call(
        _attn_out_mlp_up_kernel,
        grid=(m // ROW_TILE,),
        in_specs=[pl.BlockSpec((None, None, d, ROW_TILE), lambda i: (i // spb, i % spb, 0, 0)),
                  _layer_weight_spec(w_out, layer), rows(d),
                  pl.BlockSpec((1, d), lambda i: (0, 0)), _layer_weight_spec(w_ff1, layer)],
        out_specs=[rows(d), rows(n)],
        out_shape=[jax.ShapeDtypeStruct((m, d), F32), jax.ShapeDtypeStruct((m, n), BF16)],
        scratch_shapes=[pltpu.VMEM((d, d), BF16), pltpu.VMEM((d, n), BF16)],
        compiler_params=pltpu.CompilerParams(
            dimension_semantics=("arbitrary",), vmem_limit_bytes=V7X_VMEM_LIMIT_BYTES),
        name="attn_out_mlp_up",
    )(a_t, w_out, r, g.reshape(1, d), w_ff1)


def _mlp_down_kernel(a_ref, w_ref, r_ref, g_ref, o_ref, w_b_ref, *, final_norm):
    @pl.when(pl.program_id(0) == 0)
    def _():
        _cast_columns(w_b_ref, w_ref, 0, w_ref.shape[1])

    h = r_ref[...] + jnp.dot(a_ref[...], w_b_ref[...], preferred_element_type=F32)
    if final_norm:
        h = _rmsnorm_rows(h, g_ref[...])
    o_ref[...] = h


def _mlp_down(a, w_ff2, r, g, layer, *, final_norm):
    m, k = a.shape
    n = w_ff2.shape[2]
    return pl.pallas_call(
        functools.partial(_mlp_down_kernel, final_norm=final_norm),
        grid=(m // ROW_TILE,),
        in_specs=[
            pl.BlockSpec((ROW_TILE, k), lambda i: (i, 0)),
            _layer_weight_spec(w_ff2, layer),
            pl.BlockSpec((ROW_TILE, n), lambda i: (i, 0)),
            pl.BlockSpec((1, n), lambda i: (0, 0)),
        ],
        out_specs=pl.BlockSpec((ROW_TILE, n), lambda i: (i, 0)),
        out_shape=jax.ShapeDtypeStruct((m, n), F32),
        scratch_shapes=[pltpu.VMEM((k, n), BF16)],
        compiler_params=pltpu.CompilerParams(
            dimension_semantics=("arbitrary",), vmem_limit_bytes=V7X_VMEM_LIMIT_BYTES),
        name="mlp_down",
    )(a, w_ff2, r, g.reshape(1, n))


SUM_ON_MXU = {False: False, True: True}
L_MIN = 2.0 ** -60
UB_MARGIN = 1.01


def _attention_kernel(slope_ref, qt_ref, ka_ref, vt_ref, lam_ref, subln_ref, o_ref,
                      qa_ref, ub_ref, l_ref, acc_ref, km_ref, kn_ref, p_ref, ps_ref, m_ref, *, moba,
                      lam_init):
    group = pl.program_id(1)
    qi = pl.program_id(2)
    t = ATT_TILE
    n_tab = AUG_SLOPE_PIECES + 1
    n_blocks = ka_ref.shape[1] // MOBA_BLOCK
    n_chunks = ka_ref.shape[1] // ATT_CHUNK

    def head_of(g, s):
        pair = group * ATT_GROUP + g
        return 2 * pair + s if moba else pair

    @pl.when(qi == 0)
    def _():
        lane8 = lax.broadcasted_iota(jnp.int32, (8, PAIR), 1)
        for g in range(ATT_GROUP):
            for s in range(2):
                own = (lane8 < HEAD_DIM) if s == 0 else (lane8 >= HEAD_DIM)
                ones_own = jnp.where(own, 1.0, 0.0).astype(BF16)

                def chunk_norms(c, best):
                    r0 = pl.multiple_of(c * ATT_CHUNK, ATT_CHUNK)
                    k_c = ka_ref[s, pl.ds(r0, ATT_CHUNK), g * PAIR:(g + 1) * PAIR].astype(F32)
                    sq = lax.dot_general(ones_own, (k_c * k_c).astype(BF16), _NT,
                                         preferred_element_type=F32)
                    return jnp.maximum(best, sq)

                best = lax.fori_loop(0, n_chunks, chunk_norms, jnp.zeros((8, ATT_CHUNK), F32))
                kn_ref[g, s] = jnp.broadcast_to(
                    jnp.sqrt(jnp.max(best, axis=1, keepdims=True)), (8, PAIR))

    if moba:
        @pl.when(qi == 0)
        def _():
            lane_row = lax.broadcasted_iota(jnp.int32, (1, PAIR), 1)
            km_ref[...] = jnp.zeros_like(km_ref)
            for g in range(ATT_GROUP):
                for n in range(n_blocks):
                    rows = slice(n * MOBA_BLOCK, (n + 1) * MOBA_BLOCK)
                    lanes = slice(g * PAIR, (g + 1) * PAIR)
                    mean_a = jnp.mean(ka_ref[0, rows, lanes].astype(F32), axis=0, keepdims=True)
                    mean_b = jnp.mean(ka_ref[1, rows, lanes].astype(F32), axis=0, keepdims=True)
                    km_ref[g, n:n + 1, :] = jnp.where(lane_row < HEAD_DIM, mean_a, mean_b)

    row16 = lax.broadcasted_iota(jnp.int32, (AUG_ROWS, t), 0)
    own_rows = lax.broadcasted_iota(jnp.int32, (PAIR, t), 0) < HEAD_DIM
    blk = lax.broadcasted_iota(jnp.int32, (AUG_BLOCKS, t), 0)
    blk_f = blk.astype(F32)
    q_idx = lax.broadcasted_iota(jnp.int32, (1, t), 1)
    q_pos = q_idx.astype(F32)
    own_blk = qi * (t // MOBA_BLOCK) + q_idx // MOBA_BLOCK
    for g in range(ATT_GROUP):
        qt = qt_ref[0, 0, g * PAIR:(g + 1) * PAIR, :].astype(F32)
        for s in range(2):
            h = head_of(g, s)
            pos_rows = jnp.zeros((AUG_ROWS, t), F32)
            for piece in range(AUG_SLOPE_PIECES):
                hit = jnp.logical_or(row16 == piece, row16 == piece + AUG_SLOPE_PIECES)
                pos_rows = jnp.where(hit, slope_ref[h * n_tab + piece], pos_rows)
            if moba:
                q_own = jnp.where(own_rows if s == 0 else jnp.logical_not(own_rows), qt, 0.0)
                gate = jnp.dot(km_ref[g], q_own, precision=lax.Precision.HIGHEST,
                               preferred_element_type=F32)
                gt = jnp.where(blk < own_blk, gate, -jnp.inf)
                selected = blk == own_blk
                for pick in range(MOBA_TOPK):
                    mx = jnp.max(gt, axis=0, keepdims=True)
                    idx = jnp.min(jnp.where(gt == mx, blk_f, F32(AUG_BLOCKS)), axis=0,
                                  keepdims=True)
                    hit = jnp.logical_and(blk_f == idx, pick < own_blk)
                    selected = jnp.logical_or(selected, hit)
                    gt = jnp.where(blk_f == idx, -jnp.inf, gt)
                sel_rows = jnp.where(selected, 0.0, NEG_INF)
            else:
                sel_rows = jnp.zeros((AUG_BLOCKS, t), F32)
            own = qt[0:HEAD_DIM] if s == 0 else qt[HEAD_DIM:PAIR]
            q_norm = jnp.sqrt(jnp.sum(own * own, axis=0, keepdims=True))
            ub_ref[g, s] = (q_norm * kn_ref[g, s][0:1, 0:1] * UB_MARGIN
                            + slope_ref[h * n_tab + AUG_SLOPE_PIECES] * q_pos)
            tail = jnp.zeros((HEAD_DIM - AUG_BLOCKS - AUG_ROWS, t), F32)
            parts = [sel_rows, pos_rows, tail]
            parts = [own] + parts if s == 0 else parts + [own]
            qa_ref[g, s] = jnp.concatenate(parts, axis=0).astype(BF16)

    chains = [(g, s) for g in range(ATT_GROUP) for s in range(2)]
    n_full = qi

    def chunk_offset(g, s, c):
        return (slope_ref[head_of(g, s) * n_tab + AUG_SLOPE_PIECES]
                * ((c - qi) * ATT_CHUNK).astype(F32))

    v_rows = HEAD_DIM if moba else PAIR
    ones_rows = jnp.ones((AUG_ROWS, ATT_CHUNK), BF16)

    def key_rows(g, s, c):
        r0 = pl.multiple_of(c * ATT_CHUNK, ATT_CHUNK)
        return ka_ref[s, pl.ds(r0, ATT_CHUNK), g * PAIR:(g + 1) * PAIR]

    def value_rows(g, s, c):
        v0 = g * PAIR + (s * HEAD_DIM if moba else 0)
        return jnp.concatenate([vt_ref[0, c, v0:v0 + v_rows, :], ones_rows], axis=0)

    def probabilities(g, s, c, slot):
        st = jnp.dot(key_rows(g, s, c), qa_ref[g, s], preferred_element_type=F32)
        p = jnp.exp2(st - (ub_ref[g, s] - chunk_offset(g, s, c)))
        p_ref[slot, g, s] = p.astype(BF16)
        if not SUM_ON_MXU[moba]:
            l_ref[g, s] += jnp.sum(p, axis=0, keepdims=True)

    def values(g, s, c, slot):
        if SUM_ON_MXU[moba]:
            pv = jnp.dot(value_rows(g, s, c), p_ref[slot, g, s], preferred_element_type=F32)
            l_ref[g, s] += pv[v_rows:v_rows + 1]
            acc_ref[g, s] += pv[0:v_rows]
        else:
            v0 = g * PAIR + (s * HEAD_DIM if moba else 0)
            acc_ref[g, s] += jnp.dot(vt_ref[0, c, v0:v0 + v_rows, :], p_ref[slot, g, s],
                                     preferred_element_type=F32)

    def chunk_of(i):
        return jnp.where(i == 0, n_full, i - 1)

    def step(i, slot):
        for g, s in chains:
            probabilities(g, s, i, 1 - slot)
            values(g, s, chunk_of(i), slot)

    row_i = lax.broadcasted_iota(jnp.int32, (ATT_CHUNK, t), 0)
    col_i = lax.broadcasted_iota(jnp.int32, (ATT_CHUNK, t), 1)
    causal = row_i <= col_i

    def own_probabilities(g, s):
        half = t // 2
        r0 = pl.multiple_of(n_full * ATT_CHUNK, ATT_CHUNK)
        lanes = slice(g * PAIR, (g + 1) * PAIR)
        tri = causal[0:half, 0:half]
        st0 = jnp.dot(ka_ref[s, pl.ds(r0, half), lanes], qa_ref[g, s, :, 0:half],
                      preferred_element_type=F32)
        p0 = jnp.exp2(jnp.where(tri, st0, NEG_INF) - ub_ref[g, s, :, 0:half])
        st1 = jnp.dot(ka_ref[s, pl.ds(r0, ATT_CHUNK), lanes], qa_ref[g, s, :, half:t],
                      preferred_element_type=F32)
        p1 = jnp.exp2(jnp.where(causal[:, half:t], st1, NEG_INF) - ub_ref[g, s, :, half:t])
        p_ref[0, g, s, 0:half, 0:half] = p0.astype(BF16)
        p_ref[0, g, s, half:ATT_CHUNK, 0:half] = jnp.zeros((ATT_CHUNK - half, half), BF16)
        p_ref[0, g, s, :, half:t] = p1.astype(BF16)
        if not SUM_ON_MXU[moba]:
            l_ref[g, s, :, 0:half] += jnp.sum(p0, axis=0, keepdims=True)
            l_ref[g, s, :, half:t] += jnp.sum(p1, axis=0, keepdims=True)

    l_ref[...] = jnp.zeros_like(l_ref)
    acc_ref[...] = jnp.zeros_like(acc_ref)
    for g, s in chains:
        own_probabilities(g, s)

    def two_steps(j, carry):
        step(2 * j, 0)
        step(2 * j + 1, 1)
        return carry

    lax.fori_loop(0, n_full // 2, two_steps, 0)

    def write_output():
        for g in range(ATT_GROUP):
            out_a = acc_ref[g, 0] / l_ref[g, 0]
            out_b = acc_ref[g, 1] / l_ref[g, 1]
            if moba:
                o = jnp.concatenate([out_a, out_b], axis=0)
            else:
                lp = lam_ref[...]
                lam = (jnp.exp(jnp.sum(lp[0:1] * lp[1:2], axis=-1, keepdims=True))
                       - jnp.exp(jnp.sum(lp[2:3] * lp[3:4], axis=-1, keepdims=True)) + lam_init)
                o = out_a - lam * out_b
                rms = lax.rsqrt(jnp.mean(o * o, axis=0, keepdims=True) + RMS_EPS)
                o = o * rms * pltpu.repeat(subln_ref[...], t // PAIR, axis=1) * (1.0 - lam_init)
            o_ref[0, 0, g * PAIR:(g + 1) * PAIR, :] = o.astype(o_ref.dtype)

    @pl.when(n_full % 2 == 1)
    def _():
        step(n_full - 1, 0)
        for g, s in chains:
            values(g, s, chunk_of(n_full), 1)
        write_output()

    @pl.when(n_full % 2 == 0)
    def _():
        for g, s in chains:
            values(g, s, chunk_of(n_full), 0)
        write_output()

    def safe_path():
        m_ref[...] = jnp.full_like(m_ref, NEG_INF)
        l_ref[...] = jnp.zeros_like(l_ref)
        acc_ref[...] = jnp.zeros_like(acc_ref)

        def fold(g, s, c, mask):
            st = jnp.dot(key_rows(g, s, c), qa_ref[g, s], preferred_element_type=F32)
            if mask is not None:
                st = jnp.where(mask, st, NEG_INF)
            off = chunk_offset(g, s, c)
            m_prev = m_ref[g, s]
            m_next = jnp.maximum(m_prev, jnp.max(st, axis=0, keepdims=True) + off)
            alpha = jnp.exp2(m_prev - m_next)
            ps_ref[...] = jnp.exp2(st - (m_next - off)).astype(BF16)
            pv = jnp.dot(value_rows(g, s, c), ps_ref[...], preferred_element_type=F32)
            m_ref[g, s] = m_next
            l_ref[g, s] = alpha * l_ref[g, s] + pv[v_rows:v_rows + 1]
            acc_ref[g, s] = alpha * acc_ref[g, s] + pv[0:v_rows]

        for g, s in chains:
            fold(g, s, n_full, causal)

        def past(c, carry):
            for g, s in chains:
                fold(g, s, c, None)
            return carry

        lax.fori_loop(0, n_full, past, 0)

    l_min = l_ref[0, 0]
    for g, s in chains[1:]:
        l_min = jnp.minimum(l_min, l_ref[g, s])

    @pl.when(jnp.logical_not(jnp.min(l_min) >= L_MIN))
    def _():
        safe_path()
        write_output()


def _attention(qt, ka, vt, slope_table, lam_params, subln_w, *, moba, lam_init):
    b, nq, d, _ = qt.shape
    t = nq * ATT_TILE
    n_blocks = t // MOBA_BLOCK
    assert ATT_TILE == ATT_CHUNK and ATT_TILE % MOBA_BLOCK == 0 and n_blocks <= AUG_BLOCKS
    assert AUG_POS + 2 * AUG_SLOPE_PIECES <= AUG_BLOCKS + AUG_ROWS <= HEAD_DIM
    gl = ATT_GROUP * PAIR
    return pl.pallas_call(
        functools.partial(_attention_kernel, moba=moba, lam_init=lam_init),
        grid=(b, N_PAIRS // ATT_GROUP, nq),
        in_specs=[
            pl.BlockSpec(memory_space=pltpu.SMEM),
            pl.BlockSpec((1, 1, gl, ATT_TILE), lambda bi, p, i: (bi, i, p, 0)),
            pl.BlockSpec((2, None, t, gl), lambda bi, p, i: (0, bi, 0, p)),
            pl.BlockSpec((1, t // ATT_CHUNK, gl, ATT_CHUNK), lambda bi, p, i: (bi, 0, p, 0)),
            pl.BlockSpec((4, HEAD_DIM), lambda bi, p, i: (0, 0)),
            pl.BlockSpec((PAIR, PAIR), lambda bi, p, i: (0, 0)),
        ],
        out_specs=pl.BlockSpec((1, 1, gl, ATT_TILE), lambda bi, p, i: (bi, i, p, 0)),
        out_shape=jax.ShapeDtypeStruct((b, nq, d, ATT_TILE), BF16),
        scratch_shapes=[
            pltpu.VMEM((ATT_GROUP, 2, PAIR, ATT_TILE), BF16),
            pltpu.VMEM((ATT_GROUP, 2, 1, ATT_TILE), F32),
            pltpu.VMEM((ATT_GROUP, 2, 1, ATT_TILE), F32),
            pltpu.VMEM((ATT_GROUP, 2, HEAD_DIM if moba else PAIR, ATT_TILE), F32),
            pltpu.VMEM((ATT_GROUP, AUG_BLOCKS, PAIR), F32),
            pltpu.VMEM((ATT_GROUP, 2, 8, PAIR), F32),
            pltpu.VMEM((2, ATT_GROUP, 2, ATT_CHUNK, ATT_TILE), BF16),
            pltpu.VMEM((ATT_CHUNK, ATT_TILE), BF16),
            pltpu.VMEM((ATT_GROUP, 2, 1, ATT_TILE), F32),
        ],
        compiler_params=pltpu.CompilerParams(
            dimension_semantics=("arbitrary", "arbitrary", "arbitrary"),
            vmem_limit_bytes=V7X_VMEM_LIMIT_BYTES),
        name="moba_attention" if moba else "diff_attention",
    )(slope_table, qt, ka.reshape(2, b, t, d), vt, lam_params,
      jnp.broadcast_to(subln_w[:, None], (PAIR, PAIR)))


def kernel(x, attn_norm, w_in, w_out, diff_lambda, diff_subln, mlp_norm, w_ff1, w_ff2, final_norm):
    b, t, d = x.shape
    h = x.reshape(b * t, d)
    q_scale = HEAD_DIM ** -0.5 * LOG2E
    for i in range(DEPTH):
        moba = i % N_MIXERS == 1
        li = i // N_MIXERS
        qt, ka, vt = _qkv_projection(h, attn_norm[i], w_in, i, batch=b, q_scale=q_scale)
        slope_table = jnp.asarray(_alibi_slope_pieces(MOBA_HEADS if moba else DIFF_HEADS))
        mix = _attention(qt, ka, vt, slope_table, diff_lambda[li], diff_subln[li],
                         moba=moba, lam_init=_lambda_init(i))
        h, u = _attn_out_mlp_up(mix, w_out, h, mlp_norm[i], w_ff1, i)
        h = _mlp_down(u, w_ff2, h, final_norm, i, final_norm=(i == DEPTH - 1))
    return h.reshape(b, t, d)
```

```python
import functools
import math

import jax
import jax.numpy as jnp
import numpy as np
from jax import lax
from jax.experimental import pallas as pl
from jax.experimental.pallas import tpu as pltpu

D_MODEL = 1024
DEPTH = 2
N_MIXERS = 2
DIFF_HEADS = 8
MOBA_HEADS = 16
HEAD_DIM = 64
PAIR = 2 * HEAD_DIM
N_PAIRS = D_MODEL // PAIR
MOBA_BLOCK = 256
MOBA_TOPK = 3
D_FF = 4 * D_MODEL
RMS_EPS = 1e-6
NEG_INF = -1e30
LOG2E = math.log2(math.e)

V7X_VMEM_LIMIT_BYTES = 56 * 1024 * 1024

ROW_TILE = 512
COL_TILE = 1024
COL_CHUNK = 256
KEY_TILE = MOBA_BLOCK
ATT_CHUNK_TILES = 2
ATT_CHUNK = ATT_CHUNK_TILES * KEY_TILE
Q_CHUNKS = 2
ATT_TILE = Q_CHUNKS * ATT_CHUNK
ATT_GROUP = 2

AUG_BASE = (HEAD_DIM, 0)
AUG_BLOCKS = 32
AUG_POS = AUG_BLOCKS
AUG_SLOPE_PIECES = 3
AUG_ROWS = 16

BF16 = jnp.bfloat16
F32 = jnp.float32


def _alibi_slope_pieces(n_heads):
    slopes = np.array([2.0 ** (-8.0 * (h + 1) / n_heads) for h in range(n_heads)], np.float64)
    rest = slopes * LOG2E
    pieces = []
    for _ in range(AUG_SLOPE_PIECES):
        p = rest.astype(np.float32).astype(BF16).astype(np.float64)
        pieces.append(p)
        rest = rest - p
    pieces = np.stack(pieces, axis=1)
    table = np.concatenate([pieces, pieces.sum(axis=1, keepdims=True)], axis=1)
    return table.astype(np.float32).reshape(-1)


def _lambda_init(layer_idx):
    return 0.8 - 0.6 * math.exp(-0.3 * layer_idx)


def _rmsnorm_rows(x, g):
    return x * lax.rsqrt(jnp.mean(x * x, axis=-1, keepdims=True) + RMS_EPS) * g


_NT = (((1,), (1,)), ((), ()))


def _layer_weight_spec(w, layer):
    return pl.BlockSpec((None,) + w.shape[1:], lambda i: (layer, 0, 0),
                        pipeline_mode=pl.Buffered(1))


def _cast_columns(dst_ref, src_ref, c0, c1, *, transpose=False):
    for j in range(c0, c1, COL_CHUNK):
        blk = src_ref[:, j:j + COL_CHUNK]
        if transpose:
            dst_ref[j - c0:j - c0 + COL_CHUNK, :] = blk.T.astype(BF16)
        else:
            dst_ref[:, j - c0:j - c0 + COL_CHUNK] = blk.astype(BF16)


def _qkv_kernel(x_ref, g_ref, w_ref, qt_ref, ka_ref, vt_ref, wqt_ref, wk_ref, wvt_ref, *,
                q_scale, steps_per_seq):
    @pl.when(pl.program_id(0) == 0)
    def _():
        _cast_columns(wqt_ref, w_ref, 0, D_MODEL, transpose=True)
        _cast_columns(wk_ref, w_ref, D_MODEL, 2 * D_MODEL)
        _cast_columns(wvt_ref, w_ref, 2 * D_MODEL, 3 * D_MODEL, transpose=True)

    lane = lax.broadcasted_iota(jnp.int32, (KEY_TILE, PAIR), 1)
    pos = lax.broadcasted_iota(jnp.int32, (KEY_TILE, PAIR), 0).astype(F32)
    for part in range(ROW_TILE // KEY_TILE):
        rows = slice(part * KEY_TILE, (part + 1) * KEY_TILE)
        xn = _rmsnorm_rows(x_ref[rows], g_ref[...]).astype(BF16)
        for c0 in range(0, D_MODEL, 256):
            vt = lax.dot_general(wvt_ref[c0:c0 + 256, :], xn, _NT, preferred_element_type=F32)
            vt_ref[0, 0, c0:c0 + 256, rows] = vt.astype(BF16)
            qt = lax.dot_general(wqt_ref[c0:c0 + 256, :], xn, _NT, preferred_element_type=F32)
            qt_ref[0, 0, c0:c0 + 256, rows] = (qt * q_scale).astype(BF16)
        k = jnp.dot(xn, wk_ref[...], preferred_element_type=F32).astype(BF16)
        tile = (pl.program_id(0) % steps_per_seq) * (ROW_TILE // KEY_TILE) + part
        tile_pos = float((part % ATT_CHUNK_TILES) * KEY_TILE)
        for s in range(2):
            rel = lane - AUG_BASE[s]
            aug = jnp.where(rel == tile, 1.0, 0.0)
            aug = jnp.where(jnp.logical_and(rel >= AUG_POS, rel < AUG_POS + AUG_SLOPE_PIECES), pos, aug)
            aug = jnp.where(jnp.logical_and(rel >= AUG_POS + AUG_SLOPE_PIECES,
                                            rel < AUG_POS + 2 * AUG_SLOPE_PIECES), tile_pos, aug)
            aug = aug.astype(BF16)
            own = jnp.logical_and(rel >= -HEAD_DIM, rel < 0) if s == 0 else rel >= HEAD_DIM
            for c0 in range(0, D_MODEL, PAIR):
                ka_ref[s, rows, c0:c0 + PAIR] = jnp.where(own, k[:, c0:c0 + PAIR], aug)


def _qkv_projection(x, g, w_in, layer, *, batch, q_scale):
    m, d = x.shape
    t = m // batch
    spb = t // ROW_TILE
    spc = ATT_CHUNK // ROW_TILE
    spq = ATT_TILE // ROW_TILE
    assert t // KEY_TILE <= AUG_BLOCKS and ATT_CHUNK % ROW_TILE == 0 and t % ATT_CHUNK == 0
    assert ATT_TILE % ROW_TILE == 0 and ROW_TILE % ATT_CHUNK == 0 and t % ATT_TILE == 0
    return pl.pallas_call(
        functools.partial(_qkv_kernel, q_scale=q_scale, steps_per_seq=spb),
        grid=(m // ROW_TILE,),
        in_specs=[
            pl.BlockSpec((ROW_TILE, d), lambda i: (i, 0)),
            pl.BlockSpec((1, d), lambda i: (0, 0)),
            _layer_weight_spec(w_in, layer),
        ],
        out_specs=[
            pl.BlockSpec((1, 1, d, ROW_TILE), lambda i: (i // spb, (i % spb) // spq, 0, i % spq)),
            pl.BlockSpec((2, ROW_TILE, d), lambda i: (0, i, 0)),
            pl.BlockSpec((1, 1, d, ROW_TILE), lambda i: (i // spb, (i % spb) // spc, 0, i % spc)),
        ],
        out_shape=[
            jax.ShapeDtypeStruct((batch, t // ATT_TILE, d, ATT_TILE), BF16),
            jax.ShapeDtypeStruct((2, m, d), BF16),
            jax.ShapeDtypeStruct((batch, t // ATT_CHUNK, d, ATT_CHUNK), BF16),
        ],
        scratch_shapes=[pltpu.VMEM((d, d), BF16)] * 3,
        compiler_params=pltpu.CompilerParams(
            dimension_semantics=("arbitrary",), vmem_limit_bytes=V7X_VMEM_LIMIT_BYTES),
        name="qkv_projection",
    )(x, g.reshape(1, d), w_in)


def _attn_out_mlp_up_kernel(a_ref, wo_ref, r_ref, g_ref, w1_ref, h_ref, u_ref, wo_b_ref, w1_b_ref):
    @pl.when(pl.program_id(0) == 0)
    def _():
        _cast_columns(wo_b_ref, wo_ref, 0, wo_ref.shape[1])
        _cast_columns(w1_b_ref, w1_ref, 0, w1_ref.shape[1])

    h = r_ref[...] + lax.dot_general(a_ref[...], wo_b_ref[...], (((0,), (0,)), ((), ())),
                                     preferred_element_type=F32)
    h_ref[...] = h
    xn = _rmsnorm_rows(h, g_ref[...]).astype(BF16)
    for c0 in range(0, w1_b_ref.shape[1], COL_TILE):
        acc = jnp.dot(xn, w1_b_ref[:, c0:c0 + COL_TILE], preferred_element_type=F32)
        u_ref[:, c0:c0 + COL_TILE] = jnp.square(jnp.maximum(acc, 0.0)).astype(u_ref.dtype)


def _attn_out_mlp_up(a_t, w_out, r, g, w_ff1, layer):
    m, d = r.shape
    n = w_ff1.shape[2]
    spq = a_t.shape[3] // ROW_TILE
    spb = a_t.shape[1] * spq
    assert a_t.shape[2] == d and a_t.shape[3] % ROW_TILE == 0
    rows = lambda width: pl.BlockSpec((ROW_TILE, width), lambda i: (i, 0))
    return pl.pallas_call(
        _attn_out_mlp_up_kernel,
        grid=(m // ROW_TILE,),
        in_specs=[pl.BlockSpec((None, None, d, ROW_TILE),
                               lambda i: (i // spb, (i % spb) // spq, 0, i % spq)),
                  _layer_weight_spec(w_out, layer), rows(d),
                  pl.BlockSpec((1, d), lambda i: (0, 0)), _layer_weight_spec(w_ff1, layer)],
        out_specs=[rows(d), rows(n)],
        out_shape=[jax.ShapeDtypeStruct((m, d), F32), jax.ShapeDtypeStruct((m, n), BF16)],
        scratch_shapes=[pltpu.VMEM((d, d), BF16), pltpu.VMEM((d, n), BF16)],
        compiler_params=pltpu.CompilerParams(
            dimension_semantics=("arbitrary",), vmem_limit_bytes=V7X_VMEM_LIMIT_BYTES),
        name="attn_out_mlp_up",
    )(a_t, w_out, r, g.reshape(1, d), w_ff1)


def _mlp_down_kernel(a_ref, w_ref, r_ref, g_ref, o_ref, w_b_ref, *, final_norm):
    @pl.when(pl.program_id(0) == 0)
    def _():
        _cast_columns(w_b_ref, w_ref, 0, w_ref.shape[1])

    h = r_ref[...] + jnp.dot(a_ref[...], w_b_ref[...], preferred_element_type=F32)
    if final_norm:
        h = _rmsnorm_rows(h, g_ref[...])
    o_ref[...] = h


def _mlp_down(a, w_ff2, r, g, layer, *, final_norm):
    m, k = a.shape
    n = w_ff2.shape[2]
    return pl.pallas_call(
        functools.partial(_mlp_down_kernel, final_norm=final_norm),
        grid=(m // ROW_TILE,),
        in_specs=[
            pl.BlockSpec((ROW_TILE, k), lambda i: (i, 0)),
            _layer_weight_spec(w_ff2, layer),
            pl.BlockSpec((ROW_TILE, n), lambda i: (i, 0)),
            pl.BlockSpec((1, n), lambda i: (0, 0)),
        ],
        out_specs=pl.BlockSpec((ROW_TILE, n), lambda i: (i, 0)),
        out_shape=jax.ShapeDtypeStruct((m, n), F32),
        scratch_shapes=[pltpu.VMEM((k, n), BF16)],
        compiler_params=pltpu.CompilerParams(
            dimension_semantics=("arbitrary",), vmem_limit_bytes=V7X_VMEM_LIMIT_BYTES),
        name="mlp_down",
    )(a, w_ff2, r, g.reshape(1, n))


SUM_ON_MXU = {False: False, True: True}
L_MIN = 2.0 ** -60
UB_MARGIN = 1.01


def _attention_kernel(slope_ref, qt_ref, ka_ref, vt_ref, lam_ref, subln_ref, o_ref,
                      qa_ref, ub_ref, l_ref, acc_ref, km_ref, kn_ref, p_ref, ps_ref, m_ref, *, moba,
                      lam_init):
    group = pl.program_id(1)
    qi = pl.program_id(2)
    t = ATT_TILE
    n_tab = AUG_SLOPE_PIECES + 1
    n_blocks = ka_ref.shape[1] // MOBA_BLOCK
    n_chunks = ka_ref.shape[1] // ATT_CHUNK

    def head_of(g, s):
        pair = group * ATT_GROUP + g
        return 2 * pair + s if moba else pair

    @pl.when(qi == 0)
    def _():
        lane8 = lax.broadcasted_iota(jnp.int32, (8, PAIR), 1)
        for g in range(ATT_GROUP):
            for s in range(2):
                own = (lane8 < HEAD_DIM) if s == 0 else (lane8 >= HEAD_DIM)
                ones_own = jnp.where(own, 1.0, 0.0).astype(BF16)

                def chunk_norms(c, best):
                    r0 = pl.multiple_of(c * ATT_CHUNK, ATT_CHUNK)
                    k_c = ka_ref[s, pl.ds(r0, ATT_CHUNK), g * PAIR:(g + 1) * PAIR].astype(F32)
                    sq = lax.dot_general(ones_own, (k_c * k_c).astype(BF16), _NT,
                                         preferred_element_type=F32)
                    return jnp.maximum(best, sq)

                best = lax.fori_loop(0, n_chunks, chunk_norms, jnp.zeros((8, ATT_CHUNK), F32))
                kn_ref[g, s] = jnp.broadcast_to(
                    jnp.sqrt(jnp.max(best, axis=1, keepdims=True)), (8, PAIR))

    if moba:
        @pl.when(qi == 0)
        def _():
            lane_row = lax.broadcasted_iota(jnp.int32, (1, PAIR), 1)
            km_ref[...] = jnp.zeros_like(km_ref)
            for g in range(ATT_GROUP):
                for n in range(n_blocks):
                    rows = slice(n * MOBA_BLOCK, (n + 1) * MOBA_BLOCK)
                    lanes = slice(g * PAIR, (g + 1) * PAIR)
                    mean_a = jnp.mean(ka_ref[0, rows, lanes].astype(F32), axis=0, keepdims=True)
                    mean_b = jnp.mean(ka_ref[1, rows, lanes].astype(F32), axis=0, keepdims=True)
                    km_ref[g, n:n + 1, :] = jnp.where(lane_row < HEAD_DIM, mean_a, mean_b)

    row16 = lax.broadcasted_iota(jnp.int32, (AUG_ROWS, t), 0)
    own_rows = lax.broadcasted_iota(jnp.int32, (PAIR, t), 0) < HEAD_DIM
    blk = lax.broadcasted_iota(jnp.int32, (AUG_BLOCKS, t), 0)
    blk_f = blk.astype(F32)
    q_idx = lax.broadcasted_iota(jnp.int32, (1, t), 1)
    q_pos = q_idx.astype(F32)
    own_blk = qi * (t // MOBA_BLOCK) + q_idx // MOBA_BLOCK
    for g in range(ATT_GROUP):
        qt = qt_ref[0, 0, g * PAIR:(g + 1) * PAIR, :].astype(F32)
        for s in range(2):
            h = head_of(g, s)
            pos_rows = jnp.zeros((AUG_ROWS, t), F32)
            for piece in range(AUG_SLOPE_PIECES):
                hit = jnp.logical_or(row16 == piece, row16 == piece + AUG_SLOPE_PIECES)
                pos_rows = jnp.where(hit, slope_ref[h * n_tab + piece], pos_rows)
            if moba:
                q_own = jnp.where(own_rows if s == 0 else jnp.logical_not(own_rows), qt, 0.0)
                gate = jnp.dot(km_ref[g], q_own, precision=lax.Precision.HIGHEST,
                               preferred_element_type=F32)
                gt = jnp.where(blk < own_blk, gate, -jnp.inf)
                selected = blk == own_blk
                for pick in range(MOBA_TOPK):
                    mx = jnp.max(gt, axis=0, keepdims=True)
                    idx = jnp.min(jnp.where(gt == mx, blk_f, F32(AUG_BLOCKS)), axis=0,
                                  keepdims=True)
                    hit = jnp.logical_and(blk_f == idx, pick < own_blk)
                    selected = jnp.logical_or(selected, hit)
                    gt = jnp.where(blk_f == idx, -jnp.inf, gt)
                sel_rows = jnp.where(selected, 0.0, NEG_INF)
            else:
                sel_rows = jnp.zeros((AUG_BLOCKS, t), F32)
            own = qt[0:HEAD_DIM] if s == 0 else qt[HEAD_DIM:PAIR]
            q_norm = jnp.sqrt(jnp.sum(own * own, axis=0, keepdims=True))
            ub_ref[g, s] = (q_norm * kn_ref[g, s][0:1, 0:1] * UB_MARGIN
                            + slope_ref[h * n_tab + AUG_SLOPE_PIECES] * q_pos)
            tail = jnp.zeros((HEAD_DIM - AUG_BLOCKS - AUG_ROWS, t), F32)
            parts = [sel_rows, pos_rows, tail]
            parts = [own] + parts if s == 0 else parts + [own]
            qa_ref[g, s] = jnp.concatenate(parts, axis=0).astype(BF16)

    chains = [(g, s) for g in range(ATT_GROUP) for s in range(2)]
    first_own = qi * Q_CHUNKS
    half = ATT_CHUNK

    def chunk_offset(g, s, c):
        return (slope_ref[head_of(g, s) * n_tab + AUG_SLOPE_PIECES]
                * ((c - first_own) * ATT_CHUNK).astype(F32))

    v_rows = HEAD_DIM if moba else PAIR
    ones_rows = jnp.ones((AUG_ROWS, ATT_CHUNK), BF16)

    def key_rows(g, s, c):
        r0 = pl.multiple_of(c * ATT_CHUNK, ATT_CHUNK)
        return ka_ref[s, pl.ds(r0, ATT_CHUNK), g * PAIR:(g + 1) * PAIR]

    def value_rows(g, s, c):
        v0 = g * PAIR + (s * HEAD_DIM if moba else 0)
        return jnp.concatenate([vt_ref[0, c, v0:v0 + v_rows, :], ones_rows], axis=0)

    def probabilities(g, s, c, slot):
        st = jnp.dot(key_rows(g, s, c), qa_ref[g, s], preferred_element_type=F32)
        p = jnp.exp2(st - (ub_ref[g, s] - chunk_offset(g, s, c)))
        p_ref[slot, g, s] = p.astype(BF16)
        if not SUM_ON_MXU[moba]:
            l_ref[g, s] += jnp.sum(p, axis=0, keepdims=True)

    def values(g, s, c, slot):
        if SUM_ON_MXU[moba]:
            pv = jnp.dot(value_rows(g, s, c), p_ref[slot, g, s], preferred_element_type=F32)
            l_ref[g, s] += pv[v_rows:v_rows + 1]
            acc_ref[g, s] += pv[0:v_rows]
        else:
            v0 = g * PAIR + (s * HEAD_DIM if moba else 0)
            acc_ref[g, s] += jnp.dot(vt_ref[0, c, v0:v0 + v_rows, :], p_ref[slot, g, s],
                                     preferred_element_type=F32)

    row_i = lax.broadcasted_iota(jnp.int32, (ATT_CHUNK, t), 0)
    col_i = lax.broadcasted_iota(jnp.int32, (ATT_CHUNK, t), 1)
    mask_first = row_i <= col_i
    mask_second = row_i + half <= col_i

    def first_own_probabilities(g, s):
        st = jnp.dot(key_rows(g, s, first_own), qa_ref[g, s], preferred_element_type=F32)
        p = jnp.exp2(jnp.where(mask_first, st, NEG_INF) - ub_ref[g, s])
        p_ref[0, g, s] = p.astype(BF16)
        if not SUM_ON_MXU[moba]:
            l_ref[g, s] += jnp.sum(p, axis=0, keepdims=True)

    def second_own_probabilities(g, s):
        st = jnp.dot(key_rows(g, s, first_own + 1), qa_ref[g, s, :, half:t],
                     preferred_element_type=F32)
        ref = ub_ref[g, s, :, half:t] - chunk_offset(g, s, first_own + 1)
        p = jnp.exp2(jnp.where(mask_second[:, half:t], st, NEG_INF) - ref)
        p_ref[1, g, s, :, 0:half] = jnp.zeros((ATT_CHUNK, half), BF16)
        p_ref[1, g, s, :, half:t] = p.astype(BF16)
        if not SUM_ON_MXU[moba]:
            l_ref[g, s, :, half:t] += jnp.sum(p, axis=0, keepdims=True)

    def write_output():
        for g in range(ATT_GROUP):
            out_a = acc_ref[g, 0] / l_ref[g, 0]
            out_b = acc_ref[g, 1] / l_ref[g, 1]
            if moba:
                o = jnp.concatenate([out_a, out_b], axis=0)
            else:
                lp = lam_ref[...]
                lam = (jnp.exp(jnp.sum(lp[0:1] * lp[1:2], axis=-1, keepdims=True))
                       - jnp.exp(jnp.sum(lp[2:3] * lp[3:4], axis=-1, keepdims=True)) + lam_init)
                o = out_a - lam * out_b
                rms = lax.rsqrt(jnp.mean(o * o, axis=0, keepdims=True) + RMS_EPS)
                o = o * rms * pltpu.repeat(subln_ref[...], t // PAIR, axis=1) * (1.0 - lam_init)
            o_ref[0, 0, g * PAIR:(g + 1) * PAIR, :] = o.astype(o_ref.dtype)

    l_ref[...] = jnp.zeros_like(l_ref)
    acc_ref[...] = jnp.zeros_like(acc_ref)
    for g, s in chains:
        first_own_probabilities(g, s)
    for g, s in chains:
        second_own_probabilities(g, s)
        values(g, s, first_own, 0)

    def two_steps(j, carry):
        c_odd = jnp.where(j == 0, first_own + 1, 2 * j - 1)
        for g, s in chains:
            probabilities(g, s, 2 * j, 0)
            values(g, s, c_odd, 1)
        for g, s in chains:
            probabilities(g, s, 2 * j + 1, 1)
            values(g, s, 2 * j, 0)
        return carry

    lax.fori_loop(0, qi, two_steps, 0)
    c_last = jnp.where(qi == 0, first_own + 1, first_own - 1)
    for g, s in chains:
        values(g, s, c_last, 1)
    write_output()

    def safe_path():
        m_ref[...] = jnp.full_like(m_ref, NEG_INF)
        l_ref[...] = jnp.zeros_like(l_ref)
        acc_ref[...] = jnp.zeros_like(acc_ref)

        def fold(g, s, c, mask):
            st = jnp.dot(key_rows(g, s, c), qa_ref[g, s], preferred_element_type=F32)
            if mask is not None:
                st = jnp.where(mask, st, NEG_INF)
            off = chunk_offset(g, s, c)
            m_prev = m_ref[g, s]
            m_next = jnp.maximum(m_prev, jnp.max(st, axis=0, keepdims=True) + off)
            alpha = jnp.exp2(m_prev - m_next)
            ps_ref[...] = jnp.exp2(st - (m_next - off)).astype(BF16)
            pv = jnp.dot(value_rows(g, s, c), ps_ref[...], preferred_element_type=F32)
            m_ref[g, s] = m_next
            l_ref[g, s] = alpha * l_ref[g, s] + pv[v_rows:v_rows + 1]
            acc_ref[g, s] = alpha * acc_ref[g, s] + pv[0:v_rows]

        for g, s in chains:
            fold(g, s, first_own, mask_first)
        for g, s in chains:
            fold(g, s, first_own + 1, mask_second)

        def past(c, carry):
            for g, s in chains:
                fold(g, s, c, None)
            return carry

        lax.fori_loop(0, first_own, past, 0)

    l_min = l_ref[0, 0]
    for g, s in chains[1:]:
        l_min = jnp.minimum(l_min, l_ref[g, s])

    @pl.when(jnp.logical_not(jnp.min(l_min) >= L_MIN))
    def _():
        safe_path()
        write_output()


def _attention(qt, ka, vt, slope_table, lam_params, subln_w, *, moba, lam_init):
    b, nq, d, _ = qt.shape
    t = nq * ATT_TILE
    n_blocks = t // MOBA_BLOCK
    assert ATT_TILE == Q_CHUNKS * ATT_CHUNK and Q_CHUNKS == 2 and n_blocks <= AUG_BLOCKS
    assert AUG_POS + 2 * AUG_SLOPE_PIECES <= AUG_BLOCKS + AUG_ROWS <= HEAD_DIM
    gl = ATT_GROUP * PAIR
    return pl.pallas_call(
        functools.partial(_attention_kernel, moba=moba, lam_init=lam_init),
        grid=(b, N_PAIRS // ATT_GROUP, nq),
        in_specs=[
            pl.BlockSpec(memory_space=pltpu.SMEM),
            pl.BlockSpec((1, 1, gl, ATT_TILE), lambda bi, p, i: (bi, i, p, 0)),
            pl.BlockSpec((2, None, t, gl), lambda bi, p, i: (0, bi, 0, p)),
            pl.BlockSpec((1, t // ATT_CHUNK, gl, ATT_CHUNK), lambda bi, p, i: (bi, 0, p, 0)),
            pl.BlockSpec((4, HEAD_DIM), lambda bi, p, i: (0, 0)),
            pl.BlockSpec((PAIR, PAIR), lambda bi, p, i: (0, 0)),
        ],
        out_specs=pl.BlockSpec((1, 1, gl, ATT_TILE), lambda bi, p, i: (bi, i, p, 0)),
        out_shape=jax.ShapeDtypeStruct((b, nq, d, ATT_TILE), BF16),
        scratch_shapes=[
            pltpu.VMEM((ATT_GROUP, 2, PAIR, ATT_TILE), BF16),
            pltpu.VMEM((ATT_GROUP, 2, 1, ATT_TILE), F32),
            pltpu.VMEM((ATT_GROUP, 2, 1, ATT_TILE), F32),
            pltpu.VMEM((ATT_GROUP, 2, HEAD_DIM if moba else PAIR, ATT_TILE), F32),
            pltpu.VMEM((ATT_GROUP, AUG_BLOCKS, PAIR), F32),
            pltpu.VMEM((ATT_GROUP, 2, 8, PAIR), F32),
            pltpu.VMEM((2, ATT_GROUP, 2, ATT_CHUNK, ATT_TILE), BF16),
            pltpu.VMEM((ATT_CHUNK, ATT_TILE), BF16),
            pltpu.VMEM((ATT_GROUP, 2, 1, ATT_TILE), F32),
        ],
        compiler_params=pltpu.CompilerParams(
            dimension_semantics=("arbitrary", "arbitrary", "arbitrary"),
            vmem_limit_bytes=V7X_VMEM_LIMIT_BYTES),
        name="moba_attention" if moba else "diff_attention",
    )(slope_table, qt, ka.reshape(2, b, t, d), vt, lam_params,
      jnp.broadcast_to(subln_w[:, None], (PAIR, PAIR)))


def kernel(x, attn_norm, w_in, w_out, diff_lambda, diff_subln, mlp_norm, w_ff1, w_ff2, final_norm):
    b, t, d = x.shape
    h = x.reshape(b * t, d)
    q_scale = HEAD_DIM ** -0.5 * LOG2E
    for i in range(DEPTH):
        moba = i % N_MIXERS == 1
        li = i // N_MIXERS
        qt, ka, vt = _qkv_projection(h, attn_norm[i], w_in, i, batch=b, q_scale=q_scale)
        slope_table = jnp.asarray(_alibi_slope_pieces(MOBA_HEADS if moba else DIFF_HEADS))
        mix = _attention(qt, ka, vt, slope_table, diff_lambda[li], diff_subln[li],
                         moba=moba, lam_init=_lambda_init(i))
        h, u = _attn_out_mlp_up(mix, w_out, h, mlp_norm[i], w_ff1, i)
        h = _mlp_down(u, w_ff2, h, final_norm, i, final_norm=(i == DEPTH - 1))
    return h.reshape(b, t, d)
```

```python
import functools
import math

import jax
import jax.numpy as jnp
import numpy as np
from jax import lax
from jax.experimental import pallas as pl
from jax.experimental.pallas import tpu as pltpu

D_MODEL = 1024
DEPTH = 2
N_MIXERS = 2
DIFF_HEADS = 8
MOBA_HEADS = 16
HEAD_DIM = 64
PAIR = 2 * HEAD_DIM
N_PAIRS = D_MODEL // PAIR
MOBA_BLOCK = 256
MOBA_TOPK = 3
D_FF = 4 * D_MODEL
RMS_EPS = 1e-6
NEG_INF = -1e30
LOG2E = math.log2(math.e)

V7X_VMEM_LIMIT_BYTES = 56 * 1024 * 1024

ROW_TILE = 512
COL_TILE = 1024
COL_CHUNK = 256
KEY_TILE = MOBA_BLOCK
ATT_CHUNK_TILES = 2
ATT_CHUNK = ATT_CHUNK_TILES * KEY_TILE
Q_CHUNKS = 2
ATT_TILE = Q_CHUNKS * ATT_CHUNK
ATT_GROUP = 2

AUG_BASE = (HEAD_DIM, 0)
AUG_BLOCKS = 32
AUG_POS = AUG_BLOCKS
AUG_SLOPE_PIECES = 3
AUG_ROWS = 16

BF16 = jnp.bfloat16
F32 = jnp.float32


def _alibi_slope_pieces(n_heads):
    slopes = np.array([2.0 ** (-8.0 * (h + 1) / n_heads) for h in range(n_heads)], np.float64)
    rest = slopes * LOG2E
    pieces = []
    for _ in range(AUG_SLOPE_PIECES):
        p = rest.astype(np.float32).astype(BF16).astype(np.float64)
        pieces.append(p)
        rest = rest - p
    pieces = np.stack(pieces, axis=1)
    table = np.concatenate([pieces, pieces.sum(axis=1, keepdims=True)], axis=1)
    return table.astype(np.float32).reshape(-1)


def _lambda_init(layer_idx):
    return 0.8 - 0.6 * math.exp(-0.3 * layer_idx)


def _rmsnorm_rows(x, g):
    return x * lax.rsqrt(jnp.mean(x * x, axis=-1, keepdims=True) + RMS_EPS) * g


_NT = (((1,), (1,)), ((), ()))


def _layer_weight_spec(w, layer):
    return pl.BlockSpec((None,) + w.shape[1:], lambda i: (layer, 0, 0),
                        pipeline_mode=pl.Buffered(1))


def _cast_columns(dst_ref, src_ref, c0, c1, *, transpose=False):
    for j in range(c0, c1, COL_CHUNK):
        blk = src_ref[:, j:j + COL_CHUNK]
        if transpose:
            dst_ref[j - c0:j - c0 + COL_CHUNK, :] = blk.T.astype(BF16)
        else:
            dst_ref[:, j - c0:j - c0 + COL_CHUNK] = blk.astype(BF16)


def _qkv_kernel(x_ref, g_ref, w_ref, qt_ref, ka_ref, vt_ref, wqt_ref, wk_ref, wvt_ref, *,
                q_scale, steps_per_seq):
    @pl.when(pl.program_id(0) == 0)
    def _():
        _cast_columns(wqt_ref, w_ref, 0, D_MODEL, transpose=True)
        _cast_columns(wk_ref, w_ref, D_MODEL, 2 * D_MODEL)
        _cast_columns(wvt_ref, w_ref, 2 * D_MODEL, 3 * D_MODEL, transpose=True)

    lane = lax.broadcasted_iota(jnp.int32, (KEY_TILE, PAIR), 1)
    pos = lax.broadcasted_iota(jnp.int32, (KEY_TILE, PAIR), 0).astype(F32)
    for part in range(ROW_TILE // KEY_TILE):
        rows = slice(part * KEY_TILE, (part + 1) * KEY_TILE)
        xn = _rmsnorm_rows(x_ref[rows], g_ref[...]).astype(BF16)
        for c0 in range(0, D_MODEL, 256):
            vt = lax.dot_general(wvt_ref[c0:c0 + 256, :], xn, _NT, preferred_element_type=F32)
            vt_ref[0, 0, c0:c0 + 256, rows] = vt.astype(BF16)
            qt = lax.dot_general(wqt_ref[c0:c0 + 256, :], xn, _NT, preferred_element_type=F32)
            qt_ref[0, 0, c0:c0 + 256, rows] = (qt * q_scale).astype(BF16)
        k = jnp.dot(xn, wk_ref[...], preferred_element_type=F32).astype(BF16)
        tile = (pl.program_id(0) % steps_per_seq) * (ROW_TILE // KEY_TILE) + part
        tile_pos = float((part % ATT_CHUNK_TILES) * KEY_TILE)
        for s in range(2):
            rel = lane - AUG_BASE[s]
            aug = jnp.where(rel == tile, 1.0, 0.0)
            aug = jnp.where(jnp.logical_and(rel >= AUG_POS, rel < AUG_POS + AUG_SLOPE_PIECES), pos, aug)
            aug = jnp.where(jnp.logical_and(rel >= AUG_POS + AUG_SLOPE_PIECES,
                                            rel < AUG_POS + 2 * AUG_SLOPE_PIECES), tile_pos, aug)
            aug = aug.astype(BF16)
            own = jnp.logical_and(rel >= -HEAD_DIM, rel < 0) if s == 0 else rel >= HEAD_DIM
            for c0 in range(0, D_MODEL, PAIR):
                ka_ref[s, rows, c0:c0 + PAIR] = jnp.where(own, k[:, c0:c0 + PAIR], aug)


def _qkv_projection(x, g, w_in, layer, *, batch, q_scale):
    m, d = x.shape
    t = m // batch
    spb = t // ROW_TILE
    spc = ATT_CHUNK // ROW_TILE
    spq = ATT_TILE // ROW_TILE
    assert t // KEY_TILE <= AUG_BLOCKS and ATT_CHUNK % ROW_TILE == 0 and t % ATT_CHUNK == 0
    assert ATT_TILE % ROW_TILE == 0 and ROW_TILE % ATT_CHUNK == 0 and t % ATT_TILE == 0
    return pl.pallas_call(
        functools.partial(_qkv_kernel, q_scale=q_scale, steps_per_seq=spb),
        grid=(m // ROW_TILE,),
        in_specs=[
            pl.BlockSpec((ROW_TILE, d), lambda i: (i, 0)),
            pl.BlockSpec((1, d), lambda i: (0, 0)),
            _layer_weight_spec(w_in, layer),
        ],
        out_specs=[
            pl.BlockSpec((1, 1, d, ROW_TILE), lambda i: (i // spb, (i % spb) // spq, 0, i % spq)),
            pl.BlockSpec((2, ROW_TILE, d), lambda i: (0, i, 0)),
            pl.BlockSpec((1, 1, d, ROW_TILE), lambda i: (i // spb, (i % spb) // spc, 0, i % spc)),
        ],
        out_shape=[
            jax.ShapeDtypeStruct((batch, t // ATT_TILE, d, ATT_TILE), BF16),
            jax.ShapeDtypeStruct((2, m, d), BF16),
            jax.ShapeDtypeStruct((batch, t // ATT_CHUNK, d, ATT_CHUNK), BF16),
        ],
        scratch_shapes=[pltpu.VMEM((d, d), BF16)] * 3,
        compiler_params=pltpu.CompilerParams(
            dimension_semantics=("arbitrary",), vmem_limit_bytes=V7X_VMEM_LIMIT_BYTES),
        name="qkv_projection",
    )(x, g.reshape(1, d), w_in)


def _attn_out_mlp_up_kernel(a_ref, wo_ref, r_ref, g_ref, w1_ref, h_ref, u_ref, wo_b_ref, w1_b_ref):
    @pl.when(pl.program_id(0) == 0)
    def _():
        _cast_columns(wo_b_ref, wo_ref, 0, wo_ref.shape[1])
        _cast_columns(w1_b_ref, w1_ref, 0, w1_ref.shape[1])

    h = r_ref[...] + lax.dot_general(a_ref[...], wo_b_ref[...], (((0,), (0,)), ((), ())),
                                     preferred_element_type=F32)
    h_ref[...] = h
    xn = _rmsnorm_rows(h, g_ref[...]).astype(BF16)
    for c0 in range(0, w1_b_ref.shape[1], COL_TILE):
        acc = jnp.dot(xn, w1_b_ref[:, c0:c0 + COL_TILE], preferred_element_type=F32)
        u_ref[:, c0:c0 + COL_TILE] = jnp.square(jnp.maximum(acc, 0.0)).astype(u_ref.dtype)


def _attn_out_mlp_up(a_t, w_out, r, g, w_ff1, layer):
    m, d = r.shape
    n = w_ff1.shape[2]
    spq = a_t.shape[3] // ROW_TILE
    spb = a_t.shape[1] * spq
    assert a_t.shape[2] == d and a_t.shape[3] % ROW_TILE == 0
    rows = lambda width: pl.BlockSpec((ROW_TILE, width), lambda i: (i, 0))
    return pl.pallas_call(
        _attn_out_mlp_up_kernel,
        grid=(m // ROW_TILE,),
        in_specs=[pl.BlockSpec((None, None, d, ROW_TILE),
                               lambda i: (i // spb, (i % spb) // spq, 0, i % spq)),
                  _layer_weight_spec(w_out, layer), rows(d),
                  pl.BlockSpec((1, d), lambda i: (0, 0)), _layer_weight_spec(w_ff1, layer)],
        out_specs=[rows(d), rows(n)],
        out_shape=[jax.ShapeDtypeStruct((m, d), F32), jax.ShapeDtypeStruct((m, n), BF16)],
        scratch_shapes=[pltpu.VMEM((d, d), BF16), pltpu.VMEM((d, n), BF16)],
        compiler_params=pltpu.CompilerParams(
            dimension_semantics=("arbitrary",), vmem_limit_bytes=V7X_VMEM_LIMIT_BYTES),
        name="attn_out_mlp_up",
    )(a_t, w_out, r, g.reshape(1, d), w_ff1)


def _mlp_down_kernel(a_ref, w_ref, r_ref, g_ref, o_ref, w_b_ref, *, final_norm):
    @pl.when(pl.program_id(0) == 0)
    def _():
        _cast_columns(w_b_ref, w_ref, 0, w_ref.shape[1])

    h = r_ref[...] + jnp.dot(a_ref[...], w_b_ref[...], preferred_element_type=F32)
    if final_norm:
        h = _rmsnorm_rows(h, g_ref[...])
    o_ref[...] = h


def _mlp_down(a, w_ff2, r, g, layer, *, final_norm):
    m, k = a.shape
    n = w_ff2.shape[2]
    return pl.pallas_call(
        functools.partial(_mlp_down_kernel, final_norm=final_norm),
        grid=(m // ROW_TILE,),
        in_specs=[
            pl.BlockSpec((ROW_TILE, k), lambda i: (i, 0)),
            _layer_weight_spec(w_ff2, layer),
            pl.BlockSpec((ROW_TILE, n), lambda i: (i, 0)),
            pl.BlockSpec((1, n), lambda i: (0, 0)),
        ],
        out_specs=pl.BlockSpec((ROW_TILE, n), lambda i: (i, 0)),
        out_shape=jax.ShapeDtypeStruct((m, n), F32),
        scratch_shapes=[pltpu.VMEM((k, n), BF16)],
        compiler_params=pltpu.CompilerParams(
            dimension_semantics=("arbitrary",), vmem_limit_bytes=V7X_VMEM_LIMIT_BYTES),
        name="mlp_down",
    )(a, w_ff2, r, g.reshape(1, n))


SUM_ON_MXU = {False: False, True: True}
L_MIN = 2.0 ** -60
UB_MARGIN = 1.01
KM_PIECES = 3
NORM_UNROLL = 4


def _attention_kernel(slope_ref, qt_ref, ka_ref, vt_ref, lam_ref, subln_ref, o_ref,
                      qa_ref, ub_ref, l_ref, acc_ref, km_ref, kn_ref, p_ref, ps_ref, m_ref, kmp_ref, *, moba,
                      lam_init):
    group = pl.program_id(1)
    qi = pl.program_id(2)
    t = ATT_TILE
    n_tab = AUG_SLOPE_PIECES + 1
    n_blocks = ka_ref.shape[1] // MOBA_BLOCK
    n_chunks = ka_ref.shape[1] // ATT_CHUNK

    def head_of(g, s):
        pair = group * ATT_GROUP + g
        return 2 * pair + s if moba else pair

    @pl.when(qi == 0)
    def _():
        lane8 = lax.broadcasted_iota(jnp.int32, (8, PAIR), 1)
        row8 = lax.broadcasted_iota(jnp.int32, (8, PAIR), 0)
        selector = jnp.where(row8 == (lane8 >= HEAD_DIM).astype(jnp.int32), 1.0, 0.0).astype(BF16)
        first_lanes = lax.broadcasted_iota(jnp.int32, (ATT_CHUNK, PAIR), 1) < HEAD_DIM
        for g in range(ATT_GROUP):
            def chunk_norms(c4, best):
                for c in range(NORM_UNROLL):
                    r0 = pl.multiple_of((c4 * NORM_UNROLL + c) * ATT_CHUNK, ATT_CHUNK)
                    lanes = slice(g * PAIR, (g + 1) * PAIR)
                    k_pair = jnp.where(first_lanes, ka_ref[0, pl.ds(r0, ATT_CHUNK), lanes],
                                       ka_ref[1, pl.ds(r0, ATT_CHUNK), lanes])
                    sq = lax.dot_general(selector, k_pair * k_pair, _NT,
                                         preferred_element_type=F32)
                    best = jnp.maximum(best, sq)
                return best

            best = lax.fori_loop(0, n_chunks // NORM_UNROLL, chunk_norms,
                                 jnp.zeros((8, ATT_CHUNK), F32))
            k_norm = jnp.sqrt(jnp.max(best, axis=1, keepdims=True))
            for s in range(2):
                kn_ref[g, s] = jnp.broadcast_to(k_norm[s:s + 1], (8, PAIR))

    if moba:
        @pl.when(qi == 0)
        def _():
            lane_row = lax.broadcasted_iota(jnp.int32, (1, PAIR), 1)
            km_ref[...] = jnp.zeros_like(km_ref)
            for g in range(ATT_GROUP):
                for n in range(n_blocks):
                    rows = slice(n * MOBA_BLOCK, (n + 1) * MOBA_BLOCK)
                    lanes = slice(g * PAIR, (g + 1) * PAIR)
                    mean_a = jnp.mean(ka_ref[0, rows, lanes].astype(F32), axis=0, keepdims=True)
                    mean_b = jnp.mean(ka_ref[1, rows, lanes].astype(F32), axis=0, keepdims=True)
                    km_ref[g, n:n + 1, :] = jnp.where(lane_row < HEAD_DIM, mean_a, mean_b)
                rest = km_ref[g]
                for piece in range(KM_PIECES):
                    part = rest.astype(BF16)
                    kmp_ref[g, piece * AUG_BLOCKS:(piece + 1) * AUG_BLOCKS, :] = part
                    rest = rest - part.astype(F32)

    row16 = lax.broadcasted_iota(jnp.int32, (AUG_ROWS, t), 0)
    own_rows = lax.broadcasted_iota(jnp.int32, (PAIR, t), 0) < HEAD_DIM
    blk = lax.broadcasted_iota(jnp.int32, (AUG_BLOCKS, t), 0)
    blk_f = blk.astype(F32)
    q_idx = lax.broadcasted_iota(jnp.int32, (1, t), 1)
    q_pos = q_idx.astype(F32)
    own_blk = qi * (t // MOBA_BLOCK) + q_idx // MOBA_BLOCK
    for g in range(ATT_GROUP):
        qt = qt_ref[0, 0, g * PAIR:(g + 1) * PAIR, :].astype(F32)
        for s in range(2):
            h = head_of(g, s)
            pos_rows = jnp.zeros((AUG_ROWS, t), F32)
            for piece in range(AUG_SLOPE_PIECES):
                hit = jnp.logical_or(row16 == piece, row16 == piece + AUG_SLOPE_PIECES)
                pos_rows = jnp.where(hit, slope_ref[h * n_tab + piece], pos_rows)
            if moba:
                q_own = jnp.where(own_rows if s == 0 else jnp.logical_not(own_rows), qt, 0.0)
                gate_pieces = jnp.dot(kmp_ref[g], q_own.astype(BF16),
                                      preferred_element_type=F32)
                gate = gate_pieces[0:AUG_BLOCKS]
                for piece in range(1, KM_PIECES):
                    gate = gate + gate_pieces[piece * AUG_BLOCKS:(piece + 1) * AUG_BLOCKS]
                gt = jnp.where(blk < own_blk, gate, -jnp.inf)
                selected = blk == own_blk
                for pick in range(MOBA_TOPK):
                    mx = jnp.max(gt, axis=0, keepdims=True)
                    idx = jnp.min(jnp.where(gt == mx, blk_f, F32(AUG_BLOCKS)), axis=0,
                                  keepdims=True)
                    hit = jnp.logical_and(blk_f == idx, pick < own_blk)
                    selected = jnp.logical_or(selected, hit)
                    gt = jnp.where(blk_f == idx, -jnp.inf, gt)
                sel_rows = jnp.where(selected, 0.0, NEG_INF)
            else:
                sel_rows = jnp.zeros((AUG_BLOCKS, t), F32)
            own = qt[0:HEAD_DIM] if s == 0 else qt[HEAD_DIM:PAIR]
            q_norm = jnp.sqrt(jnp.sum(own * own, axis=0, keepdims=True))
            ub_ref[g, s] = (q_norm * kn_ref[g, s][0:1, 0:1] * UB_MARGIN
                            + slope_ref[h * n_tab + AUG_SLOPE_PIECES] * q_pos)
            tail = jnp.zeros((HEAD_DIM - AUG_BLOCKS - AUG_ROWS, t), F32)
            parts = [sel_rows, pos_rows, tail]
            parts = [own] + parts if s == 0 else parts + [own]
            qa_ref[g, s] = jnp.concatenate(parts, axis=0).astype(BF16)

    chains = [(g, s) for g in range(ATT_GROUP) for s in range(2)]
    first_own = qi * Q_CHUNKS
    half = ATT_CHUNK

    def chunk_offset(g, s, c):
        return (slope_ref[head_of(g, s) * n_tab + AUG_SLOPE_PIECES]
                * ((c - first_own) * ATT_CHUNK).astype(F32))

    v_rows = HEAD_DIM if moba else PAIR
    ones_rows = jnp.ones((AUG_ROWS, ATT_CHUNK), BF16)

    def key_rows(g, s, c):
        r0 = pl.multiple_of(c * ATT_CHUNK, ATT_CHUNK)
        return ka_ref[s, pl.ds(r0, ATT_CHUNK), g * PAIR:(g + 1) * PAIR]

    def value_rows(g, s, c):
        v0 = g * PAIR + (s * HEAD_DIM if moba else 0)
        return jnp.concatenate([vt_ref[0, c, v0:v0 + v_rows, :], ones_rows], axis=0)

    def probabilities(g, s, c, slot):
        st = jnp.dot(key_rows(g, s, c), qa_ref[g, s], preferred_element_type=F32)
        p = jnp.exp2(st - (ub_ref[g, s] - chunk_offset(g, s, c)))
        p_ref[slot, g, s] = p.astype(BF16)
        if not SUM_ON_MXU[moba]:
            l_ref[g, s] += jnp.sum(p, axis=0, keepdims=True)

    def values(g, s, c, slot):
        if SUM_ON_MXU[moba]:
            pv = jnp.dot(value_rows(g, s, c), p_ref[slot, g, s], preferred_element_type=F32)
            l_ref[g, s] += pv[v_rows:v_rows + 1]
            acc_ref[g, s] += pv[0:v_rows]
        else:
            v0 = g * PAIR + (s * HEAD_DIM if moba else 0)
            acc_ref[g, s] += jnp.dot(vt_ref[0, c, v0:v0 + v_rows, :], p_ref[slot, g, s],
                                     preferred_element_type=F32)

    row_i = lax.broadcasted_iota(jnp.int32, (ATT_CHUNK, t), 0)
    col_i = lax.broadcasted_iota(jnp.int32, (ATT_CHUNK, t), 1)
    mask_first = row_i <= col_i
    mask_second = row_i + half <= col_i

    def first_own_probabilities(g, s):
        st = jnp.dot(key_rows(g, s, first_own), qa_ref[g, s], preferred_element_type=F32)
        p = jnp.exp2(jnp.where(mask_first, st, NEG_INF) - ub_ref[g, s])
        p_ref[0, g, s] = p.astype(BF16)
        if not SUM_ON_MXU[moba]:
            l_ref[g, s] += jnp.sum(p, axis=0, keepdims=True)

    def second_own_probabilities(g, s):
        st = jnp.dot(key_rows(g, s, first_own + 1), qa_ref[g, s, :, half:t],
                     preferred_element_type=F32)
        ref = ub_ref[g, s, :, half:t] - chunk_offset(g, s, first_own + 1)
        p = jnp.exp2(jnp.where(mask_second[:, half:t], st, NEG_INF) - ref)
        p_ref[1, g, s, :, 0:half] = jnp.zeros((ATT_CHUNK, half), BF16)
        p_ref[1, g, s, :, half:t] = p.astype(BF16)
        if not SUM_ON_MXU[moba]:
            l_ref[g, s, :, half:t] += jnp.sum(p, axis=0, keepdims=True)

    def write_output():
        for g in range(ATT_GROUP):
            out_a = acc_ref[g, 0] / l_ref[g, 0]
            out_b = acc_ref[g, 1] / l_ref[g, 1]
            if moba:
                o = jnp.concatenate([out_a, out_b], axis=0)
            else:
                lp = lam_ref[...]
                lam = (jnp.exp(jnp.sum(lp[0:1] * lp[1:2], axis=-1, keepdims=True))
                       - jnp.exp(jnp.sum(lp[2:3] * lp[3:4], axis=-1, keepdims=True)) + lam_init)
                o = out_a - lam * out_b
                rms = lax.rsqrt(jnp.mean(o * o, axis=0, keepdims=True) + RMS_EPS)
                o = o * rms * pltpu.repeat(subln_ref[...], t // PAIR, axis=1) * (1.0 - lam_init)
            o_ref[0, 0, g * PAIR:(g + 1) * PAIR, :] = o.astype(o_ref.dtype)

    l_ref[...] = jnp.zeros_like(l_ref)
    acc_ref[...] = jnp.zeros_like(acc_ref)
    for g, s in chains:
        first_own_probabilities(g, s)
    for g, s in chains:
        second_own_probabilities(g, s)
        values(g, s, first_own, 0)

    def two_steps(j, carry):
        c_odd = jnp.where(j == 0, first_own + 1, 2 * j - 1)
        for g, s in chains:
            probabilities(g, s, 2 * j, 0)
            values(g, s, c_odd, 1)
        for g, s in chains:
            probabilities(g, s, 2 * j + 1, 1)
            values(g, s, 2 * j, 0)
        return carry

    lax.fori_loop(0, qi, two_steps, 0)
    c_last = jnp.where(qi == 0, first_own + 1, first_own - 1)
    for g, s in chains:
        values(g, s, c_last, 1)
    write_output()

    def safe_path():
        m_ref[...] = jnp.full_like(m_ref, NEG_INF)
        l_ref[...] = jnp.zeros_like(l_ref)
        acc_ref[...] = jnp.zeros_like(acc_ref)

        def fold(g, s, c, mask):
            st = jnp.dot(key_rows(g, s, c), qa_ref[g, s], preferred_element_type=F32)
            if mask is not None:
                st = jnp.where(mask, st, NEG_INF)
            off = chunk_offset(g, s, c)
            m_prev = m_ref[g, s]
            m_next = jnp.maximum(m_prev, jnp.max(st, axis=0, keepdims=True) + off)
            alpha = jnp.exp2(m_prev - m_next)
            ps_ref[...] = jnp.exp2(st - (m_next - off)).astype(BF16)
            pv = jnp.dot(value_rows(g, s, c), ps_ref[...], preferred_element_type=F32)
            m_ref[g, s] = m_next
            l_ref[g, s] = alpha * l_ref[g, s] + pv[v_rows:v_rows + 1]
            acc_ref[g, s] = alpha * acc_ref[g, s] + pv[0:v_rows]

        for g, s in chains:
            fold(g, s, first_own, mask_first)
        for g, s in chains:
            fold(g, s, first_own + 1, mask_second)

        def past(c, carry):
            for g, s in chains:
                fold(g, s, c, None)
            return carry

        lax.fori_loop(0, first_own, past, 0)

    l_min = l_ref[0, 0]
    for g, s in chains[1:]:
        l_min = jnp.minimum(l_min, l_ref[g, s])

    @pl.when(jnp.logical_not(jnp.min(l_min) >= L_MIN))
    def _():
        safe_path()
        write_output()


def _attention(qt, ka, vt, slope_table, lam_params, subln_w, *, moba, lam_init):
    b, nq, d, _ = qt.shape
    t = nq * ATT_TILE
    n_blocks = t // MOBA_BLOCK
    assert ATT_TILE == Q_CHUNKS * ATT_CHUNK and Q_CHUNKS == 2 and n_blocks <= AUG_BLOCKS
    assert (t // ATT_CHUNK) % NORM_UNROLL == 0
    assert AUG_POS + 2 * AUG_SLOPE_PIECES <= AUG_BLOCKS + AUG_ROWS <= HEAD_DIM
    gl = ATT_GROUP * PAIR
    return pl.pallas_call(
        functools.partial(_attention_kernel, moba=moba, lam_init=lam_init),
        grid=(b, N_PAIRS // ATT_GROUP, nq),
        in_specs=[
            pl.BlockSpec(memory_space=pltpu.SMEM),
            pl.BlockSpec((1, 1, gl, ATT_TILE), lambda bi, p, i: (bi, i, p, 0)),
            pl.BlockSpec((2, None, t, gl), lambda bi, p, i: (0, bi, 0, p)),
            pl.BlockSpec((1, t // ATT_CHUNK, gl, ATT_CHUNK), lambda bi, p, i: (bi, 0, p, 0)),
            pl.BlockSpec((4, HEAD_DIM), lambda bi, p, i: (0, 0)),
            pl.BlockSpec((PAIR, PAIR), lambda bi, p, i: (0, 0)),
        ],
        out_specs=pl.BlockSpec((1, 1, gl, ATT_TILE), lambda bi, p, i: (bi, i, p, 0)),
        out_shape=jax.ShapeDtypeStruct((b, nq, d, ATT_TILE), BF16),
        scratch_shapes=[
            pltpu.VMEM((ATT_GROUP, 2, PAIR, ATT_TILE), BF16),
            pltpu.VMEM((ATT_GROUP, 2, 1, ATT_TILE), F32),
            pltpu.VMEM((ATT_GROUP, 2, 1, ATT_TILE), F32),
            pltpu.VMEM((ATT_GROUP, 2, HEAD_DIM if moba else PAIR, ATT_TILE), F32),
            pltpu.VMEM((ATT_GROUP, AUG_BLOCKS, PAIR), F32),
            pltpu.VMEM((ATT_GROUP, 2, 8, PAIR), F32),
            pltpu.VMEM((2, ATT_GROUP, 2, ATT_CHUNK, ATT_TILE), BF16),
            pltpu.VMEM((ATT_CHUNK, ATT_TILE), BF16),
            pltpu.VMEM((ATT_GROUP, 2, 1, ATT_TILE), F32),
            pltpu.VMEM((ATT_GROUP, KM_PIECES * AUG_BLOCKS, PAIR), BF16),
        ],
        compiler_params=pltpu.CompilerParams(
            dimension_semantics=("arbitrary", "arbitrary", "arbitrary"),
            vmem_limit_bytes=V7X_VMEM_LIMIT_BYTES),
        name="moba_attention" if moba else "diff_attention",
    )(slope_table, qt, ka.reshape(2, b, t, d), vt, lam_params,
      jnp.broadcast_to(subln_w[:, None], (PAIR, PAIR)))


def kernel(x, attn_norm, w_in, w_out, diff_lambda, diff_subln, mlp_norm, w_ff1, w_ff2, final_norm):
    b, t, d = x.shape
    h = x.reshape(b * t, d)
    q_scale = HEAD_DIM ** -0.5 * LOG2E
    for i in range(DEPTH):
        moba = i % N_MIXERS == 1
        li = i // N_MIXERS
        qt, ka, vt = _qkv_projection(h, attn_norm[i], w_in, i, batch=b, q_scale=q_scale)
        slope_table = jnp.asarray(_alibi_slope_pieces(MOBA_HEADS if moba else DIFF_HEADS))
        mix = _attention(qt, ka, vt, slope_table, diff_lambda[li], diff_subln[li],
                         moba=moba, lam_init=_lambda_init(i))
        h, u = _attn_out_mlp_up(mix, w_out, h, mlp_norm[i], w_ff1, i)
        h = _mlp_down(u, w_ff2, h, final_norm, i, final_norm=(i == DEPTH - 1))
    return h.reshape(b, t, d)
```

```python
import functools
import math

import jax
import jax.numpy as jnp
import numpy as np
from jax import lax
from jax.experimental import pallas as pl
from jax.experimental.pallas import tpu as pltpu

D_MODEL = 1024
DEPTH = 2
N_MIXERS = 2
DIFF_HEADS = 8
MOBA_HEADS = 16
HEAD_DIM = 64
PAIR = 2 * HEAD_DIM
N_PAIRS = D_MODEL // PAIR
MOBA_BLOCK = 256
MOBA_TOPK = 3
D_FF = 4 * D_MODEL
RMS_EPS = 1e-6
NEG_INF = -1e30
LOG2E = math.log2(math.e)

V7X_VMEM_LIMIT_BYTES = 56 * 1024 * 1024

ROW_TILE = 512
COL_TILE = 1024
COL_CHUNK = 256
KEY_TILE = MOBA_BLOCK
ATT_CHUNK_TILES = 2
ATT_CHUNK = ATT_CHUNK_TILES * KEY_TILE
Q_CHUNKS = 2
ATT_TILE = Q_CHUNKS * ATT_CHUNK
ATT_GROUP = 2

AUG_BASE = (HEAD_DIM, 0)
AUG_BLOCKS = 32
AUG_POS = AUG_BLOCKS
AUG_SLOPE_PIECES = 3
AUG_ROWS = 16

BF16 = jnp.bfloat16
F32 = jnp.float32


def _alibi_slope_pieces(n_heads):
    slopes = np.array([2.0 ** (-8.0 * (h + 1) / n_heads) for h in range(n_heads)], np.float64)
    rest = slopes * LOG2E
    pieces = []
    for _ in range(AUG_SLOPE_PIECES):
        p = rest.astype(np.float32).astype(BF16).astype(np.float64)
        pieces.append(p)
        rest = rest - p
    pieces = np.stack(pieces, axis=1)
    table = np.concatenate([pieces, pieces.sum(axis=1, keepdims=True)], axis=1)
    return table.astype(np.float32).reshape(-1)


def _lambda_init(layer_idx):
    return 0.8 - 0.6 * math.exp(-0.3 * layer_idx)


def _rmsnorm_rows(x, g):
    return x * lax.rsqrt(jnp.mean(x * x, axis=-1, keepdims=True) + RMS_EPS) * g


_NT = (((1,), (1,)), ((), ()))


def _layer_weight_spec(w, layer):
    return pl.BlockSpec((None,) + w.shape[1:], lambda i: (layer, 0, 0),
                        pipeline_mode=pl.Buffered(1))


def _cast_columns(dst_ref, src_ref, c0, c1, *, transpose=False):
    for j in range(c0, c1, COL_CHUNK):
        blk = src_ref[:, j:j + COL_CHUNK]
        if transpose:
            dst_ref[j - c0:j - c0 + COL_CHUNK, :] = blk.T.astype(BF16)
        else:
            dst_ref[:, j - c0:j - c0 + COL_CHUNK] = blk.astype(BF16)


def _qkv_kernel(x_ref, g_ref, w_ref, qt_ref, ka_ref, vt_ref, wqt_ref, wk_ref, wvt_ref, *,
                q_scale, steps_per_seq):
    @pl.when(pl.program_id(0) == 0)
    def _():
        _cast_columns(wqt_ref, w_ref, 0, D_MODEL, transpose=True)
        _cast_columns(wk_ref, w_ref, D_MODEL, 2 * D_MODEL)
        _cast_columns(wvt_ref, w_ref, 2 * D_MODEL, 3 * D_MODEL, transpose=True)

    lane = lax.broadcasted_iota(jnp.int32, (KEY_TILE, PAIR), 1)
    pos = lax.broadcasted_iota(jnp.int32, (KEY_TILE, PAIR), 0).astype(F32)
    for part in range(ROW_TILE // KEY_TILE):
        rows = slice(part * KEY_TILE, (part + 1) * KEY_TILE)
        xn = _rmsnorm_rows(x_ref[rows], g_ref[...]).astype(BF16)
        for c0 in range(0, D_MODEL, 256):
            vt = lax.dot_general(wvt_ref[c0:c0 + 256, :], xn, _NT, preferred_element_type=F32)
            vt_ref[0, 0, c0:c0 + 256, rows] = vt.astype(BF16)
            qt = lax.dot_general(wqt_ref[c0:c0 + 256, :], xn, _NT, preferred_element_type=F32)
            qt_ref[0, 0, c0:c0 + 256, rows] = (qt * q_scale).astype(BF16)
        k = jnp.dot(xn, wk_ref[...], preferred_element_type=F32).astype(BF16)
        tile = (pl.program_id(0) % steps_per_seq) * (ROW_TILE // KEY_TILE) + part
        tile_pos = float((part % ATT_CHUNK_TILES) * KEY_TILE)
        for s in range(2):
            rel = lane - AUG_BASE[s]
            aug = jnp.where(rel == tile, 1.0, 0.0)
            aug = jnp.where(jnp.logical_and(rel >= AUG_POS, rel < AUG_POS + AUG_SLOPE_PIECES), pos, aug)
            aug = jnp.where(jnp.logical_and(rel >= AUG_POS + AUG_SLOPE_PIECES,
                                            rel < AUG_POS + 2 * AUG_SLOPE_PIECES), tile_pos, aug)
            aug = aug.astype(BF16)
            own = jnp.logical_and(rel >= -HEAD_DIM, rel < 0) if s == 0 else rel >= HEAD_DIM
            for c0 in range(0, D_MODEL, PAIR):
                ka_ref[s, rows, c0:c0 + PAIR] = jnp.where(own, k[:, c0:c0 + PAIR], aug)


def _qkv_projection(x, g, w_in, layer, *, batch, q_scale):
    m, d = x.shape
    t = m // batch
    spb = t // ROW_TILE
    spc = ATT_CHUNK // ROW_TILE
    spq = ATT_TILE // ROW_TILE
    assert t // KEY_TILE <= AUG_BLOCKS and ATT_CHUNK % ROW_TILE == 0 and t % ATT_CHUNK == 0
    assert ATT_TILE % ROW_TILE == 0 and ROW_TILE % ATT_CHUNK == 0 and t % ATT_TILE == 0
    return pl.pallas_call(
        functools.partial(_qkv_kernel, q_scale=q_scale, steps_per_seq=spb),
        grid=(m // ROW_TILE,),
        in_specs=[
            pl.BlockSpec((ROW_TILE, d), lambda i: (i, 0)),
            pl.BlockSpec((1, d), lambda i: (0, 0)),
            _layer_weight_spec(w_in, layer),
        ],
        out_specs=[
            pl.BlockSpec((1, 1, d, ROW_TILE), lambda i: (i // spb, (i % spb) // spq, 0, i % spq)),
            pl.BlockSpec((2, ROW_TILE, d), lambda i: (0, i, 0)),
            pl.BlockSpec((1, 1, d, ROW_TILE), lambda i: (i // spb, (i % spb) // spc, 0, i % spc)),
        ],
        out_shape=[
            jax.ShapeDtypeStruct((batch, t // ATT_TILE, d, ATT_TILE), BF16),
            jax.ShapeDtypeStruct((2, m, d), BF16),
            jax.ShapeDtypeStruct((batch, t // ATT_CHUNK, d, ATT_CHUNK), BF16),
        ],
        scratch_shapes=[pltpu.VMEM((d, d), BF16)] * 3,
        compiler_params=pltpu.CompilerParams(
            dimension_semantics=("arbitrary",), vmem_limit_bytes=V7X_VMEM_LIMIT_BYTES),
        name="qkv_projection",
    )(x, g.reshape(1, d), w_in)


def _attn_out_mlp_up_kernel(a_ref, wo_ref, r_ref, g_ref, w1_ref, h_ref, u_ref, wo_b_ref, w1_b_ref):
    @pl.when(pl.program_id(0) == 0)
    def _():
        _cast_columns(wo_b_ref, wo_ref, 0, wo_ref.shape[1])
        _cast_columns(w1_b_ref, w1_ref, 0, w1_ref.shape[1])

    h = r_ref[...] + lax.dot_general(a_ref[...], wo_b_ref[...], (((0,), (0,)), ((), ())),
                                     preferred_element_type=F32)
    h_ref[...] = h
    xn = _rmsnorm_rows(h, g_ref[...]).astype(BF16)
    for c0 in range(0, w1_b_ref.shape[1], COL_TILE):
        acc = jnp.dot(xn, w1_b_ref[:, c0:c0 + COL_TILE], preferred_element_type=F32)
        u_ref[:, c0:c0 + COL_TILE] = jnp.square(jnp.maximum(acc, 0.0)).astype(u_ref.dtype)


def _attn_out_mlp_up(a_t, w_out, r, g, w_ff1, layer):
    m, d = r.shape
    n = w_ff1.shape[2]
    spq = a_t.shape[3] // ROW_TILE
    spb = a_t.shape[1] * spq
    assert a_t.shape[2] == d and a_t.shape[3] % ROW_TILE == 0
    rows = lambda width: pl.BlockSpec((ROW_TILE, width), lambda i: (i, 0))
    return pl.pallas_call(
        _attn_out_mlp_up_kernel,
        grid=(m // ROW_TILE,),
        in_specs=[pl.BlockSpec((None, None, d, ROW_TILE),
                               lambda i: (i // spb, (i % spb) // spq, 0, i % spq)),
                  _layer_weight_spec(w_out, layer), rows(d),
                  pl.BlockSpec((1, d), lambda i: (0, 0)), _layer_weight_spec(w_ff1, layer)],
        out_specs=[rows(d), rows(n)],
        out_shape=[jax.ShapeDtypeStruct((m, d), F32), jax.ShapeDtypeStruct((m, n), BF16)],
        scratch_shapes=[pltpu.VMEM((d, d), BF16), pltpu.VMEM((d, n), BF16)],
        compiler_params=pltpu.CompilerParams(
            dimension_semantics=("arbitrary",), vmem_limit_bytes=V7X_VMEM_LIMIT_BYTES),
        name="attn_out_mlp_up",
    )(a_t, w_out, r, g.reshape(1, d), w_ff1)


def _mlp_down_kernel(a_ref, w_ref, r_ref, g_ref, o_ref, w_b_ref, *, final_norm):
    @pl.when(pl.program_id(0) == 0)
    def _():
        _cast_columns(w_b_ref, w_ref, 0, w_ref.shape[1])

    h = r_ref[...] + jnp.dot(a_ref[...], w_b_ref[...], preferred_element_type=F32)
    if final_norm:
        h = _rmsnorm_rows(h, g_ref[...])
    o_ref[...] = h


def _mlp_down(a, w_ff2, r, g, layer, *, final_norm):
    m, k = a.shape
    n = w_ff2.shape[2]
    return pl.pallas_call(
        functools.partial(_mlp_down_kernel, final_norm=final_norm),
        grid=(m // ROW_TILE,),
        in_specs=[
            pl.BlockSpec((ROW_TILE, k), lambda i: (i, 0)),
            _layer_weight_spec(w_ff2, layer),
            pl.BlockSpec((ROW_TILE, n), lambda i: (i, 0)),
            pl.BlockSpec((1, n), lambda i: (0, 0)),
        ],
        out_specs=pl.BlockSpec((ROW_TILE, n), lambda i: (i, 0)),
        out_shape=jax.ShapeDtypeStruct((m, n), F32),
        scratch_shapes=[pltpu.VMEM((k, n), BF16)],
        compiler_params=pltpu.CompilerParams(
            dimension_semantics=("arbitrary",), vmem_limit_bytes=V7X_VMEM_LIMIT_BYTES),
        name="mlp_down",
    )(a, w_ff2, r, g.reshape(1, n))


SUM_ON_MXU = {False: False, True: True}
L_MIN = 2.0 ** -60
UB_MARGIN = 1.01
KM_PIECES = 3
NORM_UNROLL = 4


def _attention_kernel(slope_ref, qt_ref, ka_ref, vt_ref, lam_ref, subln_ref, o_ref,
                      qa_ref, ub_ref, l_ref, acc_ref, km_ref, kn_ref, p_ref, ps_ref, m_ref, kmp_ref, *, moba,
                      lam_init):
    group = pl.program_id(1)
    qi = pl.program_id(2)
    t = ATT_TILE
    n_tab = AUG_SLOPE_PIECES + 1
    n_blocks = ka_ref.shape[1] // MOBA_BLOCK
    n_chunks = ka_ref.shape[1] // ATT_CHUNK

    def head_of(g, s):
        pair = group * ATT_GROUP + g
        return 2 * pair + s if moba else pair

    @pl.when(qi == 0)
    def _():
        lane8 = lax.broadcasted_iota(jnp.int32, (8, PAIR), 1)
        row8 = lax.broadcasted_iota(jnp.int32, (8, PAIR), 0)
        selector = jnp.where(row8 == (lane8 >= HEAD_DIM).astype(jnp.int32), 1.0, 0.0).astype(BF16)
        first_lanes = lax.broadcasted_iota(jnp.int32, (ATT_CHUNK, PAIR), 1) < HEAD_DIM
        for g in range(ATT_GROUP):
            def chunk_norms(c4, best):
                for c in range(NORM_UNROLL):
                    r0 = pl.multiple_of((c4 * NORM_UNROLL + c) * ATT_CHUNK, ATT_CHUNK)
                    lanes = slice(g * PAIR, (g + 1) * PAIR)
                    k_pair = jnp.where(first_lanes, ka_ref[0, pl.ds(r0, ATT_CHUNK), lanes],
                                       ka_ref[1, pl.ds(r0, ATT_CHUNK), lanes])
                    sq = lax.dot_general(selector, k_pair * k_pair, _NT,
                                         preferred_element_type=F32)
                    best = jnp.maximum(best, sq)
                return best

            best = lax.fori_loop(0, n_chunks // NORM_UNROLL, chunk_norms,
                                 jnp.zeros((8, ATT_CHUNK), F32))
            k_norm = jnp.sqrt(jnp.max(best, axis=1, keepdims=True))
            for s in range(2):
                kn_ref[g, s] = jnp.broadcast_to(k_norm[s:s + 1], (8, PAIR))

    if moba:
        @pl.when(qi == 0)
        def _():
            lane_row = lax.broadcasted_iota(jnp.int32, (1, PAIR), 1)
            km_ref[...] = jnp.zeros_like(km_ref)
            for g in range(ATT_GROUP):
                for n in range(n_blocks):
                    rows = slice(n * MOBA_BLOCK, (n + 1) * MOBA_BLOCK)
                    lanes = slice(g * PAIR, (g + 1) * PAIR)
                    mean_a = jnp.mean(ka_ref[0, rows, lanes].astype(F32), axis=0, keepdims=True)
                    mean_b = jnp.mean(ka_ref[1, rows, lanes].astype(F32), axis=0, keepdims=True)
                    km_ref[g, n:n + 1, :] = jnp.where(lane_row < HEAD_DIM, mean_a, mean_b)
                rest = km_ref[g]
                for piece in range(KM_PIECES):
                    part = rest.astype(BF16)
                    kmp_ref[g, piece * AUG_BLOCKS:(piece + 1) * AUG_BLOCKS, :] = part
                    rest = rest - part.astype(F32)

    row16 = lax.broadcasted_iota(jnp.int32, (AUG_ROWS, t), 0)
    own_rows = lax.broadcasted_iota(jnp.int32, (PAIR, t), 0) < HEAD_DIM
    blk = lax.broadcasted_iota(jnp.int32, (AUG_BLOCKS, t), 0)
    blk_f = blk.astype(F32)
    q_idx = lax.broadcasted_iota(jnp.int32, (1, t), 1)
    q_pos = q_idx.astype(F32)
    own_blk = qi * (t // MOBA_BLOCK) + q_idx // MOBA_BLOCK
    for g in range(ATT_GROUP):
        qt = qt_ref[0, 0, g * PAIR:(g + 1) * PAIR, :].astype(F32)
        for s in range(2):
            h = head_of(g, s)
            pos_rows = jnp.zeros((AUG_ROWS, t), F32)
            for piece in range(AUG_SLOPE_PIECES):
                hit = jnp.logical_or(row16 == piece, row16 == piece + AUG_SLOPE_PIECES)
                pos_rows = jnp.where(hit, slope_ref[h * n_tab + piece], pos_rows)
            if moba:
                q_own = jnp.where(own_rows if s == 0 else jnp.logical_not(own_rows), qt, 0.0)
                gate_pieces = jnp.dot(kmp_ref[g], q_own.astype(BF16),
                                      preferred_element_type=F32)
                gate = gate_pieces[0:AUG_BLOCKS]
                for piece in range(1, KM_PIECES):
                    gate = gate + gate_pieces[piece * AUG_BLOCKS:(piece + 1) * AUG_BLOCKS]
                gt = jnp.where(blk < own_blk, gate, -jnp.inf)
                selected = blk == own_blk
                for pick in range(MOBA_TOPK):
                    mx = jnp.max(gt, axis=0, keepdims=True)
                    idx = jnp.min(jnp.where(gt == mx, blk_f, F32(AUG_BLOCKS)), axis=0,
                                  keepdims=True)
                    hit = jnp.logical_and(blk_f == idx, pick < own_blk)
                    selected = jnp.logical_or(selected, hit)
                    gt = jnp.where(blk_f == idx, -jnp.inf, gt)
                sel_rows = jnp.where(selected, 0.0, NEG_INF)
            else:
                sel_rows = jnp.zeros((AUG_BLOCKS, t), F32)
            own = qt[0:HEAD_DIM] if s == 0 else qt[HEAD_DIM:PAIR]
            q_norm = jnp.sqrt(jnp.sum(own * own, axis=0, keepdims=True))
            ub_ref[g, s] = (q_norm * kn_ref[g, s][0:1, 0:1] * UB_MARGIN
                            + slope_ref[h * n_tab + AUG_SLOPE_PIECES] * q_pos)
            tail = jnp.zeros((HEAD_DIM - AUG_BLOCKS - AUG_ROWS, t), F32)
            parts = [sel_rows, pos_rows, tail]
            parts = [own] + parts if s == 0 else parts + [own]
            qa_ref[g, s] = jnp.concatenate(parts, axis=0).astype(BF16)

    chains = [(g, s) for g in range(ATT_GROUP) for s in range(2)]
    first_own = qi * Q_CHUNKS
    half = ATT_CHUNK

    def chunk_offset(g, s, c):
        return (slope_ref[head_of(g, s) * n_tab + AUG_SLOPE_PIECES]
                * ((c - first_own) * ATT_CHUNK).astype(F32))

    v_rows = HEAD_DIM if moba else PAIR
    ones_rows = jnp.ones((AUG_ROWS, ATT_CHUNK), BF16)

    def key_rows(g, s, c):
        r0 = pl.multiple_of(c * ATT_CHUNK, ATT_CHUNK)
        return ka_ref[s, pl.ds(r0, ATT_CHUNK), g * PAIR:(g + 1) * PAIR]

    def value_rows(g, s, c):
        v0 = g * PAIR + (s * HEAD_DIM if moba else 0)
        return jnp.concatenate([vt_ref[0, c, v0:v0 + v_rows, :], ones_rows], axis=0)

    def probabilities(g, s, c, slot):
        st = jnp.dot(key_rows(g, s, c), qa_ref[g, s], preferred_element_type=F32)
        p = jnp.exp2(st - (ub_ref[g, s] - chunk_offset(g, s, c)))
        p_ref[slot, g, s] = p.astype(BF16)
        if not SUM_ON_MXU[moba]:
            l_ref[g, s] += jnp.sum(p, axis=0, keepdims=True)

    def values(g, s, c, slot, first=False):
        if SUM_ON_MXU[moba]:
            pv = jnp.dot(value_rows(g, s, c), p_ref[slot, g, s], preferred_element_type=F32)
            l_new, acc_new = pv[v_rows:v_rows + 1], pv[0:v_rows]
            l_ref[g, s] = l_new if first else l_ref[g, s] + l_new
        else:
            v0 = g * PAIR + (s * HEAD_DIM if moba else 0)
            acc_new = jnp.dot(vt_ref[0, c, v0:v0 + v_rows, :], p_ref[slot, g, s],
                              preferred_element_type=F32)
        acc_ref[g, s] = acc_new if first else acc_ref[g, s] + acc_new

    row_i = lax.broadcasted_iota(jnp.int32, (ATT_CHUNK, t), 0)
    col_i = lax.broadcasted_iota(jnp.int32, (ATT_CHUNK, t), 1)
    mask_first = row_i <= col_i
    mask_second = row_i + half <= col_i

    def first_own_probabilities(g, s):
        st = jnp.dot(key_rows(g, s, first_own), qa_ref[g, s], preferred_element_type=F32)
        p = jnp.exp2(jnp.where(mask_first, st, NEG_INF) - ub_ref[g, s])
        p_ref[0, g, s] = p.astype(BF16)
        if not SUM_ON_MXU[moba]:
            l_ref[g, s] = jnp.sum(p, axis=0, keepdims=True)

    def second_own_probabilities(g, s):
        st = jnp.dot(key_rows(g, s, first_own + 1), qa_ref[g, s, :, half:t],
                     preferred_element_type=F32)
        ref = ub_ref[g, s, :, half:t] - chunk_offset(g, s, first_own + 1)
        p = jnp.exp2(jnp.where(mask_second[:, half:t], st, NEG_INF) - ref)
        p_ref[1, g, s, :, 0:half] = jnp.zeros((ATT_CHUNK, half), BF16)
        p_ref[1, g, s, :, half:t] = p.astype(BF16)
        if not SUM_ON_MXU[moba]:
            l_ref[g, s, :, half:t] += jnp.sum(p, axis=0, keepdims=True)

    def write_output():
        for g in range(ATT_GROUP):
            out_a = acc_ref[g, 0] / l_ref[g, 0]
            out_b = acc_ref[g, 1] / l_ref[g, 1]
            if moba:
                o = jnp.concatenate([out_a, out_b], axis=0)
            else:
                lp = lam_ref[...]
                lam = (jnp.exp(jnp.sum(lp[0:1] * lp[1:2], axis=-1, keepdims=True))
                       - jnp.exp(jnp.sum(lp[2:3] * lp[3:4], axis=-1, keepdims=True)) + lam_init)
                o = out_a - lam * out_b
                rms = lax.rsqrt(jnp.mean(o * o, axis=0, keepdims=True) + RMS_EPS)
                o = o * rms * pltpu.repeat(subln_ref[...], t // PAIR, axis=1) * (1.0 - lam_init)
            o_ref[0, 0, g * PAIR:(g + 1) * PAIR, :] = o.astype(o_ref.dtype)

    for g, s in chains:
        first_own_probabilities(g, s)
    for g, s in chains:
        second_own_probabilities(g, s)
        values(g, s, first_own, 0, first=True)

    def two_steps(j, carry):
        c_odd = jnp.where(j == 0, first_own + 1, 2 * j - 1)
        for g, s in chains:
            probabilities(g, s, 2 * j, 0)
            values(g, s, c_odd, 1)
        for g, s in chains:
            probabilities(g, s, 2 * j + 1, 1)
            values(g, s, 2 * j, 0)
        return carry

    lax.fori_loop(0, qi, two_steps, 0)
    c_last = jnp.where(qi == 0, first_own + 1, first_own - 1)
    for g, s in chains:
        values(g, s, c_last, 1)
    write_output()

    def safe_path():
        m_ref[...] = jnp.full_like(m_ref, NEG_INF)
        l_ref[...] = jnp.zeros_like(l_ref)
        acc_ref[...] = jnp.zeros_like(acc_ref)

        def fold(g, s, c, mask):
            st = jnp.dot(key_rows(g, s, c), qa_ref[g, s], preferred_element_type=F32)
            if mask is not None:
                st = jnp.where(mask, st, NEG_INF)
            off = chunk_offset(g, s, c)
            m_prev = m_ref[g, s]
            m_next = jnp.maximum(m_prev, jnp.max(st, axis=0, keepdims=True) + off)
            alpha = jnp.exp2(m_prev - m_next)
            ps_ref[...] = jnp.exp2(st - (m_next - off)).astype(BF16)
            pv = jnp.dot(value_rows(g, s, c), ps_ref[...], preferred_element_type=F32)
            m_ref[g, s] = m_next
            l_ref[g, s] = alpha * l_ref[g, s] + pv[v_rows:v_rows + 1]
            acc_ref[g, s] = alpha * acc_ref[g, s] + pv[0:v_rows]

        for g, s in chains:
            fold(g, s, first_own, mask_first)
        for g, s in chains:
            fold(g, s, first_own + 1, mask_second)

        def past(c, carry):
            for g, s in chains:
                fold(g, s, c, None)
            return carry

        lax.fori_loop(0, first_own, past, 0)

    l_min = l_ref[0, 0]
    for g, s in chains[1:]:
        l_min = jnp.minimum(l_min, l_ref[g, s])

    @pl.when(jnp.logical_not(jnp.min(l_min) >= L_MIN))
    def _():
        safe_path()
        write_output()


def _attention(qt, ka, vt, slope_table, lam_params, subln_w, *, moba, lam_init):
    b, nq, d, _ = qt.shape
    t = nq * ATT_TILE
    n_blocks = t // MOBA_BLOCK
    assert ATT_TILE == Q_CHUNKS * ATT_CHUNK and Q_CHUNKS == 2 and n_blocks <= AUG_BLOCKS
    assert (t // ATT_CHUNK) % NORM_UNROLL == 0
    assert AUG_POS + 2 * AUG_SLOPE_PIECES <= AUG_BLOCKS + AUG_ROWS <= HEAD_DIM
    gl = ATT_GROUP * PAIR
    return pl.pallas_call(
        functools.partial(_attention_kernel, moba=moba, lam_init=lam_init),
        grid=(b, N_PAIRS // ATT_GROUP, nq),
        in_specs=[
            pl.BlockSpec(memory_space=pltpu.SMEM),
            pl.BlockSpec((1, 1, gl, ATT_TILE), lambda bi, p, i: (bi, i, p, 0)),
            pl.BlockSpec((2, None, t, gl), lambda bi, p, i: (0, bi, 0, p)),
            pl.BlockSpec((1, t // ATT_CHUNK, gl, ATT_CHUNK), lambda bi, p, i: (bi, 0, p, 0)),
            pl.BlockSpec((4, HEAD_DIM), lambda bi, p, i: (0, 0)),
            pl.BlockSpec((PAIR, PAIR), lambda bi, p, i: (0, 0)),
        ],
        out_specs=pl.BlockSpec((1, 1, gl, ATT_TILE), lambda bi, p, i: (bi, i, p, 0)),
        out_shape=jax.ShapeDtypeStruct((b, nq, d, ATT_TILE), BF16),
        scratch_shapes=[
            pltpu.VMEM((ATT_GROUP, 2, PAIR, ATT_TILE), BF16),
            pltpu.VMEM((ATT_GROUP, 2, 1, ATT_TILE), F32),
            pltpu.VMEM((ATT_GROUP, 2, 1, ATT_TILE), F32),
            pltpu.VMEM((ATT_GROUP, 2, HEAD_DIM if moba else PAIR, ATT_TILE), F32),
            pltpu.VMEM((ATT_GROUP, AUG_BLOCKS, PAIR), F32),
            pltpu.VMEM((ATT_GROUP, 2, 8, PAIR), F32),
            pltpu.VMEM((2, ATT_GROUP, 2, ATT_CHUNK, ATT_TILE), BF16),
            pltpu.VMEM((ATT_CHUNK, ATT_TILE), BF16),
            pltpu.VMEM((ATT_GROUP, 2, 1, ATT_TILE), F32),
            pltpu.VMEM((ATT_GROUP, KM_PIECES * AUG_BLOCKS, PAIR), BF16),
        ],
        compiler_params=pltpu.CompilerParams(
            dimension_semantics=("arbitrary", "arbitrary", "arbitrary"),
            vmem_limit_bytes=V7X_VMEM_LIMIT_BYTES),
        name="moba_attention" if moba else "diff_attention",
    )(slope_table, qt, ka.reshape(2, b, t, d), vt, lam_params,
      jnp.broadcast_to(subln_w[:, None], (PAIR, PAIR)))


def kernel(x, attn_norm, w_in, w_out, diff_lambda, diff_subln, mlp_norm, w_ff1, w_ff2, final_norm):
    b, t, d = x.shape
    h = x.reshape(b * t, d)
    q_scale = HEAD_DIM ** -0.5 * LOG2E
    for i in range(DEPTH):
        moba = i % N_MIXERS == 1
        li = i // N_MIXERS
        qt, ka, vt = _qkv_projection(h, attn_norm[i], w_in, i, batch=b, q_scale=q_scale)
        slope_table = jnp.asarray(_alibi_slope_pieces(MOBA_HEADS if moba else DIFF_HEADS))
        mix = _attention(qt, ka, vt, slope_table, diff_lambda[li], diff_subln[li],
                         moba=moba, lam_init=_lambda_init(i))
        h, u = _attn_out_mlp_up(mix, w_out, h, mlp_norm[i], w_ff1, i)
        h = _mlp_down(u, w_ff2, h, final_norm, i, final_norm=(i == DEPTH - 1))
    return h.reshape(b, t, d)
```

```python
import functools
import math

import jax
import jax.numpy as jnp
import numpy as np
from jax import lax
from jax.experimental import pallas as pl
from jax.experimental.pallas import tpu as pltpu

D_MODEL = 1024
DEPTH = 2
N_MIXERS = 2
DIFF_HEADS = 8
MOBA_HEADS = 16
HEAD_DIM = 64
PAIR = 2 * HEAD_DIM
N_PAIRS = D_MODEL // PAIR
MOBA_BLOCK = 256
MOBA_TOPK = 3
D_FF = 4 * D_MODEL
RMS_EPS = 1e-6
NEG_INF = -1e30
LOG2E = math.log2(math.e)

V7X_VMEM_LIMIT_BYTES = 56 * 1024 * 1024

ROW_TILE = 512
COL_TILE = 1024
COL_CHUNK = 256
KEY_TILE = MOBA_BLOCK
ATT_CHUNK_TILES = 2
ATT_CHUNK = ATT_CHUNK_TILES * KEY_TILE
Q_CHUNKS = 2
ATT_TILE = Q_CHUNKS * ATT_CHUNK
ATT_GROUP = 2

AUG_BASE = (HEAD_DIM, 0)
AUG_BLOCKS = 32
AUG_POS = AUG_BLOCKS
AUG_SLOPE_PIECES = 3
AUG_ROWS = 16

BF16 = jnp.bfloat16
F32 = jnp.float32


def _alibi_slope_pieces(n_heads):
    slopes = np.array([2.0 ** (-8.0 * (h + 1) / n_heads) for h in range(n_heads)], np.float64)
    rest = slopes * LOG2E
    pieces = []
    for _ in range(AUG_SLOPE_PIECES):
        p = rest.astype(np.float32).astype(BF16).astype(np.float64)
        pieces.append(p)
        rest = rest - p
    pieces = np.stack(pieces, axis=1)
    table = np.concatenate([pieces, pieces.sum(axis=1, keepdims=True)], axis=1)
    return table.astype(np.float32).reshape(-1)


def _lambda_init(layer_idx):
    return 0.8 - 0.6 * math.exp(-0.3 * layer_idx)


def _rmsnorm_rows(x, g):
    return x * lax.rsqrt(jnp.mean(x * x, axis=-1, keepdims=True) + RMS_EPS) * g


_NT = (((1,), (1,)), ((), ()))


def _layer_weight_spec(w, layer):
    return pl.BlockSpec((None,) + w.shape[1:], lambda i: (layer, 0, 0),
                        pipeline_mode=pl.Buffered(1))


def _cast_columns(dst_ref, src_ref, c0, c1, *, transpose=False):
    for j in range(c0, c1, COL_CHUNK):
        blk = src_ref[:, j:j + COL_CHUNK]
        if transpose:
            dst_ref[j - c0:j - c0 + COL_CHUNK, :] = blk.T.astype(BF16)
        else:
            dst_ref[:, j - c0:j - c0 + COL_CHUNK] = blk.astype(BF16)


def _qkv_kernel(x_ref, g_ref, w_ref, qt_ref, ka_ref, vt_ref, wqt_ref, wk_ref, wvt_ref, *,
                q_scale, steps_per_seq):
    @pl.when(pl.program_id(0) == 0)
    def _():
        _cast_columns(wqt_ref, w_ref, 0, D_MODEL, transpose=True)
        _cast_columns(wk_ref, w_ref, D_MODEL, 2 * D_MODEL)
        _cast_columns(wvt_ref, w_ref, 2 * D_MODEL, 3 * D_MODEL, transpose=True)

    lane = lax.broadcasted_iota(jnp.int32, (KEY_TILE, PAIR), 1)
    pos = lax.broadcasted_iota(jnp.int32, (KEY_TILE, PAIR), 0).astype(F32)
    for part in range(ROW_TILE // KEY_TILE):
        rows = slice(part * KEY_TILE, (part + 1) * KEY_TILE)
        xn = _rmsnorm_rows(x_ref[rows], g_ref[...]).astype(BF16)
        for c0 in range(0, D_MODEL, 256):
            vt = lax.dot_general(wvt_ref[c0:c0 + 256, :], xn, _NT, preferred_element_type=F32)
            vt_ref[0, 0, c0:c0 + 256, rows] = vt.astype(BF16)
            qt = lax.dot_general(wqt_ref[c0:c0 + 256, :], xn, _NT, preferred_element_type=F32)
            qt_ref[0, 0, c0:c0 + 256, rows] = (qt * q_scale).astype(BF16)
        k = jnp.dot(xn, wk_ref[...], preferred_element_type=F32).astype(BF16)
        tile = (pl.program_id(0) % steps_per_seq) * (ROW_TILE // KEY_TILE) + part
        tile_pos = float((part % ATT_CHUNK_TILES) * KEY_TILE)
        for s in range(2):
            rel = lane - AUG_BASE[s]
            aug = jnp.where(rel == tile, 1.0, 0.0)
            aug = jnp.where(jnp.logical_and(rel >= AUG_POS, rel < AUG_POS + AUG_SLOPE_PIECES), pos, aug)
            aug = jnp.where(jnp.logical_and(rel >= AUG_POS + AUG_SLOPE_PIECES,
                                            rel < AUG_POS + 2 * AUG_SLOPE_PIECES), tile_pos, aug)
            aug = aug.astype(BF16)
            own = jnp.logical_and(rel >= -HEAD_DIM, rel < 0) if s == 0 else rel >= HEAD_DIM
            for c0 in range(0, D_MODEL, PAIR):
                ka_ref[s, rows, c0:c0 + PAIR] = jnp.where(own, k[:, c0:c0 + PAIR], aug)


def _qkv_projection(x, g, w_in, layer, *, batch, q_scale):
    m, d = x.shape
    t = m // batch
    spb = t // ROW_TILE
    spc = ATT_CHUNK // ROW_TILE
    spq = ATT_TILE // ROW_TILE
    assert t // KEY_TILE <= AUG_BLOCKS and ATT_CHUNK % ROW_TILE == 0 and t % ATT_CHUNK == 0
    assert ATT_TILE % ROW_TILE == 0 and ROW_TILE % ATT_CHUNK == 0 and t % ATT_TILE == 0
    return pl.pallas_call(
        functools.partial(_qkv_kernel, q_scale=q_scale, steps_per_seq=spb),
        grid=(m // ROW_TILE,),
        in_specs=[
            pl.BlockSpec((ROW_TILE, d), lambda i: (i, 0)),
            pl.BlockSpec((1, d), lambda i: (0, 0)),
            _layer_weight_spec(w_in, layer),
        ],
        out_specs=[
            pl.BlockSpec((1, 1, d, ROW_TILE), lambda i: (i // spb, (i % spb) // spq, 0, i % spq)),
            pl.BlockSpec((2, ROW_TILE, d), lambda i: (0, i, 0)),
            pl.BlockSpec((1, 1, d, ROW_TILE), lambda i: (i // spb, (i % spb) // spc, 0, i % spc)),
        ],
        out_shape=[
            jax.ShapeDtypeStruct((batch, t // ATT_TILE, d, ATT_TILE), BF16),
            jax.ShapeDtypeStruct((2, m, d), BF16),
            jax.ShapeDtypeStruct((batch, t // ATT_CHUNK, d, ATT_CHUNK), BF16),
        ],
        scratch_shapes=[pltpu.VMEM((d, d), BF16)] * 3,
        compiler_params=pltpu.CompilerParams(
            dimension_semantics=("arbitrary",), vmem_limit_bytes=V7X_VMEM_LIMIT_BYTES),
        name="qkv_projection",
    )(x, g.reshape(1, d), w_in)


def _attn_out_mlp_up_kernel(a_ref, wo_ref, r_ref, g_ref, w1_ref, h_ref, u_ref, wo_b_ref, w1_b_ref):
    @pl.when(pl.program_id(0) == 0)
    def _():
        _cast_columns(wo_b_ref, wo_ref, 0, wo_ref.shape[1])
        _cast_columns(w1_b_ref, w1_ref, 0, w1_ref.shape[1])

    h = r_ref[...] + lax.dot_general(a_ref[...], wo_b_ref[...], (((0,), (0,)), ((), ())),
                                     preferred_element_type=F32)
    h_ref[...] = h
    xn = _rmsnorm_rows(h, g_ref[...]).astype(BF16)
    for c0 in range(0, w1_b_ref.shape[1], COL_TILE):
        acc = jnp.dot(xn, w1_b_ref[:, c0:c0 + COL_TILE], preferred_element_type=F32)
        u_ref[:, c0:c0 + COL_TILE] = jnp.square(jnp.maximum(acc, 0.0)).astype(u_ref.dtype)


def _attn_out_mlp_up(a_t, w_out, r, g, w_ff1, layer):
    m, d = r.shape
    n = w_ff1.shape[2]
    spq = a_t.shape[3] // ROW_TILE
    spb = a_t.shape[1] * spq
    assert a_t.shape[2] == d and a_t.shape[3] % ROW_TILE == 0
    rows = lambda width: pl.BlockSpec((ROW_TILE, width), lambda i: (i, 0))
    return pl.pallas_call(
        _attn_out_mlp_up_kernel,
        grid=(m // ROW_TILE,),
        in_specs=[pl.BlockSpec((None, None, d, ROW_TILE),
                               lambda i: (i // spb, (i % spb) // spq, 0, i % spq)),
                  _layer_weight_spec(w_out, layer), rows(d),
                  pl.BlockSpec((1, d), lambda i: (0, 0)), _layer_weight_spec(w_ff1, layer)],
        out_specs=[rows(d), rows(n)],
        out_shape=[jax.ShapeDtypeStruct((m, d), F32), jax.ShapeDtypeStruct((m, n), BF16)],
        scratch_shapes=[pltpu.VMEM((d, d), BF16), pltpu.VMEM((d, n), BF16)],
        compiler_params=pltpu.CompilerParams(
            dimension_semantics=("arbitrary",), vmem_limit_bytes=V7X_VMEM_LIMIT_BYTES),
        name="attn_out_mlp_up",
    )(a_t, w_out, r, g.reshape(1, d), w_ff1)


def _mlp_down_kernel(a_ref, w_ref, r_ref, g_ref, o_ref, w_b_ref, *, final_norm):
    @pl.when(pl.program_id(0) == 0)
    def _():
        _cast_columns(w_b_ref, w_ref, 0, w_ref.shape[1])

    h = r_ref[...] + jnp.dot(a_ref[...], w_b_ref[...], preferred_element_type=F32)
    if final_norm:
        h = _rmsnorm_rows(h, g_ref[...])
    o_ref[...] = h


def _mlp_down(a, w_ff2, r, g, layer, *, final_norm):
    m, k = a.shape
    n = w_ff2.shape[2]
    return pl.pallas_call(
        functools.partial(_mlp_down_kernel, final_norm=final_norm),
        grid=(m // ROW_TILE,),
        in_specs=[
            pl.BlockSpec((ROW_TILE, k), lambda i: (i, 0)),
            _layer_weight_spec(w_ff2, layer),
            pl.BlockSpec((ROW_TILE, n), lambda i: (i, 0)),
            pl.BlockSpec((1, n), lambda i: (0, 0)),
        ],
        out_specs=pl.BlockSpec((ROW_TILE, n), lambda i: (i, 0)),
        out_shape=jax.ShapeDtypeStruct((m, n), F32),
        scratch_shapes=[pltpu.VMEM((k, n), BF16)],
        compiler_params=pltpu.CompilerParams(
            dimension_semantics=("arbitrary",), vmem_limit_bytes=V7X_VMEM_LIMIT_BYTES),
        name="mlp_down",
    )(a, w_ff2, r, g.reshape(1, n))


SUM_ON_MXU = {False: False, True: True}
L_MIN = 2.0 ** -60
UB_MARGIN = 1.01
KM_PIECES = 3
NORM_UNROLL = 4


def _attention_kernel(slope_ref, qt_ref, ka_ref, vt_ref, lam_ref, subln_ref, o_ref,
                      qa_ref, ub_ref, l_ref, acc_ref, km_ref, kn_ref, p_ref, ps_ref, m_ref, kmp_ref, *, moba,
                      lam_init):
    group = pl.program_id(1)
    qi = pl.program_id(2)
    t = ATT_TILE
    n_tab = AUG_SLOPE_PIECES + 1
    n_blocks = ka_ref.shape[1] // MOBA_BLOCK
    n_chunks = ka_ref.shape[1] // ATT_CHUNK

    def head_of(g, s):
        pair = group * ATT_GROUP + g
        return 2 * pair + s if moba else pair

    @pl.when(qi == 0)
    def _():
        lane8 = lax.broadcasted_iota(jnp.int32, (8, PAIR), 1)
        row8 = lax.broadcasted_iota(jnp.int32, (8, PAIR), 0)
        selector = jnp.where(row8 == (lane8 >= HEAD_DIM).astype(jnp.int32), 1.0, 0.0).astype(BF16)
        first_lanes = lax.broadcasted_iota(jnp.int32, (ATT_CHUNK, PAIR), 1) < HEAD_DIM
        for g in range(ATT_GROUP):
            def chunk_norms(c4, best):
                for c in range(NORM_UNROLL):
                    r0 = pl.multiple_of((c4 * NORM_UNROLL + c) * ATT_CHUNK, ATT_CHUNK)
                    lanes = slice(g * PAIR, (g + 1) * PAIR)
                    k_pair = jnp.where(first_lanes, ka_ref[0, pl.ds(r0, ATT_CHUNK), lanes],
                                       ka_ref[1, pl.ds(r0, ATT_CHUNK), lanes])
                    sq = lax.dot_general(selector, k_pair * k_pair, _NT,
                                         preferred_element_type=F32)
                    best = jnp.maximum(best, sq)
                return best

            best = lax.fori_loop(0, n_chunks // NORM_UNROLL, chunk_norms,
                                 jnp.zeros((8, ATT_CHUNK), F32))
            k_norm = jnp.sqrt(jnp.max(best, axis=1, keepdims=True))
            for s in range(2):
                kn_ref[g, s] = jnp.broadcast_to(k_norm[s:s + 1], (8, PAIR))

    if moba:
        @pl.when(qi == 0)
        def _():
            lane_row = lax.broadcasted_iota(jnp.int32, (1, PAIR), 1)
            km_ref[...] = jnp.zeros_like(km_ref)
            for g in range(ATT_GROUP):
                for n in range(n_blocks):
                    rows = slice(n * MOBA_BLOCK, (n + 1) * MOBA_BLOCK)
                    lanes = slice(g * PAIR, (g + 1) * PAIR)
                    mean_a = jnp.mean(ka_ref[0, rows, lanes].astype(F32), axis=0, keepdims=True)
                    mean_b = jnp.mean(ka_ref[1, rows, lanes].astype(F32), axis=0, keepdims=True)
                    km_ref[g, n:n + 1, :] = jnp.where(lane_row < HEAD_DIM, mean_a, mean_b)
                rest = km_ref[g]
                for piece in range(KM_PIECES):
                    part = rest.astype(BF16)
                    kmp_ref[g, piece * AUG_BLOCKS:(piece + 1) * AUG_BLOCKS, :] = part
                    rest = rest - part.astype(F32)

    row16 = lax.broadcasted_iota(jnp.int32, (AUG_ROWS, t), 0)
    own_rows = lax.broadcasted_iota(jnp.int32, (PAIR, t), 0) < HEAD_DIM
    blk = lax.broadcasted_iota(jnp.int32, (AUG_BLOCKS, t), 0)
    blk_f = blk.astype(F32)
    q_idx = lax.broadcasted_iota(jnp.int32, (1, t), 1)
    q_pos = q_idx.astype(F32)
    own_blk = qi * (t // MOBA_BLOCK) + q_idx // MOBA_BLOCK
    for g in range(ATT_GROUP):
        qt = qt_ref[0, 0, g * PAIR:(g + 1) * PAIR, :].astype(F32)
        for s in range(2):
            h = head_of(g, s)
            pos_rows = jnp.zeros((AUG_ROWS, t), F32)
            for piece in range(AUG_SLOPE_PIECES):
                hit = jnp.logical_or(row16 == piece, row16 == piece + AUG_SLOPE_PIECES)
                pos_rows = jnp.where(hit, slope_ref[h * n_tab + piece], pos_rows)
            if moba:
                q_own = jnp.where(own_rows if s == 0 else jnp.logical_not(own_rows), qt, 0.0)
                gate_pieces = jnp.dot(kmp_ref[g], q_own.astype(BF16),
                                      preferred_element_type=F32)
                gate = gate_pieces[0:AUG_BLOCKS]
                for piece in range(1, KM_PIECES):
                    gate = gate + gate_pieces[piece * AUG_BLOCKS:(piece + 1) * AUG_BLOCKS]
                gt = jnp.where(blk < own_blk, gate, -jnp.inf)
                selected = blk == own_blk
                for pick in range(MOBA_TOPK):
                    mx = jnp.max(gt, axis=0, keepdims=True)
                    idx = jnp.min(jnp.where(gt == mx, blk_f, F32(AUG_BLOCKS)), axis=0,
                                  keepdims=True)
                    hit = jnp.logical_and(blk_f == idx, pick < own_blk)
                    selected = jnp.logical_or(selected, hit)
                    gt = jnp.where(blk_f == idx, -jnp.inf, gt)
                sel_rows = jnp.where(selected, 0.0, NEG_INF)
            else:
                sel_rows = jnp.zeros((AUG_BLOCKS, t), F32)
            own = qt[0:HEAD_DIM] if s == 0 else qt[HEAD_DIM:PAIR]
            q_norm = jnp.sqrt(jnp.sum(own * own, axis=0, keepdims=True))
            ub_ref[g, s] = (q_norm * kn_ref[g, s][0:1, 0:1] * UB_MARGIN
                            + slope_ref[h * n_tab + AUG_SLOPE_PIECES] * q_pos)
            tail = jnp.zeros((HEAD_DIM - AUG_BLOCKS - AUG_ROWS, t), F32)
            parts = [sel_rows, pos_rows, tail]
            parts = [own] + parts if s == 0 else parts + [own]
            qa_ref[g, s] = jnp.concatenate(parts, axis=0).astype(BF16)

    chains = [(g, s) for g in range(ATT_GROUP) for s in range(2)]
    first_own = qi * Q_CHUNKS
    half = ATT_CHUNK

    def chunk_offset(g, s, c):
        return (slope_ref[head_of(g, s) * n_tab + AUG_SLOPE_PIECES]
                * ((c - first_own) * ATT_CHUNK).astype(F32))

    v_rows = HEAD_DIM if moba else PAIR
    ones_rows = jnp.ones((AUG_ROWS, ATT_CHUNK), BF16)

    def key_rows(g, s, c):
        r0 = pl.multiple_of(c * ATT_CHUNK, ATT_CHUNK)
        return ka_ref[s, pl.ds(r0, ATT_CHUNK), g * PAIR:(g + 1) * PAIR]

    def value_rows(g, s, c):
        v0 = g * PAIR + (s * HEAD_DIM if moba else 0)
        return jnp.concatenate([vt_ref[0, c, v0:v0 + v_rows, :], ones_rows], axis=0)

    def probabilities(g, s, c, slot):
        st = jnp.dot(key_rows(g, s, c), qa_ref[g, s], preferred_element_type=F32)
        p = jnp.exp2(st - (ub_ref[g, s] - chunk_offset(g, s, c)))
        p_ref[slot, g, s] = p.astype(BF16)
        if not SUM_ON_MXU[moba]:
            l_ref[g, s] += jnp.sum(p, axis=0, keepdims=True)

    def values(g, s, c, slot):
        if SUM_ON_MXU[moba]:
            pv = jnp.dot(value_rows(g, s, c), p_ref[slot, g, s], preferred_element_type=F32)
            l_ref[g, s] += pv[v_rows:v_rows + 1]
            acc_ref[g, s] += pv[0:v_rows]
        else:
            v0 = g * PAIR + (s * HEAD_DIM if moba else 0)
            acc_ref[g, s] += jnp.dot(vt_ref[0, c, v0:v0 + v_rows, :], p_ref[slot, g, s],
                                     preferred_element_type=F32)

    row_i = lax.broadcasted_iota(jnp.int32, (ATT_CHUNK, t), 0)
    col_i = lax.broadcasted_iota(jnp.int32, (ATT_CHUNK, t), 1)
    mask_first = row_i <= col_i
    mask_second = row_i + half <= col_i

    def own_probabilities(g, s, which):
        c = first_own + which
        q0 = which * half
        r0 = pl.multiple_of(c * ATT_CHUNK, ATT_CHUNK)
        lanes = slice(g * PAIR, (g + 1) * PAIR)
        mask = mask_second if which else mask_first
        off = chunk_offset(g, s, c)

        def strip(keys, a, b, masked):
            st = jnp.dot(ka_ref[s, pl.ds(r0, keys), lanes], qa_ref[g, s, :, a:b],
                         preferred_element_type=F32)
            if masked:
                st = jnp.where(mask[0:keys, a:b], st, NEG_INF)
            p = jnp.exp2(st - (ub_ref[g, s, :, a:b] - off))
            p_ref[which, g, s, 0:keys, a:b] = p.astype(BF16)
            if keys < ATT_CHUNK:
                p_ref[which, g, s, keys:ATT_CHUNK, a:b] = jnp.zeros((ATT_CHUNK - keys, b - a), BF16)
            if not SUM_ON_MXU[moba]:
                l_ref[g, s, :, a:b] += jnp.sum(p, axis=0, keepdims=True)

        if q0 > 0:
            p_ref[which, g, s, :, 0:q0] = jnp.zeros((ATT_CHUNK, q0), BF16)
        strip(KEY_TILE, q0, q0 + KEY_TILE, True)
        strip(ATT_CHUNK, q0 + KEY_TILE, q0 + half, True)
        if q0 + half < t:
            strip(ATT_CHUNK, q0 + half, t, False)

    def write_output():
        for g in range(ATT_GROUP):
            out_a = acc_ref[g, 0] / l_ref[g, 0]
            out_b = acc_ref[g, 1] / l_ref[g, 1]
            if moba:
                o = jnp.concatenate([out_a, out_b], axis=0)
            else:
                lp = lam_ref[...]
                lam = (jnp.exp(jnp.sum(lp[0:1] * lp[1:2], axis=-1, keepdims=True))
                       - jnp.exp(jnp.sum(lp[2:3] * lp[3:4], axis=-1, keepdims=True)) + lam_init)
                o = out_a - lam * out_b
                rms = lax.rsqrt(jnp.mean(o * o, axis=0, keepdims=True) + RMS_EPS)
                o = o * rms * pltpu.repeat(subln_ref[...], t // PAIR, axis=1) * (1.0 - lam_init)
            o_ref[0, 0, g * PAIR:(g + 1) * PAIR, :] = o.astype(o_ref.dtype)

    l_ref[...] = jnp.zeros_like(l_ref)
    acc_ref[...] = jnp.zeros_like(acc_ref)
    for g, s in chains:
        own_probabilities(g, s, 0)
    for g, s in chains:
        own_probabilities(g, s, 1)
        values(g, s, first_own, 0)

    def two_steps(j, carry):
        c_odd = jnp.where(j == 0, first_own + 1, 2 * j - 1)
        for g, s in chains:
            probabilities(g, s, 2 * j, 0)
            values(g, s, c_odd, 1)
        for g, s in chains:
            probabilities(g, s, 2 * j + 1, 1)
            values(g, s, 2 * j, 0)
        return carry

    lax.fori_loop(0, qi, two_steps, 0)
    c_last = jnp.where(qi == 0, first_own + 1, first_own - 1)
    for g, s in chains:
        values(g, s, c_last, 1)
    write_output()

    def safe_path():
        m_ref[...] = jnp.full_like(m_ref, NEG_INF)
        l_ref[...] = jnp.zeros_like(l_ref)
        acc_ref[...] = jnp.zeros_like(acc_ref)

        def fold(g, s, c, mask):
            st = jnp.dot(key_rows(g, s, c), qa_ref[g, s], preferred_element_type=F32)
            if mask is not None:
                st = jnp.where(mask, st, NEG_INF)
            off = chunk_offset(g, s, c)
            m_prev = m_ref[g, s]
            m_next = jnp.maximum(m_prev, jnp.max(st, axis=0, keepdims=True) + off)
            alpha = jnp.exp2(m_prev - m_next)
            ps_ref[...] = jnp.exp2(st - (m_next - off)).astype(BF16)
            pv = jnp.dot(value_rows(g, s, c), ps_ref[...], preferred_element_type=F32)
            m_ref[g, s] = m_next
            l_ref[g, s] = alpha * l_ref[g, s] + pv[v_rows:v_rows + 1]
            acc_ref[g, s] = alpha * acc_ref[g, s] + pv[0:v_rows]

        for g, s in chains:
            fold(g, s, first_own, mask_first)
        for g, s in chains:
            fold(g, s, first_own + 1, mask_second)

        def past(c, carry):
            for g, s in chains:
                fold(g, s, c, None)
            return carry

        lax.fori_loop(0, first_own, past, 0)

    l_min = l_ref[0, 0]
    for g, s in chains[1:]:
        l_min = jnp.minimum(l_min, l_ref[g, s])

    @pl.when(jnp.logical_not(jnp.min(l_min) >= L_MIN))
    def _():
        safe_path()
        write_output()


def _attention(qt, ka, vt, slope_table, lam_params, subln_w, *, moba, lam_init):
    b, nq, d, _ = qt.shape
    t = nq * ATT_TILE
    n_blocks = t // MOBA_BLOCK
    assert ATT_TILE == Q_CHUNKS * ATT_CHUNK and Q_CHUNKS == 2 and n_blocks <= AUG_BLOCKS
    assert (t // ATT_CHUNK) % NORM_UNROLL == 0
    assert AUG_POS + 2 * AUG_SLOPE_PIECES <= AUG_BLOCKS + AUG_ROWS <= HEAD_DIM
    gl = ATT_GROUP * PAIR
    return pl.pallas_call(
        functools.partial(_attention_kernel, moba=moba, lam_init=lam_init),
        grid=(b, N_PAIRS // ATT_GROUP, nq),
        in_specs=[
            pl.BlockSpec(memory_space=pltpu.SMEM),
            pl.BlockSpec((1, 1, gl, ATT_TILE), lambda bi, p, i: (bi, i, p, 0)),
            pl.BlockSpec((2, None, t, gl), lambda bi, p, i: (0, bi, 0, p)),
            pl.BlockSpec((1, t // ATT_CHUNK, gl, ATT_CHUNK), lambda bi, p, i: (bi, 0, p, 0)),
            pl.BlockSpec((4, HEAD_DIM), lambda bi, p, i: (0, 0)),
            pl.BlockSpec((PAIR, PAIR), lambda bi, p, i: (0, 0)),
        ],
        out_specs=pl.BlockSpec((1, 1, gl, ATT_TILE), lambda bi, p, i: (bi, i, p, 0)),
        out_shape=jax.ShapeDtypeStruct((b, nq, d, ATT_TILE), BF16),
        scratch_shapes=[
            pltpu.VMEM((ATT_GROUP, 2, PAIR, ATT_TILE), BF16),
            pltpu.VMEM((ATT_GROUP, 2, 1, ATT_TILE), F32),
            pltpu.VMEM((ATT_GROUP, 2, 1, ATT_TILE), F32),
            pltpu.VMEM((ATT_GROUP, 2, HEAD_DIM if moba else PAIR, ATT_TILE), F32),
            pltpu.VMEM((ATT_GROUP, AUG_BLOCKS, PAIR), F32),
            pltpu.VMEM((ATT_GROUP, 2, 8, PAIR), F32),
            pltpu.VMEM((2, ATT_GROUP, 2, ATT_CHUNK, ATT_TILE), BF16),
            pltpu.VMEM((ATT_CHUNK, ATT_TILE), BF16),
            pltpu.VMEM((ATT_GROUP, 2, 1, ATT_TILE), F32),
            pltpu.VMEM((ATT_GROUP, KM_PIECES * AUG_BLOCKS, PAIR), BF16),
        ],
        compiler_params=pltpu.CompilerParams(
            dimension_semantics=("arbitrary", "arbitrary", "arbitrary"),
            vmem_limit_bytes=V7X_VMEM_LIMIT_BYTES),
        name="moba_attention" if moba else "diff_attention",
    )(slope_table, qt, ka.reshape(2, b, t, d), vt, lam_params,
      jnp.broadcast_to(subln_w[:, None], (PAIR, PAIR)))


def kernel(x, attn_norm, w_in, w_out, diff_lambda, diff_subln, mlp_norm, w_ff1, w_ff2, final_norm):
    b, t, d = x.shape
    h = x.reshape(b * t, d)
    q_scale = HEAD_DIM ** -0.5 * LOG2E
    for i in range(DEPTH):
        moba = i % N_MIXERS == 1
        li = i // N_MIXERS
        qt, ka, vt = _qkv_projection(h, attn_norm[i], w_in, i, batch=b, q_scale=q_scale)
        slope_table = jnp.asarray(_alibi_slope_pieces(MOBA_HEADS if moba else DIFF_HEADS))
        mix = _attention(qt, ka, vt, slope_table, diff_lambda[li], diff_subln[li],
                         moba=moba, lam_init=_lambda_init(i))
        h, u = _attn_out_mlp_up(mix, w_out, h, mlp_norm[i], w_ff1, i)
        h = _mlp_down(u, w_ff2, h, final_norm, i, final_norm=(i == DEPTH - 1))
    return h.reshape(b, t, d)
```

```python
import functools
import math

import jax
import jax.numpy as jnp
import numpy as np
from jax import lax
from jax.experimental import pallas as pl
from jax.experimental.pallas import tpu as pltpu

D_MODEL = 1024
DEPTH = 2
N_MIXERS = 2
DIFF_HEADS = 8
MOBA_HEADS = 16
HEAD_DIM = 64
PAIR = 2 * HEAD_DIM
N_PAIRS = D_MODEL // PAIR
MOBA_BLOCK = 256
MOBA_TOPK = 3
D_FF = 4 * D_MODEL
RMS_EPS = 1e-6
NEG_INF = -1e30
LOG2E = math.log2(math.e)

V7X_VMEM_LIMIT_BYTES = 56 * 1024 * 1024

ROW_TILE = 512
COL_TILE = 1024
COL_CHUNK = 256
KEY_TILE = MOBA_BLOCK
ATT_CHUNK_TILES = 2
ATT_CHUNK = ATT_CHUNK_TILES * KEY_TILE
Q_CHUNKS = 2
ATT_TILE = Q_CHUNKS * ATT_CHUNK
ATT_GROUP = 2

AUG_BASE = (HEAD_DIM, 0)
AUG_BLOCKS = 32
AUG_POS = AUG_BLOCKS
AUG_SLOPE_PIECES = 3
AUG_ROWS = 16

BF16 = jnp.bfloat16
F32 = jnp.float32


def _alibi_slope_pieces(n_heads):
    slopes = np.array([2.0 ** (-8.0 * (h + 1) / n_heads) for h in range(n_heads)], np.float64)
    rest = slopes * LOG2E
    pieces = []
    for _ in range(AUG_SLOPE_PIECES):
        p = rest.astype(np.float32).astype(BF16).astype(np.float64)
        pieces.append(p)
        rest = rest - p
    pieces = np.stack(pieces, axis=1)
    table = np.concatenate([pieces, pieces.sum(axis=1, keepdims=True)], axis=1)
    return table.astype(np.float32).reshape(-1)


def _lambda_init(layer_idx):
    return 0.8 - 0.6 * math.exp(-0.3 * layer_idx)


def _rmsnorm_rows(x, g):
    return x * lax.rsqrt(jnp.mean(x * x, axis=-1, keepdims=True) + RMS_EPS) * g


_NT = (((1,), (1,)), ((), ()))


def _layer_weight_spec(w, layer):
    return pl.BlockSpec((None,) + w.shape[1:], lambda i: (layer, 0, 0),
                        pipeline_mode=pl.Buffered(1))


def _cast_columns(dst_ref, src_ref, c0, c1, *, transpose=False):
    for j in range(c0, c1, COL_CHUNK):
        blk = src_ref[:, j:j + COL_CHUNK]
        if transpose:
            dst_ref[j - c0:j - c0 + COL_CHUNK, :] = blk.T.astype(BF16)
        else:
            dst_ref[:, j - c0:j - c0 + COL_CHUNK] = blk.astype(BF16)


def _qkv_kernel(x_ref, g_ref, w_ref, qt_ref, ka_ref, vt_ref, wqt_ref, wk_ref, wvt_ref, *,
                q_scale, steps_per_seq):
    @pl.when(pl.program_id(0) == 0)
    def _():
        _cast_columns(wqt_ref, w_ref, 0, D_MODEL, transpose=True)
        _cast_columns(wk_ref, w_ref, D_MODEL, 2 * D_MODEL)
        _cast_columns(wvt_ref, w_ref, 2 * D_MODEL, 3 * D_MODEL, transpose=True)

    lane = lax.broadcasted_iota(jnp.int32, (KEY_TILE, PAIR), 1)
    pos = lax.broadcasted_iota(jnp.int32, (KEY_TILE, PAIR), 0).astype(F32)
    for part in range(ROW_TILE // KEY_TILE):
        rows = slice(part * KEY_TILE, (part + 1) * KEY_TILE)
        xn = _rmsnorm_rows(x_ref[rows], g_ref[...]).astype(BF16)
        for c0 in range(0, D_MODEL, 256):
            vt = lax.dot_general(wvt_ref[c0:c0 + 256, :], xn, _NT, preferred_element_type=F32)
            vt_ref[0, 0, c0:c0 + 256, rows] = vt.astype(BF16)
            qt = lax.dot_general(wqt_ref[c0:c0 + 256, :], xn, _NT, preferred_element_type=F32)
            qt_ref[0, 0, c0:c0 + 256, rows] = (qt * q_scale).astype(BF16)
        k = jnp.dot(xn, wk_ref[...], preferred_element_type=F32).astype(BF16)
        tile = (pl.program_id(0) % steps_per_seq) * (ROW_TILE // KEY_TILE) + part
        tile_pos = float((part % ATT_CHUNK_TILES) * KEY_TILE)
        for s in range(2):
            rel = lane - AUG_BASE[s]
            aug = jnp.where(rel == tile, 1.0, 0.0)
            aug = jnp.where(jnp.logical_and(rel >= AUG_POS, rel < AUG_POS + AUG_SLOPE_PIECES), pos, aug)
            aug = jnp.where(jnp.logical_and(rel >= AUG_POS + AUG_SLOPE_PIECES,
                                            rel < AUG_POS + 2 * AUG_SLOPE_PIECES), tile_pos, aug)
            aug = aug.astype(BF16)
            own = jnp.logical_and(rel >= -HEAD_DIM, rel < 0) if s == 0 else rel >= HEAD_DIM
            for c0 in range(0, D_MODEL, PAIR):
                ka_ref[s, rows, c0:c0 + PAIR] = jnp.where(own, k[:, c0:c0 + PAIR], aug)


def _qkv_projection(x, g, w_in, layer, *, batch, q_scale):
    m, d = x.shape
    t = m // batch
    spb = t // ROW_TILE
    spc = ATT_CHUNK // ROW_TILE
    spq = ATT_TILE // ROW_TILE
    assert t // KEY_TILE <= AUG_BLOCKS and ATT_CHUNK % ROW_TILE == 0 and t % ATT_CHUNK == 0
    assert ATT_TILE % ROW_TILE == 0 and ROW_TILE % ATT_CHUNK == 0 and t % ATT_TILE == 0
    return pl.pallas_call(
        functools.partial(_qkv_kernel, q_scale=q_scale, steps_per_seq=spb),
        grid=(m // ROW_TILE,),
        in_specs=[
            pl.BlockSpec((ROW_TILE, d), lambda i: (i, 0)),
            pl.BlockSpec((1, d), lambda i: (0, 0)),
            _layer_weight_spec(w_in, layer),
        ],
        out_specs=[
            pl.BlockSpec((1, 1, d, ROW_TILE), lambda i: (i // spb, (i % spb) // spq, 0, i % spq)),
            pl.BlockSpec((2, ROW_TILE, d), lambda i: (0, i, 0)),
            pl.BlockSpec((1, 1, d, ROW_TILE), lambda i: (i // spb, (i % spb) // spc, 0, i % spc)),
        ],
        out_shape=[
            jax.ShapeDtypeStruct((batch, t // ATT_TILE, d, ATT_TILE), BF16),
            jax.ShapeDtypeStruct((2, m, d), BF16),
            jax.ShapeDtypeStruct((batch, t // ATT_CHUNK, d, ATT_CHUNK), BF16),
        ],
        scratch_shapes=[pltpu.VMEM((d, d), BF16)] * 3,
        compiler_params=pltpu.CompilerParams(
            dimension_semantics=("arbitrary",), vmem_limit_bytes=V7X_VMEM_LIMIT_BYTES),
        name="qkv_projection",
    )(x, g.reshape(1, d), w_in)


def _attn_out_mlp_up_kernel(a_ref, wo_ref, r_ref, g_ref, w1_ref, h_ref, u_ref, wo_b_ref, w1_b_ref):
    @pl.when(pl.program_id(0) == 0)
    def _():
        _cast_columns(wo_b_ref, wo_ref, 0, wo_ref.shape[1])
        _cast_columns(w1_b_ref, w1_ref, 0, w1_ref.shape[1])

    h = r_ref[...] + lax.dot_general(a_ref[...], wo_b_ref[...], (((0,), (0,)), ((), ())),
                                     preferred_element_type=F32)
    h_ref[...] = h
    xn = _rmsnorm_rows(h, g_ref[...]).astype(BF16)
    for c0 in range(0, w1_b_ref.shape[1], COL_TILE):
        acc = jnp.dot(xn, w1_b_ref[:, c0:c0 + COL_TILE], preferred_element_type=F32)
        u_ref[:, c0:c0 + COL_TILE] = jnp.square(jnp.maximum(acc, 0.0)).astype(u_ref.dtype)


def _attn_out_mlp_up(a_t, w_out, r, g, w_ff1, layer):
    m, d = r.shape
    n = w_ff1.shape[2]
    spq = a_t.shape[3] // ROW_TILE
    spb = a_t.shape[1] * spq
    assert a_t.shape[2] == d and a_t.shape[3] % ROW_TILE == 0
    rows = lambda width: pl.BlockSpec((ROW_TILE, width), lambda i: (i, 0))
    return pl.pallas_call(
        _attn_out_mlp_up_kernel,
        grid=(m // ROW_TILE,),
        in_specs=[pl.BlockSpec((None, None, d, ROW_TILE),
                               lambda i: (i // spb, (i % spb) // spq, 0, i % spq)),
                  _layer_weight_spec(w_out, layer), rows(d),
                  pl.BlockSpec((1, d), lambda i: (0, 0)), _layer_weight_spec(w_ff1, layer)],
        out_specs=[rows(d), rows(n)],
        out_shape=[jax.ShapeDtypeStruct((m, d), F32), jax.ShapeDtypeStruct((m, n), BF16)],
        scratch_shapes=[pltpu.VMEM((d, d), BF16), pltpu.VMEM((d, n), BF16)],
        compiler_params=pltpu.CompilerParams(
            dimension_semantics=("arbitrary",), vmem_limit_bytes=V7X_VMEM_LIMIT_BYTES),
        name="attn_out_mlp_up",
    )(a_t, w_out, r, g.reshape(1, d), w_ff1)


def _mlp_down_kernel(a_ref, w_ref, r_ref, g_ref, o_ref, w_b_ref, *, final_norm):
    @pl.when(pl.program_id(0) == 0)
    def _():
        _cast_columns(w_b_ref, w_ref, 0, w_ref.shape[1])

    h = r_ref[...] + jnp.dot(a_ref[...], w_b_ref[...], preferred_element_type=F32)
    if final_norm:
        h = _rmsnorm_rows(h, g_ref[...])
    o_ref[...] = h


def _mlp_down(a, w_ff2, r, g, layer, *, final_norm):
    m, k = a.shape
    n = w_ff2.shape[2]
    return pl.pallas_call(
        functools.partial(_mlp_down_kernel, final_norm=final_norm),
        grid=(m // ROW_TILE,),
        in_specs=[
            pl.BlockSpec((ROW_TILE, k), lambda i: (i, 0)),
            _layer_weight_spec(w_ff2, layer),
            pl.BlockSpec((ROW_TILE, n), lambda i: (i, 0)),
            pl.BlockSpec((1, n), lambda i: (0, 0)),
        ],
        out_specs=pl.BlockSpec((ROW_TILE, n), lambda i: (i, 0)),
        out_shape=jax.ShapeDtypeStruct((m, n), F32),
        scratch_shapes=[pltpu.VMEM((k, n), BF16)],
        compiler_params=pltpu.CompilerParams(
            dimension_semantics=("arbitrary",), vmem_limit_bytes=V7X_VMEM_LIMIT_BYTES),
        name="mlp_down",
    )(a, w_ff2, r, g.reshape(1, n))


SUM_ON_MXU = {False: False, True: True}
L_MIN = 2.0 ** -60
UB_MARGIN = 1.01
KM_PIECES = 3
NORM_UNROLL = 4


def _attention_kernel(slope_ref, qt_ref, ka_ref, vt_ref, lam_ref, subln_ref, o_ref,
                      qa_ref, ub_ref, l_ref, acc_ref, km_ref, kn_ref, p_ref, ps_ref, m_ref, kmp_ref, *, moba,
                      lam_init):
    group = pl.program_id(1)
    qi = pl.program_id(2)
    t = ATT_TILE
    n_tab = AUG_SLOPE_PIECES + 1
    n_blocks = ka_ref.shape[1] // MOBA_BLOCK
    n_chunks = ka_ref.shape[1] // ATT_CHUNK

    def head_of(g, s):
        pair = group * ATT_GROUP + g
        return 2 * pair + s if moba else pair

    @pl.when(qi == 0)
    def _():
        lane8 = lax.broadcasted_iota(jnp.int32, (8, PAIR), 1)
        row8 = lax.broadcasted_iota(jnp.int32, (8, PAIR), 0)
        selector = jnp.where(row8 == (lane8 >= HEAD_DIM).astype(jnp.int32), 1.0, 0.0).astype(BF16)
        first_lanes = lax.broadcasted_iota(jnp.int32, (ATT_CHUNK, PAIR), 1) < HEAD_DIM
        for g in range(ATT_GROUP):
            def chunk_norms(c4, best):
                for c in range(NORM_UNROLL):
                    r0 = pl.multiple_of((c4 * NORM_UNROLL + c) * ATT_CHUNK, ATT_CHUNK)
                    lanes = slice(g * PAIR, (g + 1) * PAIR)
                    k_pair = jnp.where(first_lanes, ka_ref[0, pl.ds(r0, ATT_CHUNK), lanes],
                                       ka_ref[1, pl.ds(r0, ATT_CHUNK), lanes])
                    sq = lax.dot_general(selector, k_pair * k_pair, _NT,
                                         preferred_element_type=F32)
                    best = jnp.maximum(best, sq)
                return best

            best = lax.fori_loop(0, n_chunks // NORM_UNROLL, chunk_norms,
                                 jnp.zeros((8, ATT_CHUNK), F32))
            k_norm = jnp.sqrt(jnp.max(best, axis=1, keepdims=True))
            for s in range(2):
                kn_ref[g, s] = jnp.broadcast_to(k_norm[s:s + 1], (8, PAIR))

    if moba:
        @pl.when(qi == 0)
        def _():
            lane_row = lax.broadcasted_iota(jnp.int32, (1, PAIR), 1)
            km_ref[...] = jnp.zeros_like(km_ref)
            for g in range(ATT_GROUP):
                for n in range(n_blocks):
                    rows = slice(n * MOBA_BLOCK, (n + 1) * MOBA_BLOCK)
                    lanes = slice(g * PAIR, (g + 1) * PAIR)
                    mean_a = jnp.mean(ka_ref[0, rows, lanes].astype(F32), axis=0, keepdims=True)
                    mean_b = jnp.mean(ka_ref[1, rows, lanes].astype(F32), axis=0, keepdims=True)
                    km_ref[g, n:n + 1, :] = jnp.where(lane_row < HEAD_DIM, mean_a, mean_b)
                rest = km_ref[g]
                for piece in range(KM_PIECES):
                    part = rest.astype(BF16)
                    kmp_ref[g, piece * AUG_BLOCKS:(piece + 1) * AUG_BLOCKS, :] = part
                    rest = rest - part.astype(F32)

    row16 = lax.broadcasted_iota(jnp.int32, (AUG_ROWS, t), 0)
    own_rows = lax.broadcasted_iota(jnp.int32, (PAIR, t), 0) < HEAD_DIM
    blk = lax.broadcasted_iota(jnp.int32, (AUG_BLOCKS, t), 0)
    blk_f = blk.astype(F32)
    q_idx = lax.broadcasted_iota(jnp.int32, (1, t), 1)
    q_pos = q_idx.astype(F32)
    own_blk = qi * (t // MOBA_BLOCK) + q_idx // MOBA_BLOCK
    for g in range(ATT_GROUP):
        qt = qt_ref[0, 0, g * PAIR:(g + 1) * PAIR, :].astype(F32)
        for s in range(2):
            h = head_of(g, s)
            pos_rows = jnp.zeros((AUG_ROWS, t), F32)
            for piece in range(AUG_SLOPE_PIECES):
                hit = jnp.logical_or(row16 == piece, row16 == piece + AUG_SLOPE_PIECES)
                pos_rows = jnp.where(hit, slope_ref[h * n_tab + piece], pos_rows)
            if moba:
                q_own = jnp.where(own_rows if s == 0 else jnp.logical_not(own_rows), qt, 0.0)
                gate_pieces = jnp.dot(kmp_ref[g], q_own.astype(BF16),
                                      preferred_element_type=F32)
                gate = gate_pieces[0:AUG_BLOCKS]
                for piece in range(1, KM_PIECES):
                    gate = gate + gate_pieces[piece * AUG_BLOCKS:(piece + 1) * AUG_BLOCKS]
                gt = jnp.where(blk < own_blk, gate, -jnp.inf)
                selected = blk == own_blk
                for pick in range(MOBA_TOPK):
                    mx = jnp.max(gt, axis=0, keepdims=True)
                    idx = jnp.min(jnp.where(gt == mx, blk_f, F32(AUG_BLOCKS)), axis=0,
                                  keepdims=True)
                    hit = jnp.logical_and(blk_f == idx, pick < own_blk)
                    selected = jnp.logical_or(selected, hit)
                    gt = jnp.where(blk_f == idx, -jnp.inf, gt)
                sel_rows = jnp.where(selected, 0.0, NEG_INF)
            else:
                sel_rows = jnp.zeros((AUG_BLOCKS, t), F32)
            own = qt[0:HEAD_DIM] if s == 0 else qt[HEAD_DIM:PAIR]
            q_norm = jnp.sqrt(jnp.sum(own * own, axis=0, keepdims=True))
            ub_ref[g, s] = (q_norm * kn_ref[g, s][0:1, 0:1] * UB_MARGIN
                            + slope_ref[h * n_tab + AUG_SLOPE_PIECES] * q_pos)
            tail = jnp.zeros((HEAD_DIM - AUG_BLOCKS - AUG_ROWS, t), F32)
            parts = [sel_rows, pos_rows, tail]
            parts = [own] + parts if s == 0 else parts + [own]
            qa_ref[g, s] = jnp.concatenate(parts, axis=0).astype(BF16)

    chains = [(g, s) for g in range(ATT_GROUP) for s in range(2)]
    first_own = qi * Q_CHUNKS
    half = ATT_CHUNK

    def chunk_offset(g, s, c):
        return (slope_ref[head_of(g, s) * n_tab + AUG_SLOPE_PIECES]
                * ((c - first_own) * ATT_CHUNK).astype(F32))

    v_rows = HEAD_DIM if moba else PAIR
    ones_rows = jnp.ones((AUG_ROWS, ATT_CHUNK), BF16)

    def key_rows(g, s, c):
        r0 = pl.multiple_of(c * ATT_CHUNK, ATT_CHUNK)
        return ka_ref[s, pl.ds(r0, ATT_CHUNK), g * PAIR:(g + 1) * PAIR]

    def value_rows(g, s, c):
        v0 = g * PAIR + (s * HEAD_DIM if moba else 0)
        return jnp.concatenate([vt_ref[0, c, v0:v0 + v_rows, :], ones_rows], axis=0)

    def probabilities(g, s, c, slot):
        st = jnp.dot(key_rows(g, s, c), qa_ref[g, s], preferred_element_type=F32)
        p = jnp.exp2(st - (ub_ref[g, s] - chunk_offset(g, s, c)))
        p_ref[slot, g, s] = p.astype(BF16)
        if not SUM_ON_MXU[moba]:
            l_ref[g, s] += jnp.sum(p, axis=0, keepdims=True)

    def values(g, s, c, slot, q0=0):
        if SUM_ON_MXU[moba]:
            pv = jnp.dot(value_rows(g, s, c), p_ref[slot, g, s, :, q0:t],
                         preferred_element_type=F32)
            l_ref[g, s, :, q0:t] += pv[v_rows:v_rows + 1]
            acc_ref[g, s, :, q0:t] += pv[0:v_rows]
        else:
            v0 = g * PAIR + (s * HEAD_DIM if moba else 0)
            acc_ref[g, s, :, q0:t] += jnp.dot(vt_ref[0, c, v0:v0 + v_rows, :],
                                              p_ref[slot, g, s, :, q0:t],
                                              preferred_element_type=F32)

    row_i = lax.broadcasted_iota(jnp.int32, (ATT_CHUNK, t), 0)
    col_i = lax.broadcasted_iota(jnp.int32, (ATT_CHUNK, t), 1)
    mask_first = row_i <= col_i
    mask_second = row_i + half <= col_i

    def own_probabilities(g, s, which):
        c = first_own + which
        slot = 1 - which
        q0 = which * half
        r0 = pl.multiple_of(c * ATT_CHUNK, ATT_CHUNK)
        lanes = slice(g * PAIR, (g + 1) * PAIR)
        mask = mask_second if which else mask_first
        off = chunk_offset(g, s, c)

        def strip(keys, a, b, masked):
            st = jnp.dot(ka_ref[s, pl.ds(r0, keys), lanes], qa_ref[g, s, :, a:b],
                         preferred_element_type=F32)
            if masked:
                st = jnp.where(mask[0:keys, a:b], st, NEG_INF)
            p = jnp.exp2(st - (ub_ref[g, s, :, a:b] - off))
            p_ref[slot, g, s, 0:keys, a:b] = p.astype(BF16)
            if keys < ATT_CHUNK:
                p_ref[slot, g, s, keys:ATT_CHUNK, a:b] = jnp.zeros((ATT_CHUNK - keys, b - a), BF16)
            if not SUM_ON_MXU[moba]:
                l_ref[g, s, :, a:b] += jnp.sum(p, axis=0, keepdims=True)

        strip(KEY_TILE, q0, q0 + KEY_TILE, True)
        strip(ATT_CHUNK, q0 + KEY_TILE, q0 + half, True)
        if q0 + half < t:
            strip(ATT_CHUNK, q0 + half, t, False)

    def write_output():
        for g in range(ATT_GROUP):
            out_a = acc_ref[g, 0] / l_ref[g, 0]
            out_b = acc_ref[g, 1] / l_ref[g, 1]
            if moba:
                o = jnp.concatenate([out_a, out_b], axis=0)
            else:
                lp = lam_ref[...]
                lam = (jnp.exp(jnp.sum(lp[0:1] * lp[1:2], axis=-1, keepdims=True))
                       - jnp.exp(jnp.sum(lp[2:3] * lp[3:4], axis=-1, keepdims=True)) + lam_init)
                o = out_a - lam * out_b
                rms = lax.rsqrt(jnp.mean(o * o, axis=0, keepdims=True) + RMS_EPS)
                o = o * rms * pltpu.repeat(subln_ref[...], t // PAIR, axis=1) * (1.0 - lam_init)
            o_ref[0, 0, g * PAIR:(g + 1) * PAIR, :] = o.astype(o_ref.dtype)

    l_ref[...] = jnp.zeros_like(l_ref)
    acc_ref[...] = jnp.zeros_like(acc_ref)
    for g, s in chains:
        own_probabilities(g, s, 1)
    for g, s in chains:
        own_probabilities(g, s, 0)
        values(g, s, first_own + 1, 0, q0=half)

    def two_steps(j, carry):
        c_odd = jnp.where(j == 0, first_own, 2 * j - 1)
        for g, s in chains:
            probabilities(g, s, 2 * j, 0)
            values(g, s, c_odd, 1)
        for g, s in chains:
            probabilities(g, s, 2 * j + 1, 1)
            values(g, s, 2 * j, 0)
        return carry

    lax.fori_loop(0, qi, two_steps, 0)
    c_last = jnp.where(qi == 0, first_own, first_own - 1)
    for g, s in chains:
        values(g, s, c_last, 1)
    write_output()

    def safe_path():
        m_ref[...] = jnp.full_like(m_ref, NEG_INF)
        l_ref[...] = jnp.zeros_like(l_ref)
        acc_ref[...] = jnp.zeros_like(acc_ref)

        def fold(g, s, c, mask):
            st = jnp.dot(key_rows(g, s, c), qa_ref[g, s], preferred_element_type=F32)
            if mask is not None:
                st = jnp.where(mask, st, NEG_INF)
            off = chunk_offset(g, s, c)
            m_prev = m_ref[g, s]
            m_next = jnp.maximum(m_prev, jnp.max(st, axis=0, keepdims=True) + off)
            alpha = jnp.exp2(m_prev - m_next)
            ps_ref[...] = jnp.exp2(st - (m_next - off)).astype(BF16)
            pv = jnp.dot(value_rows(g, s, c), ps_ref[...], preferred_element_type=F32)
            m_ref[g, s] = m_next
            l_ref[g, s] = alpha * l_ref[g, s] + pv[v_rows:v_rows + 1]
            acc_ref[g, s] = alpha * acc_ref[g, s] + pv[0:v_rows]

        for g, s in chains:
            fold(g, s, first_own, mask_first)
        for g, s in chains:
            fold(g, s, first_own + 1, mask_second)

        def past(c, carry):
            for g, s in chains:
                fold(g, s, c, None)
            return carry

        lax.fori_loop(0, first_own, past, 0)

    l_min = l_ref[0, 0]
    for g, s in chains[1:]:
        l_min = jnp.minimum(l_min, l_ref[g, s])

    @pl.when(jnp.logical_not(jnp.min(l_min) >= L_MIN))
    def _():
        safe_path()
        write_output()


def _attention(qt, ka, vt, slope_table, lam_params, subln_w, *, moba, lam_init):
    b, nq, d, _ = qt.shape
    t = nq * ATT_TILE
    n_blocks = t // MOBA_BLOCK
    assert ATT_TILE == Q_CHUNKS * ATT_CHUNK and Q_CHUNKS == 2 and n_blocks <= AUG_BLOCKS
    assert (t // ATT_CHUNK) % NORM_UNROLL == 0
    assert AUG_POS + 2 * AUG_SLOPE_PIECES <= AUG_BLOCKS + AUG_ROWS <= HEAD_DIM
    gl = ATT_GROUP * PAIR
    return pl.pallas_call(
        functools.partial(_attention_kernel, moba=moba, lam_init=lam_init),
        grid=(b, N_PAIRS // ATT_GROUP, nq),
        in_specs=[
            pl.BlockSpec(memory_space=pltpu.SMEM),
            pl.BlockSpec((1, 1, gl, ATT_TILE), lambda bi, p, i: (bi, i, p, 0)),
            pl.BlockSpec((2, None, t, gl), lambda bi, p, i: (0, bi, 0, p)),
            pl.BlockSpec((1, t // ATT_CHUNK, gl, ATT_CHUNK), lambda bi, p, i: (bi, 0, p, 0)),
            pl.BlockSpec((4, HEAD_DIM), lambda bi, p, i: (0, 0)),
            pl.BlockSpec((PAIR, PAIR), lambda bi, p, i: (0, 0)),
        ],
        out_specs=pl.BlockSpec((1, 1, gl, ATT_TILE), lambda bi, p, i: (bi, i, p, 0)),
        out_shape=jax.ShapeDtypeStruct((b, nq, d, ATT_TILE), BF16),
        scratch_shapes=[
            pltpu.VMEM((ATT_GROUP, 2, PAIR, ATT_TILE), BF16),
            pltpu.VMEM((ATT_GROUP, 2, 1, ATT_TILE), F32),
            pltpu.VMEM((ATT_GROUP, 2, 1, ATT_TILE), F32),
            pltpu.VMEM((ATT_GROUP, 2, HEAD_DIM if moba else PAIR, ATT_TILE), F32),
            pltpu.VMEM((ATT_GROUP, AUG_BLOCKS, PAIR), F32),
            pltpu.VMEM((ATT_GROUP, 2, 8, PAIR), F32),
            pltpu.VMEM((2, ATT_GROUP, 2, ATT_CHUNK, ATT_TILE), BF16),
            pltpu.VMEM((ATT_CHUNK, ATT_TILE), BF16),
            pltpu.VMEM((ATT_GROUP, 2, 1, ATT_TILE), F32),
            pltpu.VMEM((ATT_GROUP, KM_PIECES * AUG_BLOCKS, PAIR), BF16),
        ],
        compiler_params=pltpu.CompilerParams(
            dimension_semantics=("arbitrary", "arbitrary", "arbitrary"),
            vmem_limit_bytes=V7X_VMEM_LIMIT_BYTES),
        name="moba_attention" if moba else "diff_attention",
    )(slope_table, qt, ka.reshape(2, b, t, d), vt, lam_params,
      jnp.broadcast_to(subln_w[:, None], (PAIR, PAIR)))


def kernel(x, attn_norm, w_in, w_out, diff_lambda, diff_subln, mlp_norm, w_ff1, w_ff2, final_norm):
    b, t, d = x.shape
    h = x.reshape(b * t, d)
    q_scale = HEAD_DIM ** -0.5 * LOG2E
    for i in range(DEPTH):
        moba = i % N_MIXERS == 1
        li = i // N_MIXERS
        qt, ka, vt = _qkv_projection(h, attn_norm[i], w_in, i, batch=b, q_scale=q_scale)
        slope_table = jnp.asarray(_alibi_slope_pieces(MOBA_HEADS if moba else DIFF_HEADS))
        mix = _attention(qt, ka, vt, slope_table, diff_lambda[li], diff_subln[li],
                         moba=moba, lam_init=_lambda_init(i))
        h, u = _attn_out_mlp_up(mix, w_out, h, mlp_norm[i], w_ff1, i)
        h = _mlp_down(u, w_ff2, h, final_norm, i, final_norm=(i == DEPTH - 1))
    return h.reshape(b, t, d)
```

```python
import functools
import math

import jax
import jax.numpy as jnp
import numpy as np
from jax import lax
from jax.experimental import pallas as pl
from jax.experimental.pallas import tpu as pltpu

D_MODEL = 1024
DEPTH = 2
N_MIXERS = 2
DIFF_HEADS = 8
MOBA_HEADS = 16
HEAD_DIM = 64
PAIR = 2 * HEAD_DIM
N_PAIRS = D_MODEL // PAIR
MOBA_BLOCK = 256
MOBA_TOPK = 3
D_FF = 4 * D_MODEL
RMS_EPS = 1e-6
NEG_INF = -1e30
LOG2E = math.log2(math.e)

V7X_VMEM_LIMIT_BYTES = 56 * 1024 * 1024

ROW_TILE = 512
COL_TILE = 1024
COL_CHUNK = 256
KEY_TILE = MOBA_BLOCK
ATT_CHUNK_TILES = 2
ATT_CHUNK = ATT_CHUNK_TILES * KEY_TILE
Q_CHUNKS = 2
ATT_TILE = Q_CHUNKS * ATT_CHUNK
ATT_GROUP = 2

AUG_BASE = (HEAD_DIM, 0)
AUG_BLOCKS = 32
AUG_POS = AUG_BLOCKS
AUG_SLOPE_PIECES = 3
AUG_ROWS = 16

BF16 = jnp.bfloat16
F32 = jnp.float32


def _alibi_slope_pieces(n_heads):
    slopes = np.array([2.0 ** (-8.0 * (h + 1) / n_heads) for h in range(n_heads)], np.float64)
    rest = slopes * LOG2E
    pieces = []
    for _ in range(AUG_SLOPE_PIECES):
        p = rest.astype(np.float32).astype(BF16).astype(np.float64)
        pieces.append(p)
        rest = rest - p
    pieces = np.stack(pieces, axis=1)
    table = np.concatenate([pieces, pieces.sum(axis=1, keepdims=True)], axis=1)
    return table.astype(np.float32).reshape(-1)


def _lambda_init(layer_idx):
    return 0.8 - 0.6 * math.exp(-0.3 * layer_idx)


def _rmsnorm_rows(x, g):
    return x * lax.rsqrt(jnp.mean(x * x, axis=-1, keepdims=True) + RMS_EPS) * g


_NT = (((1,), (1,)), ((), ()))


def _layer_weight_spec(w, layer):
    return pl.BlockSpec((None,) + w.shape[1:], lambda i: (layer, 0, 0),
                        pipeline_mode=pl.Buffered(1))


def _cast_columns(dst_ref, src_ref, c0, c1, *, transpose=False):
    for j in range(c0, c1, COL_CHUNK):
        blk = src_ref[:, j:j + COL_CHUNK]
        if transpose:
            dst_ref[j - c0:j - c0 + COL_CHUNK, :] = blk.T.astype(BF16)
        else:
            dst_ref[:, j - c0:j - c0 + COL_CHUNK] = blk.astype(BF16)


def _qkv_kernel(x_ref, g_ref, w_ref, qt_ref, ka_ref, vt_ref, km_ref, wqt_ref, wk_ref, wvt_ref, *,
                q_scale, steps_per_seq):
    @pl.when(pl.program_id(0) == 0)
    def _():
        _cast_columns(wqt_ref, w_ref, 0, D_MODEL, transpose=True)
        _cast_columns(wk_ref, w_ref, D_MODEL, 2 * D_MODEL)
        _cast_columns(wvt_ref, w_ref, 2 * D_MODEL, 3 * D_MODEL, transpose=True)

    lane = lax.broadcasted_iota(jnp.int32, (KEY_TILE, PAIR), 1)
    pos = lax.broadcasted_iota(jnp.int32, (KEY_TILE, PAIR), 0).astype(F32)
    for part in range(ROW_TILE // KEY_TILE):
        rows = slice(part * KEY_TILE, (part + 1) * KEY_TILE)
        xn = _rmsnorm_rows(x_ref[rows], g_ref[...]).astype(BF16)
        for c0 in range(0, D_MODEL, 256):
            vt = lax.dot_general(wvt_ref[c0:c0 + 256, :], xn, _NT, preferred_element_type=F32)
            vt_ref[0, 0, c0:c0 + 256, rows] = vt.astype(BF16)
            qt = lax.dot_general(wqt_ref[c0:c0 + 256, :], xn, _NT, preferred_element_type=F32)
            qt_ref[0, 0, c0:c0 + 256, rows] = (qt * q_scale).astype(BF16)
        k = jnp.dot(xn, wk_ref[...], preferred_element_type=F32).astype(BF16)
        tile = (pl.program_id(0) % steps_per_seq) * (ROW_TILE // KEY_TILE) + part
        km_ref[0, pl.ds(tile, 1), :] = jnp.mean(k.astype(F32), axis=0, keepdims=True)
        tile_pos = float((part % ATT_CHUNK_TILES) * KEY_TILE)
        for s in range(2):
            rel = lane - AUG_BASE[s]
            aug = jnp.where(rel == tile, 1.0, 0.0)
            aug = jnp.where(jnp.logical_and(rel >= AUG_POS, rel < AUG_POS + AUG_SLOPE_PIECES), pos, aug)
            aug = jnp.where(jnp.logical_and(rel >= AUG_POS + AUG_SLOPE_PIECES,
                                            rel < AUG_POS + 2 * AUG_SLOPE_PIECES), tile_pos, aug)
            aug = aug.astype(BF16)
            own = jnp.logical_and(rel >= -HEAD_DIM, rel < 0) if s == 0 else rel >= HEAD_DIM
            for c0 in range(0, D_MODEL, PAIR):
                ka_ref[s, rows, c0:c0 + PAIR] = jnp.where(own, k[:, c0:c0 + PAIR], aug)


def _qkv_projection(x, g, w_in, layer, *, batch, q_scale):
    m, d = x.shape
    t = m // batch
    spb = t // ROW_TILE
    spc = ATT_CHUNK // ROW_TILE
    spq = ATT_TILE // ROW_TILE
    assert t // KEY_TILE <= AUG_BLOCKS and ATT_CHUNK % ROW_TILE == 0 and t % ATT_CHUNK == 0
    assert ATT_TILE % ROW_TILE == 0 and ROW_TILE % ATT_CHUNK == 0 and t % ATT_TILE == 0
    return pl.pallas_call(
        functools.partial(_qkv_kernel, q_scale=q_scale, steps_per_seq=spb),
        grid=(m // ROW_TILE,),
        in_specs=[
            pl.BlockSpec((ROW_TILE, d), lambda i: (i, 0)),
            pl.BlockSpec((1, d), lambda i: (0, 0)),
            _layer_weight_spec(w_in, layer),
        ],
        out_specs=[
            pl.BlockSpec((1, 1, d, ROW_TILE), lambda i: (i // spb, (i % spb) // spq, 0, i % spq)),
            pl.BlockSpec((2, ROW_TILE, d), lambda i: (0, i, 0)),
            pl.BlockSpec((1, 1, d, ROW_TILE), lambda i: (i // spb, (i % spb) // spc, 0, i % spc)),
            pl.BlockSpec((1, t // KEY_TILE, d), lambda i: (i // spb, 0, 0)),
        ],
        out_shape=[
            jax.ShapeDtypeStruct((batch, t // ATT_TILE, d, ATT_TILE), BF16),
            jax.ShapeDtypeStruct((2, m, d), BF16),
            jax.ShapeDtypeStruct((batch, t // ATT_CHUNK, d, ATT_CHUNK), BF16),
            jax.ShapeDtypeStruct((batch, t // KEY_TILE, d), F32),
        ],
        scratch_shapes=[pltpu.VMEM((d, d), BF16)] * 3,
        compiler_params=pltpu.CompilerParams(
            dimension_semantics=("arbitrary",), vmem_limit_bytes=V7X_VMEM_LIMIT_BYTES),
        name="qkv_projection",
    )(x, g.reshape(1, d), w_in)


def _attn_out_mlp_up_kernel(a_ref, wo_ref, r_ref, g_ref, w1_ref, h_ref, u_ref, wo_b_ref, w1_b_ref):
    @pl.when(pl.program_id(0) == 0)
    def _():
        _cast_columns(wo_b_ref, wo_ref, 0, wo_ref.shape[1])
        _cast_columns(w1_b_ref, w1_ref, 0, w1_ref.shape[1])

    h = r_ref[...] + lax.dot_general(a_ref[...], wo_b_ref[...], (((0,), (0,)), ((), ())),
                                     preferred_element_type=F32)
    h_ref[...] = h
    xn = _rmsnorm_rows(h, g_ref[...]).astype(BF16)
    for c0 in range(0, w1_b_ref.shape[1], COL_TILE):
        acc = jnp.dot(xn, w1_b_ref[:, c0:c0 + COL_TILE], preferred_element_type=F32)
        u_ref[:, c0:c0 + COL_TILE] = jnp.square(jnp.maximum(acc, 0.0)).astype(u_ref.dtype)


def _attn_out_mlp_up(a_t, w_out, r, g, w_ff1, layer):
    m, d = r.shape
    n = w_ff1.shape[2]
    spq = a_t.shape[3] // ROW_TILE
    spb = a_t.shape[1] * spq
    assert a_t.shape[2] == d and a_t.shape[3] % ROW_TILE == 0
    rows = lambda width: pl.BlockSpec((ROW_TILE, width), lambda i: (i, 0))
    return pl.pallas_call(
        _attn_out_mlp_up_kernel,
        grid=(m // ROW_TILE,),
        in_specs=[pl.BlockSpec((None, None, d, ROW_TILE),
                               lambda i: (i // spb, (i % spb) // spq, 0, i % spq)),
                  _layer_weight_spec(w_out, layer), rows(d),
                  pl.BlockSpec((1, d), lambda i: (0, 0)), _layer_weight_spec(w_ff1, layer)],
        out_specs=[rows(d), rows(n)],
        out_shape=[jax.ShapeDtypeStruct((m, d), F32), jax.ShapeDtypeStruct((m, n), BF16)],
        scratch_shapes=[pltpu.VMEM((d, d), BF16), pltpu.VMEM((d, n), BF16)],
        compiler_params=pltpu.CompilerParams(
            dimension_semantics=("arbitrary",), vmem_limit_bytes=V7X_VMEM_LIMIT_BYTES),
        name="attn_out_mlp_up",
    )(a_t, w_out, r, g.reshape(1, d), w_ff1)


def _mlp_down_kernel(a_ref, w_ref, r_ref, g_ref, o_ref, w_b_ref, *, final_norm):
    @pl.when(pl.program_id(0) == 0)
    def _():
        _cast_columns(w_b_ref, w_ref, 0, w_ref.shape[1])

    h = r_ref[...] + jnp.dot(a_ref[...], w_b_ref[...], preferred_element_type=F32)
    if final_norm:
        h = _rmsnorm_rows(h, g_ref[...])
    o_ref[...] = h


def _mlp_down(a, w_ff2, r, g, layer, *, final_norm):
    m, k = a.shape
    n = w_ff2.shape[2]
    return pl.pallas_call(
        functools.partial(_mlp_down_kernel, final_norm=final_norm),
        grid=(m // ROW_TILE,),
        in_specs=[
            pl.BlockSpec((ROW_TILE, k), lambda i: (i, 0)),
            _layer_weight_spec(w_ff2, layer),
            pl.BlockSpec((ROW_TILE, n), lambda i: (i, 0)),
            pl.BlockSpec((1, n), lambda i: (0, 0)),
        ],
        out_specs=pl.BlockSpec((ROW_TILE, n), lambda i: (i, 0)),
        out_shape=jax.ShapeDtypeStruct((m, n), F32),
        scratch_shapes=[pltpu.VMEM((k, n), BF16)],
        compiler_params=pltpu.CompilerParams(
            dimension_semantics=("arbitrary",), vmem_limit_bytes=V7X_VMEM_LIMIT_BYTES),
        name="mlp_down",
    )(a, w_ff2, r, g.reshape(1, n))


SUM_ON_MXU = {False: False, True: True}
L_MIN = 2.0 ** -60
UB_MARGIN = 1.01
KM_PIECES = 3
NORM_UNROLL = 16


def _attention_kernel(slope_ref, qt_ref, ka_ref, vt_ref, km_ref, lam_ref, subln_ref, o_ref,
                      qa_ref, ub_ref, l_ref, acc_ref, kn_ref, p_ref, ps_ref, m_ref, kmp_ref, *, moba,
                      lam_init):
    group = pl.program_id(1)
    qi = pl.program_id(2)
    t = ATT_TILE
    n_tab = AUG_SLOPE_PIECES + 1
    n_blocks = ka_ref.shape[1] // MOBA_BLOCK
    n_chunks = ka_ref.shape[1] // ATT_CHUNK

    def head_of(g, s):
        pair = group * ATT_GROUP + g
        return 2 * pair + s if moba else pair

    @pl.when(qi == 0)
    def _():
        lane8 = lax.broadcasted_iota(jnp.int32, (8, PAIR), 1)
        row8 = lax.broadcasted_iota(jnp.int32, (8, PAIR), 0)
        selector = jnp.where(row8 == (lane8 >= HEAD_DIM).astype(jnp.int32), 1.0, 0.0).astype(BF16)
        first_lanes = lax.broadcasted_iota(jnp.int32, (ATT_CHUNK, PAIR), 1) < HEAD_DIM
        unroll = math.gcd(n_chunks, NORM_UNROLL)
        for g in range(ATT_GROUP):
            def chunk_norms(trip, best):
                for c in range(unroll):
                    r0 = pl.multiple_of((trip * unroll + c) * ATT_CHUNK, ATT_CHUNK)
                    lanes = slice(g * PAIR, (g + 1) * PAIR)
                    k_pair = jnp.where(first_lanes, ka_ref[0, pl.ds(r0, ATT_CHUNK), lanes],
                                       ka_ref[1, pl.ds(r0, ATT_CHUNK), lanes])
                    sq = lax.dot_general(selector, k_pair * k_pair, _NT,
                                         preferred_element_type=F32)
                    best = jnp.maximum(best, sq)
                return best

            best = lax.fori_loop(0, n_chunks // unroll, chunk_norms,
                                 jnp.zeros((8, ATT_CHUNK), F32))
            k_norm = jnp.sqrt(jnp.max(best, axis=1, keepdims=True))
            for s in range(2):
                kn_ref[g, s] = jnp.broadcast_to(k_norm[s:s + 1], (8, PAIR))

    if moba:
        @pl.when(qi == 0)
        def _():
            if n_blocks < AUG_BLOCKS:
                kmp_ref[...] = jnp.zeros_like(kmp_ref)
            for g in range(ATT_GROUP):
                rest = km_ref[0, :, g * PAIR:(g + 1) * PAIR]
                for piece in range(KM_PIECES):
                    part = rest.astype(BF16)
                    kmp_ref[g, piece * AUG_BLOCKS:piece * AUG_BLOCKS + n_blocks, :] = part
                    rest = rest - part.astype(F32)

    row16 = lax.broadcasted_iota(jnp.int32, (AUG_ROWS, t), 0)
    own_rows = lax.broadcasted_iota(jnp.int32, (PAIR, t), 0) < HEAD_DIM
    blk = lax.broadcasted_iota(jnp.int32, (AUG_BLOCKS, t), 0)
    blk_f = blk.astype(F32)
    q_idx = lax.broadcasted_iota(jnp.int32, (1, t), 1)
    q_pos = q_idx.astype(F32)
    own_blk = qi * (t // MOBA_BLOCK) + q_idx // MOBA_BLOCK
    for g in range(ATT_GROUP):
        qt = qt_ref[0, 0, g * PAIR:(g + 1) * PAIR, :].astype(F32)
        for s in range(2):
            h = head_of(g, s)
            pos_rows = jnp.zeros((AUG_ROWS, t), F32)
            for piece in range(AUG_SLOPE_PIECES):
                hit = jnp.logical_or(row16 == piece, row16 == piece + AUG_SLOPE_PIECES)
                pos_rows = jnp.where(hit, slope_ref[h * n_tab + piece], pos_rows)
            if moba:
                q_own = jnp.where(own_rows if s == 0 else jnp.logical_not(own_rows), qt, 0.0)
                gate_pieces = jnp.dot(kmp_ref[g], q_own.astype(BF16),
                                      preferred_element_type=F32)
                gate = gate_pieces[0:AUG_BLOCKS]
                for piece in range(1, KM_PIECES):
                    gate = gate + gate_pieces[piece * AUG_BLOCKS:(piece + 1) * AUG_BLOCKS]
                gt = jnp.where(blk < own_blk, gate, -jnp.inf)
                selected = blk == own_blk
                for pick in range(MOBA_TOPK):
                    mx = jnp.max(gt, axis=0, keepdims=True)
                    idx = jnp.min(jnp.where(gt == mx, blk_f, F32(AUG_BLOCKS)), axis=0,
                                  keepdims=True)
                    hit = jnp.logical_and(blk_f == idx, pick < own_blk)
                    selected = jnp.logical_or(selected, hit)
                    gt = jnp.where(blk_f == idx, -jnp.inf, gt)
                sel_rows = jnp.where(selected, 0.0, NEG_INF)
            else:
                sel_rows = jnp.zeros((AUG_BLOCKS, t), F32)
            own = qt[0:HEAD_DIM] if s == 0 else qt[HEAD_DIM:PAIR]
            q_norm = jnp.sqrt(jnp.sum(own * own, axis=0, keepdims=True))
            ub_ref[g, s] = (q_norm * kn_ref[g, s][0:1, 0:1] * UB_MARGIN
                            + slope_ref[h * n_tab + AUG_SLOPE_PIECES] * q_pos)
            tail = jnp.zeros((HEAD_DIM - AUG_BLOCKS - AUG_ROWS, t), F32)
            parts = [sel_rows, pos_rows, tail]
            parts = [own] + parts if s == 0 else parts + [own]
            qa_ref[g, s] = jnp.concatenate(parts, axis=0).astype(BF16)

    chains = [(g, s) for g in range(ATT_GROUP) for s in range(2)]
    first_own = qi * Q_CHUNKS
    half = ATT_CHUNK

    def chunk_offset(g, s, c):
        return (slope_ref[head_of(g, s) * n_tab + AUG_SLOPE_PIECES]
                * ((c - first_own) * ATT_CHUNK).astype(F32))

    v_rows = HEAD_DIM if moba else PAIR
    ones_rows = jnp.ones((AUG_ROWS, ATT_CHUNK), BF16)

    def key_rows(g, s, c):
        r0 = pl.multiple_of(c * ATT_CHUNK, ATT_CHUNK)
        return ka_ref[s, pl.ds(r0, ATT_CHUNK), g * PAIR:(g + 1) * PAIR]

    def value_rows(g, s, c):
        v0 = g * PAIR + (s * HEAD_DIM if moba else 0)
        return jnp.concatenate([vt_ref[0, c, v0:v0 + v_rows, :], ones_rows], axis=0)

    def probabilities(g, s, c, slot):
        st = jnp.dot(key_rows(g, s, c), qa_ref[g, s], preferred_element_type=F32)
        p = jnp.exp2(st - (ub_ref[g, s] - chunk_offset(g, s, c)))
        p_ref[slot, g, s] = p.astype(BF16)
        if not SUM_ON_MXU[moba]:
            l_ref[g, s] += jnp.sum(p, axis=0, keepdims=True)

    def values(g, s, c, slot, q0=0):
        if SUM_ON_MXU[moba]:
            pv = jnp.dot(value_rows(g, s, c), p_ref[slot, g, s, :, q0:t],
                         preferred_element_type=F32)
            l_ref[g, s, :, q0:t] += pv[v_rows:v_rows + 1]
            acc_ref[g, s, :, q0:t] += pv[0:v_rows]
        else:
            v0 = g * PAIR + (s * HEAD_DIM if moba else 0)
            acc_ref[g, s, :, q0:t] += jnp.dot(vt_ref[0, c, v0:v0 + v_rows, :],
                                              p_ref[slot, g, s, :, q0:t],
                                              preferred_element_type=F32)

    row_i = lax.broadcasted_iota(jnp.int32, (ATT_CHUNK, t), 0)
    col_i = lax.broadcasted_iota(jnp.int32, (ATT_CHUNK, t), 1)
    mask_first = row_i <= col_i
    mask_second = row_i + half <= col_i

    def own_probabilities(g, s, which):
        c = first_own + which
        slot = 1 - which
        q0 = which * half
        r0 = pl.multiple_of(c * ATT_CHUNK, ATT_CHUNK)
        lanes = slice(g * PAIR, (g + 1) * PAIR)
        mask = mask_second if which else mask_first
        off = chunk_offset(g, s, c)

        def strip(keys, a, b, masked):
            st = jnp.dot(ka_ref[s, pl.ds(r0, keys), lanes], qa_ref[g, s, :, a:b],
                         preferred_element_type=F32)
            if masked:
                st = jnp.where(mask[0:keys, a:b], st, NEG_INF)
            p = jnp.exp2(st - (ub_ref[g, s, :, a:b] - off))
            p_ref[slot, g, s, 0:keys, a:b] = p.astype(BF16)
            if keys < ATT_CHUNK:
                p_ref[slot, g, s, keys:ATT_CHUNK, a:b] = jnp.zeros((ATT_CHUNK - keys, b - a), BF16)
            if not SUM_ON_MXU[moba]:
                l_ref[g, s, :, a:b] += jnp.sum(p, axis=0, keepdims=True)

        strip(KEY_TILE, q0, q0 + KEY_TILE, True)
        strip(ATT_CHUNK, q0 + KEY_TILE, q0 + half, True)
        if q0 + half < t:
            strip(ATT_CHUNK, q0 + half, t, False)

    def write_output():
        for g in range(ATT_GROUP):
            out_a = acc_ref[g, 0] / l_ref[g, 0]
            out_b = acc_ref[g, 1] / l_ref[g, 1]
            if moba:
                o = jnp.concatenate([out_a, out_b], axis=0)
            else:
                lp = lam_ref[...]
                lam = (jnp.exp(jnp.sum(lp[0:1] * lp[1:2], axis=-1, keepdims=True))
                       - jnp.exp(jnp.sum(lp[2:3] * lp[3:4], axis=-1, keepdims=True)) + lam_init)
                o = out_a - lam * out_b
                rms = lax.rsqrt(jnp.mean(o * o, axis=0, keepdims=True) + RMS_EPS)
                o = o * rms * pltpu.repeat(subln_ref[...], t // PAIR, axis=1) * (1.0 - lam_init)
            o_ref[0, 0, g * PAIR:(g + 1) * PAIR, :] = o.astype(o_ref.dtype)

    l_ref[...] = jnp.zeros_like(l_ref)
    acc_ref[...] = jnp.zeros_like(acc_ref)
    for g, s in chains:
        own_probabilities(g, s, 1)
    for g, s in chains:
        own_probabilities(g, s, 0)
        values(g, s, first_own + 1, 0, q0=half)

    def two_steps(j, carry):
        c_odd = jnp.where(j == 0, first_own, 2 * j - 1)
        for g, s in chains:
            probabilities(g, s, 2 * j, 0)
            values(g, s, c_odd, 1)
        for g, s in chains:
            probabilities(g, s, 2 * j + 1, 1)
            values(g, s, 2 * j, 0)
        return carry

    lax.fori_loop(0, qi, two_steps, 0)
    c_last = jnp.where(qi == 0, first_own, first_own - 1)
    for g, s in chains:
        values(g, s, c_last, 1)
    write_output()

    def safe_path():
        m_ref[...] = jnp.full_like(m_ref, NEG_INF)
        l_ref[...] = jnp.zeros_like(l_ref)
        acc_ref[...] = jnp.zeros_like(acc_ref)

        def fold(g, s, c, mask):
            st = jnp.dot(key_rows(g, s, c), qa_ref[g, s], preferred_element_type=F32)
            if mask is not None:
                st = jnp.where(mask, st, NEG_INF)
            off = chunk_offset(g, s, c)
            m_prev = m_ref[g, s]
            m_next = jnp.maximum(m_prev, jnp.max(st, axis=0, keepdims=True) + off)
            alpha = jnp.exp2(m_prev - m_next)
            ps_ref[...] = jnp.exp2(st - (m_next - off)).astype(BF16)
            pv = jnp.dot(value_rows(g, s, c), ps_ref[...], preferred_element_type=F32)
            m_ref[g, s] = m_next
            l_ref[g, s] = alpha * l_ref[g, s] + pv[v_rows:v_rows + 1]
            acc_ref[g, s] = alpha * acc_ref[g, s] + pv[0:v_rows]

        for g, s in chains:
            fold(g, s, first_own, mask_first)
        for g, s in chains:
            fold(g, s, first_own + 1, mask_second)

        def past(c, carry):
            for g, s in chains:
                fold(g, s, c, None)
            return carry

        lax.fori_loop(0, first_own, past, 0)

    l_min = l_ref[0, 0]
    for g, s in chains[1:]:
        l_min = jnp.minimum(l_min, l_ref[g, s])

    @pl.when(jnp.logical_not(jnp.min(l_min) >= L_MIN))
    def _():
        safe_path()
        write_output()


def _attention(qt, ka, vt, km, slope_table, lam_params, subln_w, *, moba, lam_init):
    b, nq, d, _ = qt.shape
    t = nq * ATT_TILE
    n_blocks = t // MOBA_BLOCK
    assert ATT_TILE == Q_CHUNKS * ATT_CHUNK and Q_CHUNKS == 2 and n_blocks <= AUG_BLOCKS
    assert AUG_POS + 2 * AUG_SLOPE_PIECES <= AUG_BLOCKS + AUG_ROWS <= HEAD_DIM
    gl = ATT_GROUP * PAIR
    return pl.pallas_call(
        functools.partial(_attention_kernel, moba=moba, lam_init=lam_init),
        grid=(b, N_PAIRS // ATT_GROUP, nq),
        in_specs=[
            pl.BlockSpec(memory_space=pltpu.SMEM),
            pl.BlockSpec((1, 1, gl, ATT_TILE), lambda bi, p, i: (bi, i, p, 0)),
            pl.BlockSpec((2, None, t, gl), lambda bi, p, i: (0, bi, 0, p)),
            pl.BlockSpec((1, t // ATT_CHUNK, gl, ATT_CHUNK), lambda bi, p, i: (bi, 0, p, 0)),
            pl.BlockSpec((1, n_blocks, gl), lambda bi, p, i: (bi, 0, p)),
            pl.BlockSpec((4, HEAD_DIM), lambda bi, p, i: (0, 0)),
            pl.BlockSpec((PAIR, PAIR), lambda bi, p, i: (0, 0)),
        ],
        out_specs=pl.BlockSpec((1, 1, gl, ATT_TILE), lambda bi, p, i: (bi, i, p, 0)),
        out_shape=jax.ShapeDtypeStruct((b, nq, d, ATT_TILE), BF16),
        scratch_shapes=[
            pltpu.VMEM((ATT_GROUP, 2, PAIR, ATT_TILE), BF16),
            pltpu.VMEM((ATT_GROUP, 2, 1, ATT_TILE), F32),
            pltpu.VMEM((ATT_GROUP, 2, 1, ATT_TILE), F32),
            pltpu.VMEM((ATT_GROUP, 2, HEAD_DIM if moba else PAIR, ATT_TILE), F32),
            pltpu.VMEM((ATT_GROUP, 2, 8, PAIR), F32),
            pltpu.VMEM((2, ATT_GROUP, 2, ATT_CHUNK, ATT_TILE), BF16),
            pltpu.VMEM((ATT_CHUNK, ATT_TILE), BF16),
            pltpu.VMEM((ATT_GROUP, 2, 1, ATT_TILE), F32),
            pltpu.VMEM((ATT_GROUP, KM_PIECES * AUG_BLOCKS, PAIR), BF16),
        ],
        compiler_params=pltpu.CompilerParams(
            dimension_semantics=("arbitrary", "arbitrary", "arbitrary"),
            vmem_limit_bytes=V7X_VMEM_LIMIT_BYTES),
        name="moba_attention" if moba else "diff_attention",
    )(slope_table, qt, ka.reshape(2, b, t, d), vt, km, lam_params,
      jnp.broadcast_to(subln_w[:, None], (PAIR, PAIR)))


def kernel(x, attn_norm, w_in, w_out, diff_lambda, diff_subln, mlp_norm, w_ff1, w_ff2, final_norm):
    b, t, d = x.shape
    h = x.reshape(b * t, d)
    q_scale = HEAD_DIM ** -0.5 * LOG2E
    for i in range(DEPTH):
        moba = i % N_MIXERS == 1
        li = i // N_MIXERS
        qt, ka, vt, km = _qkv_projection(h, attn_norm[i], w_in, i, batch=b, q_scale=q_scale)
        slope_table = jnp.asarray(_alibi_slope_pieces(MOBA_HEADS if moba else DIFF_HEADS))
        mix = _attention(qt, ka, vt, km, slope_table, diff_lambda[li], diff_subln[li],
                         moba=moba, lam_init=_lambda_init(i))
        h, u = _attn_out_mlp_up(mix, w_out, h, mlp_norm[i], w_ff1, i)
        h = _mlp_down(u, w_ff2, h, final_norm, i, final_norm=(i == DEPTH - 1))
    return h.reshape(b, t, d)
```

```python
import functools
import math

import jax
import jax.numpy as jnp
import numpy as np
from jax import lax
from jax.experimental import pallas as pl
from jax.experimental.pallas import tpu as pltpu

D_MODEL = 1024
DEPTH = 2
N_MIXERS = 2
DIFF_HEADS = 8
MOBA_HEADS = 16
HEAD_DIM = 64
PAIR = 2 * HEAD_DIM
N_PAIRS = D_MODEL // PAIR
MOBA_BLOCK = 256
MOBA_TOPK = 3
D_FF = 4 * D_MODEL
RMS_EPS = 1e-6
NEG_INF = -1e30
LOG2E = math.log2(math.e)

V7X_VMEM_LIMIT_BYTES = 56 * 1024 * 1024

ROW_TILE = 512
COL_TILE = 1024
COL_CHUNK = 256
KEY_TILE = MOBA_BLOCK
ATT_CHUNK_TILES = 2
ATT_CHUNK = ATT_CHUNK_TILES * KEY_TILE
Q_CHUNKS = 2
ATT_TILE = Q_CHUNKS * ATT_CHUNK
ATT_GROUP = 2

AUG_BASE = (HEAD_DIM, 0)
AUG_BLOCKS = 32
AUG_POS = AUG_BLOCKS
AUG_SLOPE_PIECES = 3
AUG_ROWS = 16

BF16 = jnp.bfloat16
F32 = jnp.float32


def _alibi_slope_pieces(n_heads):
    slopes = np.array([2.0 ** (-8.0 * (h + 1) / n_heads) for h in range(n_heads)], np.float64)
    rest = slopes * LOG2E
    pieces = []
    for _ in range(AUG_SLOPE_PIECES):
        p = rest.astype(np.float32).astype(BF16).astype(np.float64)
        pieces.append(p)
        rest = rest - p
    pieces = np.stack(pieces, axis=1)
    table = np.concatenate([pieces, pieces.sum(axis=1, keepdims=True)], axis=1)
    return table.astype(np.float32).reshape(-1)


def _lambda_init(layer_idx):
    return 0.8 - 0.6 * math.exp(-0.3 * layer_idx)


def _rmsnorm_rows(x, g):
    return x * lax.rsqrt(jnp.mean(x * x, axis=-1, keepdims=True) + RMS_EPS) * g


_NT = (((1,), (1,)), ((), ()))


def _layer_weight_spec(w, layer):
    return pl.BlockSpec((None,) + w.shape[1:], lambda i: (layer, 0, 0),
                        pipeline_mode=pl.Buffered(1))


def _cast_columns(dst_ref, src_ref, c0, c1, *, transpose=False):
    for j in range(c0, c1, COL_CHUNK):
        blk = src_ref[:, j:j + COL_CHUNK]
        if transpose:
            dst_ref[j - c0:j - c0 + COL_CHUNK, :] = blk.T.astype(BF16)
        else:
            dst_ref[:, j - c0:j - c0 + COL_CHUNK] = blk.astype(BF16)


def _qkv_kernel(x_ref, g_ref, w_ref, qt_ref, ka_ref, vt_ref, *rest, q_scale, steps_per_seq):
    *km_refs, wqt_ref, wk_ref, wvt_ref = rest

    @pl.when(pl.program_id(0) == 0)
    def _():
        _cast_columns(wqt_ref, w_ref, 0, D_MODEL, transpose=True)
        _cast_columns(wk_ref, w_ref, D_MODEL, 2 * D_MODEL)
        _cast_columns(wvt_ref, w_ref, 2 * D_MODEL, 3 * D_MODEL, transpose=True)

    lane = lax.broadcasted_iota(jnp.int32, (KEY_TILE, PAIR), 1)
    pos = lax.broadcasted_iota(jnp.int32, (KEY_TILE, PAIR), 0).astype(F32)
    for part in range(ROW_TILE // KEY_TILE):
        rows = slice(part * KEY_TILE, (part + 1) * KEY_TILE)
        xn = _rmsnorm_rows(x_ref[rows], g_ref[...]).astype(BF16)
        k = jnp.dot(xn, wk_ref[...], preferred_element_type=F32).astype(BF16)
        tile = (pl.program_id(0) % steps_per_seq) * (ROW_TILE // KEY_TILE) + part
        for km_ref in km_refs:
            km_ref[0, pl.ds(tile, 1), :] = jnp.mean(k.astype(F32), axis=0, keepdims=True)
        tile_pos = float((part % ATT_CHUNK_TILES) * KEY_TILE)
        for s in range(2):
            rel = lane - AUG_BASE[s]
            aug = jnp.where(rel == tile, 1.0, 0.0)
            aug = jnp.where(jnp.logical_and(rel >= AUG_POS, rel < AUG_POS + AUG_SLOPE_PIECES), pos, aug)
            aug = jnp.where(jnp.logical_and(rel >= AUG_POS + AUG_SLOPE_PIECES,
                                            rel < AUG_POS + 2 * AUG_SLOPE_PIECES), tile_pos, aug)
            aug = aug.astype(BF16)
            own = jnp.logical_and(rel >= -HEAD_DIM, rel < 0) if s == 0 else rel >= HEAD_DIM
            for c0 in range(0, D_MODEL, PAIR):
                ka_ref[s, rows, c0:c0 + PAIR] = jnp.where(own, k[:, c0:c0 + PAIR], aug)
        for c0 in range(0, D_MODEL, 256):
            vt = lax.dot_general(wvt_ref[c0:c0 + 256, :], xn, _NT, preferred_element_type=F32)
            vt_ref[0, 0, c0:c0 + 256, rows] = vt.astype(BF16)
            qt = lax.dot_general(wqt_ref[c0:c0 + 256, :], xn, _NT, preferred_element_type=F32)
            qt_ref[0, 0, c0:c0 + 256, rows] = (qt * q_scale).astype(BF16)


def _qkv_projection(x, g, w_in, layer, *, batch, q_scale, block_means):
    m, d = x.shape
    t = m // batch
    spb = t // ROW_TILE
    spc = ATT_CHUNK // ROW_TILE
    spq = ATT_TILE // ROW_TILE
    assert t // KEY_TILE <= AUG_BLOCKS and ATT_CHUNK % ROW_TILE == 0 and t % ATT_CHUNK == 0
    assert ATT_TILE % ROW_TILE == 0 and ROW_TILE % ATT_CHUNK == 0 and t % ATT_TILE == 0
    return pl.pallas_call(
        functools.partial(_qkv_kernel, q_scale=q_scale, steps_per_seq=spb),
        grid=(m // ROW_TILE,),
        in_specs=[
            pl.BlockSpec((ROW_TILE, d), lambda i: (i, 0)),
            pl.BlockSpec((1, d), lambda i: (0, 0)),
            _layer_weight_spec(w_in, layer),
        ],
        out_specs=[
            pl.BlockSpec((1, 1, d, ROW_TILE), lambda i: (i // spb, (i % spb) // spq, 0, i % spq)),
            pl.BlockSpec((2, ROW_TILE, d), lambda i: (0, i, 0)),
            pl.BlockSpec((1, 1, d, ROW_TILE), lambda i: (i // spb, (i % spb) // spc, 0, i % spc)),
        ] + [pl.BlockSpec((1, t // KEY_TILE, d), lambda i: (i // spb, 0, 0))] * block_means,
        out_shape=[
            jax.ShapeDtypeStruct((batch, t // ATT_TILE, d, ATT_TILE), BF16),
            jax.ShapeDtypeStruct((2, m, d), BF16),
            jax.ShapeDtypeStruct((batch, t // ATT_CHUNK, d, ATT_CHUNK), BF16),
        ] + [jax.ShapeDtypeStruct((batch, t // KEY_TILE, d), F32)] * block_means,
        scratch_shapes=[pltpu.VMEM((d, d), BF16)] * 3,
        compiler_params=pltpu.CompilerParams(
            dimension_semantics=("arbitrary",), vmem_limit_bytes=V7X_VMEM_LIMIT_BYTES),
        name="qkv_projection",
    )(x, g.reshape(1, d), w_in)


def _attn_out_mlp_up_kernel(a_ref, wo_ref, r_ref, g_ref, w1_ref, h_ref, u_ref, wo_b_ref, w1_b_ref):
    @pl.when(pl.program_id(0) == 0)
    def _():
        _cast_columns(wo_b_ref, wo_ref, 0, wo_ref.shape[1])
        _cast_columns(w1_b_ref, w1_ref, 0, w1_ref.shape[1])

    h = r_ref[...] + lax.dot_general(a_ref[...], wo_b_ref[...], (((0,), (0,)), ((), ())),
                                     preferred_element_type=F32)
    h_ref[...] = h
    xn = _rmsnorm_rows(h, g_ref[...]).astype(BF16)
    for c0 in range(0, w1_b_ref.shape[1], COL_TILE):
        acc = jnp.dot(xn, w1_b_ref[:, c0:c0 + COL_TILE], preferred_element_type=F32)
        u_ref[:, c0:c0 + COL_TILE] = jnp.square(jnp.maximum(acc, 0.0)).astype(u_ref.dtype)


def _attn_out_mlp_up(a_t, w_out, r, g, w_ff1, layer):
    m, d = r.shape
    n = w_ff1.shape[2]
    spq = a_t.shape[3] // ROW_TILE
    spb = a_t.shape[1] * spq
    assert a_t.shape[2] == d and a_t.shape[3] % ROW_TILE == 0
    rows = lambda width: pl.BlockSpec((ROW_TILE, width), lambda i: (i, 0))
    return pl.pallas_call(
        _attn_out_mlp_up_kernel,
        grid=(m // ROW_TILE,),
        in_specs=[pl.BlockSpec((None, None, d, ROW_TILE),
                               lambda i: (i // spb, (i % spb) // spq, 0, i % spq)),
                  _layer_weight_spec(w_out, layer), rows(d),
                  pl.BlockSpec((1, d), lambda i: (0, 0)), _layer_weight_spec(w_ff1, layer)],
        out_specs=[rows(d), rows(n)],
        out_shape=[jax.ShapeDtypeStruct((m, d), F32), jax.ShapeDtypeStruct((m, n), BF16)],
        scratch_shapes=[pltpu.VMEM((d, d), BF16), pltpu.VMEM((d, n), BF16)],
        compiler_params=pltpu.CompilerParams(
            dimension_semantics=("arbitrary",), vmem_limit_bytes=V7X_VMEM_LIMIT_BYTES),
        name="attn_out_mlp_up",
    )(a_t, w_out, r, g.reshape(1, d), w_ff1)


def _mlp_down_kernel(a_ref, w_ref, r_ref, g_ref, o_ref, w_b_ref, *, final_norm):
    @pl.when(pl.program_id(0) == 0)
    def _():
        _cast_columns(w_b_ref, w_ref, 0, w_ref.shape[1])

    h = r_ref[...] + jnp.dot(a_ref[...], w_b_ref[...], preferred_element_type=F32)
    if final_norm:
        h = _rmsnorm_rows(h, g_ref[...])
    o_ref[...] = h


def _mlp_down(a, w_ff2, r, g, layer, *, final_norm):
    m, k = a.shape
    n = w_ff2.shape[2]
    return pl.pallas_call(
        functools.partial(_mlp_down_kernel, final_norm=final_norm),
        grid=(m // ROW_TILE,),
        in_specs=[
            pl.BlockSpec((ROW_TILE, k), lambda i: (i, 0)),
            _layer_weight_spec(w_ff2, layer),
            pl.BlockSpec((ROW_TILE, n), lambda i: (i, 0)),
            pl.BlockSpec((1, n), lambda i: (0, 0)),
        ],
        out_specs=pl.BlockSpec((ROW_TILE, n), lambda i: (i, 0)),
        out_shape=jax.ShapeDtypeStruct((m, n), F32),
        scratch_shapes=[pltpu.VMEM((k, n), BF16)],
        compiler_params=pltpu.CompilerParams(
            dimension_semantics=("arbitrary",), vmem_limit_bytes=V7X_VMEM_LIMIT_BYTES),
        name="mlp_down",
    )(a, w_ff2, r, g.reshape(1, n))


SUM_ON_MXU = {False: False, True: True}
L_MIN = 2.0 ** -60
UB_MARGIN = 1.01
KM_PIECES = 3
NORM_UNROLL = 16


def _attention_kernel(slope_ref, qt_ref, ka_ref, vt_ref, km_ref, lam_ref, subln_ref, o_ref,
                      qa_ref, ub_ref, l_ref, acc_ref, kn_ref, p_ref, ps_ref, m_ref, kmp_ref, *, moba,
                      lam_init):
    group = pl.program_id(1)
    qi = pl.program_id(2)
    t = ATT_TILE
    n_tab = AUG_SLOPE_PIECES + 1
    n_blocks = ka_ref.shape[1] // MOBA_BLOCK
    n_chunks = ka_ref.shape[1] // ATT_CHUNK

    def head_of(g, s):
        pair = group * ATT_GROUP + g
        return 2 * pair + s if moba else pair

    @pl.when(qi == 0)
    def _():
        lane8 = lax.broadcasted_iota(jnp.int32, (8, PAIR), 1)
        row8 = lax.broadcasted_iota(jnp.int32, (8, PAIR), 0)
        selector = jnp.where(row8 == (lane8 >= HEAD_DIM).astype(jnp.int32), 1.0, 0.0).astype(BF16)
        first_lanes = lax.broadcasted_iota(jnp.int32, (ATT_CHUNK, PAIR), 1) < HEAD_DIM
        unroll = math.gcd(n_chunks, NORM_UNROLL)
        for g in range(ATT_GROUP):
            def chunk_norms(trip, best):
                for c in range(unroll):
                    r0 = pl.multiple_of((trip * unroll + c) * ATT_CHUNK, ATT_CHUNK)
                    lanes = slice(g * PAIR, (g + 1) * PAIR)
                    k_pair = jnp.where(first_lanes, ka_ref[0, pl.ds(r0, ATT_CHUNK), lanes],
                                       ka_ref[1, pl.ds(r0, ATT_CHUNK), lanes])
                    sq = lax.dot_general(selector, k_pair * k_pair, _NT,
                                         preferred_element_type=F32)
                    best = jnp.maximum(best, sq)
                return best

            best = lax.fori_loop(0, n_chunks // unroll, chunk_norms,
                                 jnp.zeros((8, ATT_CHUNK), F32))
            k_norm = jnp.sqrt(jnp.max(best, axis=1, keepdims=True))
            for s in range(2):
                kn_ref[g, s] = jnp.broadcast_to(k_norm[s:s + 1], (8, PAIR))

    if moba:
        @pl.when(qi == 0)
        def _():
            if n_blocks < AUG_BLOCKS:
                kmp_ref[...] = jnp.zeros_like(kmp_ref)
            for g in range(ATT_GROUP):
                rest = km_ref[0, :, g * PAIR:(g + 1) * PAIR]
                for piece in range(KM_PIECES):
                    part = rest.astype(BF16)
                    kmp_ref[g, piece * AUG_BLOCKS:piece * AUG_BLOCKS + n_blocks, :] = part
                    rest = rest - part.astype(F32)

    row16 = lax.broadcasted_iota(jnp.int32, (AUG_ROWS, t), 0)
    own_rows = lax.broadcasted_iota(jnp.int32, (PAIR, t), 0) < HEAD_DIM
    blk = lax.broadcasted_iota(jnp.int32, (AUG_BLOCKS, t), 0)
    blk_f = blk.astype(F32)
    q_idx = lax.broadcasted_iota(jnp.int32, (1, t), 1)
    q_pos = q_idx.astype(F32)
    own_blk = qi * (t // MOBA_BLOCK) + q_idx // MOBA_BLOCK
    for g in range(ATT_GROUP):
        qt = qt_ref[0, 0, g * PAIR:(g + 1) * PAIR, :].astype(F32)
        for s in range(2):
            h = head_of(g, s)
            pos_rows = jnp.zeros((AUG_ROWS, t), F32)
            for piece in range(AUG_SLOPE_PIECES):
                hit = jnp.logical_or(row16 == piece, row16 == piece + AUG_SLOPE_PIECES)
                pos_rows = jnp.where(hit, slope_ref[h * n_tab + piece], pos_rows)
            if moba:
                q_own = jnp.where(own_rows if s == 0 else jnp.logical_not(own_rows), qt, 0.0)
                gate_pieces = jnp.dot(kmp_ref[g], q_own.astype(BF16),
                                      preferred_element_type=F32)
                gate = gate_pieces[0:AUG_BLOCKS]
                for piece in range(1, KM_PIECES):
                    gate = gate + gate_pieces[piece * AUG_BLOCKS:(piece + 1) * AUG_BLOCKS]
                gt = jnp.where(blk < own_blk, gate, -jnp.inf)
                selected = blk == own_blk
                for pick in range(MOBA_TOPK):
                    mx = jnp.max(gt, axis=0, keepdims=True)
                    idx = jnp.min(jnp.where(gt == mx, blk_f, F32(AUG_BLOCKS)), axis=0,
                                  keepdims=True)
                    hit = jnp.logical_and(blk_f == idx, pick < own_blk)
                    selected = jnp.logical_or(selected, hit)
                    gt = jnp.where(blk_f == idx, -jnp.inf, gt)
                sel_rows = jnp.where(selected, 0.0, NEG_INF)
            else:
                sel_rows = jnp.zeros((AUG_BLOCKS, t), F32)
            own = qt[0:HEAD_DIM] if s == 0 else qt[HEAD_DIM:PAIR]
            q_norm = jnp.sqrt(jnp.sum(own * own, axis=0, keepdims=True))
            ub_ref[g, s] = (q_norm * kn_ref[g, s][0:1, 0:1] * UB_MARGIN
                            + slope_ref[h * n_tab + AUG_SLOPE_PIECES] * q_pos)
            tail = jnp.zeros((HEAD_DIM - AUG_BLOCKS - AUG_ROWS, t), F32)
            parts = [sel_rows, pos_rows, tail]
            parts = [own] + parts if s == 0 else parts + [own]
            qa_ref[g, s] = jnp.concatenate(parts, axis=0).astype(BF16)

    chains = [(g, s) for g in range(ATT_GROUP) for s in range(2)]
    first_own = qi * Q_CHUNKS
    half = ATT_CHUNK

    def chunk_offset(g, s, c):
        return (slope_ref[head_of(g, s) * n_tab + AUG_SLOPE_PIECES]
                * ((c - first_own) * ATT_CHUNK).astype(F32))

    v_rows = HEAD_DIM if moba else PAIR
    ones_rows = jnp.ones((AUG_ROWS, ATT_CHUNK), BF16)

    def key_rows(g, s, c):
        r0 = pl.multiple_of(c * ATT_CHUNK, ATT_CHUNK)
        return ka_ref[s, pl.ds(r0, ATT_CHUNK), g * PAIR:(g + 1) * PAIR]

    def value_rows(g, s, c):
        v0 = g * PAIR + (s * HEAD_DIM if moba else 0)
        return jnp.concatenate([vt_ref[0, c, v0:v0 + v_rows, :], ones_rows], axis=0)

    def probabilities(g, s, c, slot):
        st = jnp.dot(key_rows(g, s, c), qa_ref[g, s], preferred_element_type=F32)
        p = jnp.exp2(st - (ub_ref[g, s] - chunk_offset(g, s, c)))
        p_ref[slot, g, s] = p.astype(BF16)
        if not SUM_ON_MXU[moba]:
            l_ref[g, s] += jnp.sum(p, axis=0, keepdims=True)

    def values(g, s, c, slot, q0=0):
        if SUM_ON_MXU[moba]:
            pv = jnp.dot(value_rows(g, s, c), p_ref[slot, g, s, :, q0:t],
                         preferred_element_type=F32)
            l_ref[g, s, :, q0:t] += pv[v_rows:v_rows + 1]
            acc_ref[g, s, :, q0:t] += pv[0:v_rows]
        else:
            v0 = g * PAIR + (s * HEAD_DIM if moba else 0)
            acc_ref[g, s, :, q0:t] += jnp.dot(vt_ref[0, c, v0:v0 + v_rows, :],
                                              p_ref[slot, g, s, :, q0:t],
                                              preferred_element_type=F32)

    row_i = lax.broadcasted_iota(jnp.int32, (ATT_CHUNK, t), 0)
    col_i = lax.broadcasted_iota(jnp.int32, (ATT_CHUNK, t), 1)
    mask_first = row_i <= col_i
    mask_second = row_i + half <= col_i

    def own_probabilities(g, s, which):
        c = first_own + which
        slot = 1 - which
        q0 = which * half
        r0 = pl.multiple_of(c * ATT_CHUNK, ATT_CHUNK)
        lanes = slice(g * PAIR, (g + 1) * PAIR)
        mask = mask_second if which else mask_first
        off = chunk_offset(g, s, c)

        def strip(keys, a, b, masked):
            st = jnp.dot(ka_ref[s, pl.ds(r0, keys), lanes], qa_ref[g, s, :, a:b],
                         preferred_element_type=F32)
            if masked:
                st = jnp.where(mask[0:keys, a:b], st, NEG_INF)
            p = jnp.exp2(st - (ub_ref[g, s, :, a:b] - off))
            p_ref[slot, g, s, 0:keys, a:b] = p.astype(BF16)
            if keys < ATT_CHUNK:
                p_ref[slot, g, s, keys:ATT_CHUNK, a:b] = jnp.zeros((ATT_CHUNK - keys, b - a), BF16)
            if not SUM_ON_MXU[moba]:
                l_ref[g, s, :, a:b] += jnp.sum(p, axis=0, keepdims=True)

        strip(KEY_TILE, q0, q0 + KEY_TILE, True)
        strip(ATT_CHUNK, q0 + KEY_TILE, q0 + half, True)
        if q0 + half < t:
            strip(ATT_CHUNK, q0 + half, t, False)

    def write_output():
        for g in range(ATT_GROUP):
            out_a = acc_ref[g, 0] / l_ref[g, 0]
            out_b = acc_ref[g, 1] / l_ref[g, 1]
            if moba:
                o = jnp.concatenate([out_a, out_b], axis=0)
            else:
                lp = lam_ref[...]
                lam = (jnp.exp(jnp.sum(lp[0:1] * lp[1:2], axis=-1, keepdims=True))
                       - jnp.exp(jnp.sum(lp[2:3] * lp[3:4], axis=-1, keepdims=True)) + lam_init)
                o = out_a - lam * out_b
                rms = lax.rsqrt(jnp.mean(o * o, axis=0, keepdims=True) + RMS_EPS)
                o = o * rms * pltpu.repeat(subln_ref[...], t // PAIR, axis=1) * (1.0 - lam_init)
            o_ref[0, 0, g * PAIR:(g + 1) * PAIR, :] = o.astype(o_ref.dtype)

    l_ref[...] = jnp.zeros_like(l_ref)
    acc_ref[...] = jnp.zeros_like(acc_ref)
    for g, s in chains:
        own_probabilities(g, s, 1)
    for g, s in chains:
        own_probabilities(g, s, 0)
        values(g, s, first_own + 1, 0, q0=half)

    def two_steps(j, carry):
        c_odd = jnp.where(j == 0, first_own, 2 * j - 1)
        for g, s in chains:
            probabilities(g, s, 2 * j, 0)
            values(g, s, c_odd, 1)
        for g, s in chains:
            probabilities(g, s, 2 * j + 1, 1)
            values(g, s, 2 * j, 0)
        return carry

    lax.fori_loop(0, qi, two_steps, 0)
    c_last = jnp.where(qi == 0, first_own, first_own - 1)
    for g, s in chains:
        values(g, s, c_last, 1)
    write_output()

    def safe_path():
        m_ref[...] = jnp.full_like(m_ref, NEG_INF)
        l_ref[...] = jnp.zeros_like(l_ref)
        acc_ref[...] = jnp.zeros_like(acc_ref)

        def fold(g, s, c, mask):
            st = jnp.dot(key_rows(g, s, c), qa_ref[g, s], preferred_element_type=F32)
            if mask is not None:
                st = jnp.where(mask, st, NEG_INF)
            off = chunk_offset(g, s, c)
            m_prev = m_ref[g, s]
            m_next = jnp.maximum(m_prev, jnp.max(st, axis=0, keepdims=True) + off)
            alpha = jnp.exp2(m_prev - m_next)
            ps_ref[...] = jnp.exp2(st - (m_next - off)).astype(BF16)
            pv = jnp.dot(value_rows(g, s, c), ps_ref[...], preferred_element_type=F32)
            m_ref[g, s] = m_next
            l_ref[g, s] = alpha * l_ref[g, s] + pv[v_rows:v_rows + 1]
            acc_ref[g, s] = alpha * acc_ref[g, s] + pv[0:v_rows]

        for g, s in chains:
            fold(g, s, first_own, mask_first)
        for g, s in chains:
            fold(g, s, first_own + 1, mask_second)

        def past(c, carry):
            for g, s in chains:
                fold(g, s, c, None)
            return carry

        lax.fori_loop(0, first_own, past, 0)

    l_min = l_ref[0, 0]
    for g, s in chains[1:]:
        l_min = jnp.minimum(l_min, l_ref[g, s])

    @pl.when(jnp.logical_not(jnp.min(l_min) >= L_MIN))
    def _():
        safe_path()
        write_output()


def _attention(qt, ka, vt, km, slope_table, lam_params, subln_w, *, moba, lam_init):
    b, nq, d, _ = qt.shape
    t = nq * ATT_TILE
    n_blocks = t // MOBA_BLOCK
    assert ATT_TILE == Q_CHUNKS * ATT_CHUNK and Q_CHUNKS == 2 and n_blocks <= AUG_BLOCKS
    assert AUG_POS + 2 * AUG_SLOPE_PIECES <= AUG_BLOCKS + AUG_ROWS <= HEAD_DIM
    gl = ATT_GROUP * PAIR
    return pl.pallas_call(
        functools.partial(_attention_kernel, moba=moba, lam_init=lam_init),
        grid=(b, N_PAIRS // ATT_GROUP, nq),
        in_specs=[
            pl.BlockSpec(memory_space=pltpu.SMEM),
            pl.BlockSpec((1, 1, gl, ATT_TILE), lambda bi, p, i: (bi, i, p, 0)),
            pl.BlockSpec((2, None, t, gl), lambda bi, p, i: (0, bi, 0, p)),
            pl.BlockSpec((1, t // ATT_CHUNK, gl, ATT_CHUNK), lambda bi, p, i: (bi, 0, p, 0)),
            pl.BlockSpec((1, n_blocks, gl), lambda bi, p, i: (bi, 0, p)),
            pl.BlockSpec((4, HEAD_DIM), lambda bi, p, i: (0, 0)),
            pl.BlockSpec((PAIR, PAIR), lambda bi, p, i: (0, 0)),
        ],
        out_specs=pl.BlockSpec((1, 1, gl, ATT_TILE), lambda bi, p, i: (bi, i, p, 0)),
        out_shape=jax.ShapeDtypeStruct((b, nq, d, ATT_TILE), BF16),
        scratch_shapes=[
            pltpu.VMEM((ATT_GROUP, 2, PAIR, ATT_TILE), BF16),
            pltpu.VMEM((ATT_GROUP, 2, 1, ATT_TILE), F32),
            pltpu.VMEM((ATT_GROUP, 2, 1, ATT_TILE), F32),
            pltpu.VMEM((ATT_GROUP, 2, HEAD_DIM if moba else PAIR, ATT_TILE), F32),
            pltpu.VMEM((ATT_GROUP, 2, 8, PAIR), F32),
            pltpu.VMEM((2, ATT_GROUP, 2, ATT_CHUNK, ATT_TILE), BF16),
            pltpu.VMEM((ATT_CHUNK, ATT_TILE), BF16),
            pltpu.VMEM((ATT_GROUP, 2, 1, ATT_TILE), F32),
            pltpu.VMEM((ATT_GROUP, KM_PIECES * AUG_BLOCKS, PAIR), BF16),
        ],
        compiler_params=pltpu.CompilerParams(
            dimension_semantics=("arbitrary", "arbitrary", "arbitrary"),
            vmem_limit_bytes=V7X_VMEM_LIMIT_BYTES),
        name="moba_attention" if moba else "diff_attention",
    )(slope_table, qt, ka.reshape(2, b, t, d), vt, km, lam_params,
      jnp.broadcast_to(subln_w[:, None], (PAIR, PAIR)))


def kernel(x, attn_norm, w_in, w_out, diff_lambda, diff_subln, mlp_norm, w_ff1, w_ff2, final_norm):
    b, t, d = x.shape
    h = x.reshape(b * t, d)
    q_scale = HEAD_DIM ** -0.5 * LOG2E
    for i in range(DEPTH):
        moba = i % N_MIXERS == 1
        li = i // N_MIXERS
        qt, ka, vt, *km = _qkv_projection(h, attn_norm[i], w_in, i, batch=b, q_scale=q_scale,
                                          block_means=moba)
        km = km[0] if moba else jnp.zeros((b, t // MOBA_BLOCK, d), F32)
        slope_table = jnp.asarray(_alibi_slope_pieces(MOBA_HEADS if moba else DIFF_HEADS))
        mix = _attention(qt, ka, vt, km, slope_table, diff_lambda[li], diff_subln[li],
                         moba=moba, lam_init=_lambda_init(i))
        h, u = _attn_out_mlp_up(mix, w_out, h, mlp_norm[i], w_ff1, i)
        h = _mlp_down(u, w_ff2, h, final_norm, i, final_norm=(i == DEPTH - 1))
    return h.reshape(b, t, d)
```

```python
import functools
import math

import jax
import jax.numpy as jnp
import numpy as np
from jax import lax
from jax.experimental import pallas as pl
from jax.experimental.pallas import tpu as pltpu

D_MODEL = 1024
DEPTH = 2
N_MIXERS = 2
DIFF_HEADS = 8
MOBA_HEADS = 16
HEAD_DIM = 64
PAIR = 2 * HEAD_DIM
N_PAIRS = D_MODEL // PAIR
MOBA_BLOCK = 256
MOBA_TOPK = 3
D_FF = 4 * D_MODEL
RMS_EPS = 1e-6
NEG_INF = -1e30
LOG2E = math.log2(math.e)

V7X_VMEM_LIMIT_BYTES = 56 * 1024 * 1024

ROW_TILE = 512
QKV_ROW_TILE = 1024
COL_TILE = 1024
COL_CHUNK = 256
KEY_TILE = MOBA_BLOCK
ATT_CHUNK_TILES = 2
ATT_CHUNK = ATT_CHUNK_TILES * KEY_TILE
Q_CHUNKS = 2
ATT_TILE = Q_CHUNKS * ATT_CHUNK
ATT_GROUP = 2

AUG_BASE = (HEAD_DIM, 0)
AUG_BLOCKS = 32
AUG_POS = AUG_BLOCKS
AUG_SLOPE_PIECES = 3
AUG_ROWS = 16

BF16 = jnp.bfloat16
F32 = jnp.float32


def _alibi_slope_pieces(n_heads):
    slopes = np.array([2.0 ** (-8.0 * (h + 1) / n_heads) for h in range(n_heads)], np.float64)
    rest = slopes * LOG2E
    pieces = []
    for _ in range(AUG_SLOPE_PIECES):
        p = rest.astype(np.float32).astype(BF16).astype(np.float64)
        pieces.append(p)
        rest = rest - p
    pieces = np.stack(pieces, axis=1)
    table = np.concatenate([pieces, pieces.sum(axis=1, keepdims=True)], axis=1)
    return table.astype(np.float32).reshape(-1)


def _lambda_init(layer_idx):
    return 0.8 - 0.6 * math.exp(-0.3 * layer_idx)


def _rmsnorm_rows(x, g):
    return x * lax.rsqrt(jnp.mean(x * x, axis=-1, keepdims=True) + RMS_EPS) * g


_NT = (((1,), (1,)), ((), ()))


def _layer_weight_spec(w, layer):
    return pl.BlockSpec((None,) + w.shape[1:], lambda i: (layer, 0, 0),
                        pipeline_mode=pl.Buffered(1))


def _cast_columns(dst_ref, src_ref, c0, c1, *, transpose=False):
    for j in range(c0, c1, COL_CHUNK):
        blk = src_ref[:, j:j + COL_CHUNK]
        if transpose:
            dst_ref[j - c0:j - c0 + COL_CHUNK, :] = blk.T.astype(BF16)
        else:
            dst_ref[:, j - c0:j - c0 + COL_CHUNK] = blk.astype(BF16)


def _qkv_kernel(x_ref, g_ref, w_ref, qt_ref, ka_ref, vt_ref, *rest, q_scale, steps_per_seq):
    *km_refs, wqt_ref, wk_ref, wvt_ref = rest

    @pl.when(pl.program_id(0) == 0)
    def _():
        _cast_columns(wqt_ref, w_ref, 0, D_MODEL, transpose=True)
        _cast_columns(wk_ref, w_ref, D_MODEL, 2 * D_MODEL)
        _cast_columns(wvt_ref, w_ref, 2 * D_MODEL, 3 * D_MODEL, transpose=True)

    lane = lax.broadcasted_iota(jnp.int32, (KEY_TILE, PAIR), 1)
    pos = lax.broadcasted_iota(jnp.int32, (KEY_TILE, PAIR), 0).astype(F32)
    for part in range(QKV_ROW_TILE // KEY_TILE):
        rows = slice(part * KEY_TILE, (part + 1) * KEY_TILE)
        chunk = part // ATT_CHUNK_TILES
        chunk_cols = slice((part % ATT_CHUNK_TILES) * KEY_TILE, (part % ATT_CHUNK_TILES + 1) * KEY_TILE)
        xn = _rmsnorm_rows(x_ref[rows], g_ref[...]).astype(BF16)
        k = jnp.dot(xn, wk_ref[...], preferred_element_type=F32).astype(BF16)
        tile = (pl.program_id(0) % steps_per_seq) * (QKV_ROW_TILE // KEY_TILE) + part
        for km_ref in km_refs:
            km_ref[0, pl.ds(tile, 1), :] = jnp.mean(k.astype(F32), axis=0, keepdims=True)
        tile_pos = float((part % ATT_CHUNK_TILES) * KEY_TILE)
        for s in range(2):
            rel = lane - AUG_BASE[s]
            aug = jnp.where(rel == tile, 1.0, 0.0)
            aug = jnp.where(jnp.logical_and(rel >= AUG_POS, rel < AUG_POS + AUG_SLOPE_PIECES), pos, aug)
            aug = jnp.where(jnp.logical_and(rel >= AUG_POS + AUG_SLOPE_PIECES,
                                            rel < AUG_POS + 2 * AUG_SLOPE_PIECES), tile_pos, aug)
            aug = aug.astype(BF16)
            own = jnp.logical_and(rel >= -HEAD_DIM, rel < 0) if s == 0 else rel >= HEAD_DIM
            for c0 in range(0, D_MODEL, PAIR):
                ka_ref[s, rows, c0:c0 + PAIR] = jnp.where(own, k[:, c0:c0 + PAIR], aug)
        for c0 in range(0, D_MODEL, 256):
            vt = lax.dot_general(wvt_ref[c0:c0 + 256, :], xn, _NT, preferred_element_type=F32)
            vt_ref[0, chunk, c0:c0 + 256, chunk_cols] = vt.astype(BF16)
            qt = lax.dot_general(wqt_ref[c0:c0 + 256, :], xn, _NT, preferred_element_type=F32)
            qt_ref[0, 0, c0:c0 + 256, rows] = (qt * q_scale).astype(BF16)


def _qkv_projection(x, g, w_in, layer, *, batch, q_scale, block_means):
    m, d = x.shape
    t = m // batch
    rt = QKV_ROW_TILE
    spb = t // rt
    cps = rt // ATT_CHUNK
    assert t // KEY_TILE <= AUG_BLOCKS and rt == ATT_TILE and rt % ATT_CHUNK == 0 and t % rt == 0
    return pl.pallas_call(
        functools.partial(_qkv_kernel, q_scale=q_scale, steps_per_seq=spb),
        grid=(m // rt,),
        in_specs=[
            pl.BlockSpec((rt, d), lambda i: (i, 0)),
            pl.BlockSpec((1, d), lambda i: (0, 0)),
            _layer_weight_spec(w_in, layer),
        ],
        out_specs=[
            pl.BlockSpec((1, 1, d, rt), lambda i: (i // spb, i % spb, 0, 0)),
            pl.BlockSpec((2, rt, d), lambda i: (0, i, 0)),
            pl.BlockSpec((1, cps, d, ATT_CHUNK), lambda i: (i // spb, i % spb, 0, 0)),
        ] + [pl.BlockSpec((1, t // KEY_TILE, d), lambda i: (i // spb, 0, 0))] * block_means,
        out_shape=[
            jax.ShapeDtypeStruct((batch, t // ATT_TILE, d, ATT_TILE), BF16),
            jax.ShapeDtypeStruct((2, m, d), BF16),
            jax.ShapeDtypeStruct((batch, t // ATT_CHUNK, d, ATT_CHUNK), BF16),
        ] + [jax.ShapeDtypeStruct((batch, t // KEY_TILE, d), F32)] * block_means,
        scratch_shapes=[pltpu.VMEM((d, d), BF16)] * 3,
        compiler_params=pltpu.CompilerParams(
            dimension_semantics=("arbitrary",), vmem_limit_bytes=V7X_VMEM_LIMIT_BYTES),
        name="qkv_projection",
    )(x, g.reshape(1, d), w_in)


def _attn_out_mlp_up_kernel(a_ref, wo_ref, r_ref, g_ref, w1_ref, h_ref, u_ref, wo_b_ref, w1_b_ref):
    @pl.when(pl.program_id(0) == 0)
    def _():
        _cast_columns(wo_b_ref, wo_ref, 0, wo_ref.shape[1])
        _cast_columns(w1_b_ref, w1_ref, 0, w1_ref.shape[1])

    h = r_ref[...] + lax.dot_general(a_ref[...], wo_b_ref[...], (((0,), (0,)), ((), ())),
                                     preferred_element_type=F32)
    h_ref[...] = h
    xn = _rmsnorm_rows(h, g_ref[...]).astype(BF16)
    for c0 in range(0, w1_b_ref.shape[1], COL_TILE):
        acc = jnp.dot(xn, w1_b_ref[:, c0:c0 + COL_TILE], preferred_element_type=F32)
        u_ref[:, c0:c0 + COL_TILE] = jnp.square(jnp.maximum(acc, 0.0)).astype(u_ref.dtype)


def _attn_out_mlp_up(a_t, w_out, r, g, w_ff1, layer):
    m, d = r.shape
    n = w_ff1.shape[2]
    spq = a_t.shape[3] // ROW_TILE
    spb = a_t.shape[1] * spq
    assert a_t.shape[2] == d and a_t.shape[3] % ROW_TILE == 0
    rows = lambda width: pl.BlockSpec((ROW_TILE, width), lambda i: (i, 0))
    return pl.pallas_call(
        _attn_out_mlp_up_kernel,
        grid=(m // ROW_TILE,),
        in_specs=[pl.BlockSpec((None, None, d, ROW_TILE),
                               lambda i: (i // spb, (i % spb) // spq, 0, i % spq)),
                  _layer_weight_spec(w_out, layer), rows(d),
                  pl.BlockSpec((1, d), lambda i: (0, 0)), _layer_weight_spec(w_ff1, layer)],
        out_specs=[rows(d), rows(n)],
        out_shape=[jax.ShapeDtypeStruct((m, d), F32), jax.ShapeDtypeStruct((m, n), BF16)],
        scratch_shapes=[pltpu.VMEM((d, d), BF16), pltpu.VMEM((d, n), BF16)],
        compiler_params=pltpu.CompilerParams(
            dimension_semantics=("arbitrary",), vmem_limit_bytes=V7X_VMEM_LIMIT_BYTES),
        name="attn_out_mlp_up",
    )(a_t, w_out, r, g.reshape(1, d), w_ff1)


def _mlp_down_kernel(a_ref, w_ref, r_ref, g_ref, o_ref, w_b_ref, *, final_norm):
    @pl.when(pl.program_id(0) == 0)
    def _():
        _cast_columns(w_b_ref, w_ref, 0, w_ref.shape[1])

    for rows in (slice(0, ROW_TILE // 2), slice(ROW_TILE // 2, ROW_TILE)):
        h = r_ref[rows] + jnp.dot(a_ref[rows], w_b_ref[...], preferred_element_type=F32)
        if final_norm:
            h = _rmsnorm_rows(h, g_ref[...])
        o_ref[rows] = h


def _mlp_down(a, w_ff2, r, g, layer, *, final_norm):
    m, k = a.shape
    n = w_ff2.shape[2]
    return pl.pallas_call(
        functools.partial(_mlp_down_kernel, final_norm=final_norm),
        grid=(m // ROW_TILE,),
        in_specs=[
            pl.BlockSpec((ROW_TILE, k), lambda i: (i, 0)),
            _layer_weight_spec(w_ff2, layer),
            pl.BlockSpec((ROW_TILE, n), lambda i: (i, 0)),
            pl.BlockSpec((1, n), lambda i: (0, 0)),
        ],
        out_specs=pl.BlockSpec((ROW_TILE, n), lambda i: (i, 0)),
        out_shape=jax.ShapeDtypeStruct((m, n), F32),
        scratch_shapes=[pltpu.VMEM((k, n), BF16)],
        compiler_params=pltpu.CompilerParams(
            dimension_semantics=("arbitrary",), vmem_limit_bytes=V7X_VMEM_LIMIT_BYTES),
        name="mlp_down",
    )(a, w_ff2, r, g.reshape(1, n))


SUM_ON_MXU = {False: False, True: True}
L_MIN = 2.0 ** -60
UB_MARGIN = 1.01
KM_PIECES = 3
NORM_UNROLL = 16


def _attention_kernel(slope_ref, qt_ref, ka_ref, vt_ref, km_ref, lam_ref, subln_ref, o_ref,
                      qa_ref, ub_ref, l_ref, acc_ref, kn_ref, p_ref, ps_ref, m_ref, kmp_ref, *, moba,
                      lam_init):
    group = pl.program_id(1)
    qi = pl.program_id(2)
    t = ATT_TILE
    n_tab = AUG_SLOPE_PIECES + 1
    n_blocks = ka_ref.shape[1] // MOBA_BLOCK
    n_chunks = ka_ref.shape[1] // ATT_CHUNK

    def head_of(g, s):
        pair = group * ATT_GROUP + g
        return 2 * pair + s if moba else pair

    @pl.when(qi == 0)
    def _():
        lane8 = lax.broadcasted_iota(jnp.int32, (8, PAIR), 1)
        row8 = lax.broadcasted_iota(jnp.int32, (8, PAIR), 0)
        selector = jnp.where(row8 == (lane8 >= HEAD_DIM).astype(jnp.int32), 1.0, 0.0).astype(BF16)
        first_lanes = lax.broadcasted_iota(jnp.int32, (ATT_CHUNK, PAIR), 1) < HEAD_DIM
        unroll = math.gcd(n_chunks, NORM_UNROLL)
        for g in range(ATT_GROUP):
            def chunk_norms(trip, best):
                for c in range(unroll):
                    r0 = pl.multiple_of((trip * unroll + c) * ATT_CHUNK, ATT_CHUNK)
                    lanes = slice(g * PAIR, (g + 1) * PAIR)
                    k_pair = jnp.where(first_lanes, ka_ref[0, pl.ds(r0, ATT_CHUNK), lanes],
                                       ka_ref[1, pl.ds(r0, ATT_CHUNK), lanes])
                    sq = lax.dot_general(selector, k_pair * k_pair, _NT,
                                         preferred_element_type=F32)
                    best = jnp.maximum(best, sq)
                return best

            best = lax.fori_loop(0, n_chunks // unroll, chunk_norms,
                                 jnp.zeros((8, ATT_CHUNK), F32))
            k_norm = jnp.sqrt(jnp.max(best, axis=1, keepdims=True))
            for s in range(2):
                kn_ref[g, s] = jnp.broadcast_to(k_norm[s:s + 1], (8, PAIR))

    if moba:
        @pl.when(qi == 0)
        def _():
            if n_blocks < AUG_BLOCKS:
                kmp_ref[...] = jnp.zeros_like(kmp_ref)
            for g in range(ATT_GROUP):
                rest = km_ref[0, :, g * PAIR:(g + 1) * PAIR]
                for piece in range(KM_PIECES):
                    part = rest.astype(BF16)
                    kmp_ref[g, piece * AUG_BLOCKS:piece * AUG_BLOCKS + n_blocks, :] = part
                    rest = rest - part.astype(F32)

    row16 = lax.broadcasted_iota(jnp.int32, (AUG_ROWS, t), 0)
    own_rows = lax.broadcasted_iota(jnp.int32, (PAIR, t), 0) < HEAD_DIM
    blk = lax.broadcasted_iota(jnp.int32, (AUG_BLOCKS, t), 0)
    blk_f = blk.astype(F32)
    q_idx = lax.broadcasted_iota(jnp.int32, (1, t), 1)
    q_pos = q_idx.astype(F32)
    own_blk = qi * (t // MOBA_BLOCK) + q_idx // MOBA_BLOCK
    for g in range(ATT_GROUP):
        qt = qt_ref[0, 0, g * PAIR:(g + 1) * PAIR, :].astype(F32)
        for s in range(2):
            h = head_of(g, s)
            pos_rows = jnp.zeros((AUG_ROWS, t), F32)
            for piece in range(AUG_SLOPE_PIECES):
                hit = jnp.logical_or(row16 == piece, row16 == piece + AUG_SLOPE_PIECES)
                pos_rows = jnp.where(hit, slope_ref[h * n_tab + piece], pos_rows)
            if moba:
                q_own = jnp.where(own_rows if s == 0 else jnp.logical_not(own_rows), qt, 0.0)
                gate_pieces = jnp.dot(kmp_ref[g], q_own.astype(BF16),
                                      preferred_element_type=F32)
                gate = gate_pieces[0:AUG_BLOCKS]
                for piece in range(1, KM_PIECES):
                    gate = gate + gate_pieces[piece * AUG_BLOCKS:(piece + 1) * AUG_BLOCKS]
                gt = jnp.where(blk < own_blk, gate, -jnp.inf)
                selected = blk == own_blk
                for pick in range(MOBA_TOPK):
                    mx = jnp.max(gt, axis=0, keepdims=True)
                    idx = jnp.min(jnp.where(gt == mx, blk_f, F32(AUG_BLOCKS)), axis=0,
                                  keepdims=True)
                    hit = jnp.logical_and(blk_f == idx, pick < own_blk)
                    selected = jnp.logical_or(selected, hit)
                    gt = jnp.where(blk_f == idx, -jnp.inf, gt)
                sel_rows = jnp.where(selected, 0.0, NEG_INF)
            else:
                sel_rows = jnp.zeros((AUG_BLOCKS, t), F32)
            own = qt[0:HEAD_DIM] if s == 0 else qt[HEAD_DIM:PAIR]
            q_norm = jnp.sqrt(jnp.sum(own * own, axis=0, keepdims=True))
            ub_ref[g, s] = (q_norm * kn_ref[g, s][0:1, 0:1] * UB_MARGIN
                            + slope_ref[h * n_tab + AUG_SLOPE_PIECES] * q_pos)
            tail = jnp.zeros((HEAD_DIM - AUG_BLOCKS - AUG_ROWS, t), F32)
            parts = [sel_rows, pos_rows, tail]
            parts = [own] + parts if s == 0 else parts + [own]
            qa_ref[g, s] = jnp.concatenate(parts, axis=0).astype(BF16)

    chains = [(g, s) for g in range(ATT_GROUP) for s in range(2)]
    first_own = qi * Q_CHUNKS
    half = ATT_CHUNK

    def chunk_offset(g, s, c):
        return (slope_ref[head_of(g, s) * n_tab + AUG_SLOPE_PIECES]
                * ((c - first_own) * ATT_CHUNK).astype(F32))

    v_rows = HEAD_DIM if moba else PAIR
    ones_rows = jnp.ones((AUG_ROWS, ATT_CHUNK), BF16)

    def key_rows(g, s, c):
        r0 = pl.multiple_of(c * ATT_CHUNK, ATT_CHUNK)
        return ka_ref[s, pl.ds(r0, ATT_CHUNK), g * PAIR:(g + 1) * PAIR]

    def value_rows(g, s, c):
        v0 = g * PAIR + (s * HEAD_DIM if moba else 0)
        return jnp.concatenate([vt_ref[0, c, v0:v0 + v_rows, :], ones_rows], axis=0)

    def probabilities(g, s, c, slot):
        st = jnp.dot(key_rows(g, s, c), qa_ref[g, s], preferred_element_type=F32)
        p = jnp.exp2(st - (ub_ref[g, s] - chunk_offset(g, s, c)))
        p_ref[slot, g, s] = p.astype(BF16)
        if not SUM_ON_MXU[moba]:
            l_ref[g, s] += jnp.sum(p, axis=0, keepdims=True)

    def values(g, s, c, slot, q0=0):
        if SUM_ON_MXU[moba]:
            pv = jnp.dot(value_rows(g, s, c), p_ref[slot, g, s, :, q0:t],
                         preferred_element_type=F32)
            l_ref[g, s, :, q0:t] += pv[v_rows:v_rows + 1]
            acc_ref[g, s, :, q0:t] += pv[0:v_rows]
        else:
            v0 = g * PAIR + (s * HEAD_DIM if moba else 0)
            acc_ref[g, s, :, q0:t] += jnp.dot(vt_ref[0, c, v0:v0 + v_rows, :],
                                              p_ref[slot, g, s, :, q0:t],
                                              preferred_element_type=F32)

    row_i = lax.broadcasted_iota(jnp.int32, (ATT_CHUNK, t), 0)
    col_i = lax.broadcasted_iota(jnp.int32, (ATT_CHUNK, t), 1)
    mask_first = row_i <= col_i
    mask_second = row_i + half <= col_i

    def own_probabilities(g, s, which):
        c = first_own + which
        slot = 1 - which
        q0 = which * half
        r0 = pl.multiple_of(c * ATT_CHUNK, ATT_CHUNK)
        lanes = slice(g * PAIR, (g + 1) * PAIR)
        mask = mask_second if which else mask_first
        off = chunk_offset(g, s, c)

        def strip(keys, a, b, masked):
            st = jnp.dot(ka_ref[s, pl.ds(r0, keys), lanes], qa_ref[g, s, :, a:b],
                         preferred_element_type=F32)
            if masked:
                st = jnp.where(mask[0:keys, a:b], st, NEG_INF)
            p = jnp.exp2(st - (ub_ref[g, s, :, a:b] - off))
            p_ref[slot, g, s, 0:keys, a:b] = p.astype(BF16)
            if keys < ATT_CHUNK:
                p_ref[slot, g, s, keys:ATT_CHUNK, a:b] = jnp.zeros((ATT_CHUNK - keys, b - a), BF16)
            if not SUM_ON_MXU[moba]:
                l_ref[g, s, :, a:b] += jnp.sum(p, axis=0, keepdims=True)

        strip(KEY_TILE, q0, q0 + KEY_TILE, True)
        strip(ATT_CHUNK, q0 + KEY_TILE, q0 + half, True)
        if q0 + half < t:
            strip(ATT_CHUNK, q0 + half, t, False)

    def write_output():
        for g in range(ATT_GROUP):
            out_a = acc_ref[g, 0] / l_ref[g, 0]
            out_b = acc_ref[g, 1] / l_ref[g, 1]
            if moba:
                o = jnp.concatenate([out_a, out_b], axis=0)
            else:
                lp = lam_ref[...]
                lam = (jnp.exp(jnp.sum(lp[0:1] * lp[1:2], axis=-1, keepdims=True))
                       - jnp.exp(jnp.sum(lp[2:3] * lp[3:4], axis=-1, keepdims=True)) + lam_init)
                o = out_a - lam * out_b
                rms = lax.rsqrt(jnp.mean(o * o, axis=0, keepdims=True) + RMS_EPS)
                o = o * rms * pltpu.repeat(subln_ref[...], t // PAIR, axis=1) * (1.0 - lam_init)
            o_ref[0, 0, g * PAIR:(g + 1) * PAIR, :] = o.astype(o_ref.dtype)

    l_ref[...] = jnp.zeros_like(l_ref)
    acc_ref[...] = jnp.zeros_like(acc_ref)
    for g, s in chains:
        own_probabilities(g, s, 1)
    for g, s in chains:
        own_probabilities(g, s, 0)
        values(g, s, first_own + 1, 0, q0=half)

    def two_steps(j, carry):
        c_odd = jnp.where(j == 0, first_own, 2 * j - 1)
        for g, s in chains:
            probabilities(g, s, 2 * j, 0)
            values(g, s, c_odd, 1)
        for g, s in chains:
            probabilities(g, s, 2 * j + 1, 1)
            values(g, s, 2 * j, 0)
        return carry

    lax.fori_loop(0, qi, two_steps, 0)
    c_last = jnp.where(qi == 0, first_own, first_own - 1)
    for g, s in chains:
        values(g, s, c_last, 1)
    write_output()

    def safe_path():
        m_ref[...] = jnp.full_like(m_ref, NEG_INF)
        l_ref[...] = jnp.zeros_like(l_ref)
        acc_ref[...] = jnp.zeros_like(acc_ref)

        def fold(g, s, c, mask):
            st = jnp.dot(key_rows(g, s, c), qa_ref[g, s], preferred_element_type=F32)
            if mask is not None:
                st = jnp.where(mask, st, NEG_INF)
            off = chunk_offset(g, s, c)
            m_prev = m_ref[g, s]
            m_next = jnp.maximum(m_prev, jnp.max(st, axis=0, keepdims=True) + off)
            alpha = jnp.exp2(m_prev - m_next)
            ps_ref[...] = jnp.exp2(st - (m_next - off)).astype(BF16)
            pv = jnp.dot(value_rows(g, s, c), ps_ref[...], preferred_element_type=F32)
            m_ref[g, s] = m_next
            l_ref[g, s] = alpha * l_ref[g, s] + pv[v_rows:v_rows + 1]
            acc_ref[g, s] = alpha * acc_ref[g, s] + pv[0:v_rows]

        for g, s in chains:
            fold(g, s, first_own, mask_first)
        for g, s in chains:
            fold(g, s, first_own + 1, mask_second)

        def past(c, carry):
            for g, s in chains:
                fold(g, s, c, None)
            return carry

        lax.fori_loop(0, first_own, past, 0)

    l_min = l_ref[0, 0]
    for g, s in chains[1:]:
        l_min = jnp.minimum(l_min, l_ref[g, s])

    @pl.when(jnp.logical_not(jnp.min(l_min) >= L_MIN))
    def _():
        safe_path()
        write_output()


def _attention(qt, ka, vt, km, slope_table, lam_params, subln_w, *, moba, lam_init):
    b, nq, d, _ = qt.shape
    t = nq * ATT_TILE
    n_blocks = t // MOBA_BLOCK
    assert ATT_TILE == Q_CHUNKS * ATT_CHUNK and Q_CHUNKS == 2 and n_blocks <= AUG_BLOCKS
    assert AUG_POS + 2 * AUG_SLOPE_PIECES <= AUG_BLOCKS + AUG_ROWS <= HEAD_DIM
    gl = ATT_GROUP * PAIR
    return pl.pallas_call(
        functools.partial(_attention_kernel, moba=moba, lam_init=lam_init),
        grid=(b, N_PAIRS // ATT_GROUP, nq),
        in_specs=[
            pl.BlockSpec(memory_space=pltpu.SMEM),
            pl.BlockSpec((1, 1, gl, ATT_TILE), lambda bi, p, i: (bi, i, p, 0)),
            pl.BlockSpec((2, None, t, gl), lambda bi, p, i: (0, bi, 0, p)),
            pl.BlockSpec((1, t // ATT_CHUNK, gl, ATT_CHUNK), lambda bi, p, i: (bi, 0, p, 0)),
            pl.BlockSpec((1, n_blocks, gl), lambda bi, p, i: (bi, 0, p)),
            pl.BlockSpec((4, HEAD_DIM), lambda bi, p, i: (0, 0)),
            pl.BlockSpec((PAIR, PAIR), lambda bi, p, i: (0, 0)),
        ],
        out_specs=pl.BlockSpec((1, 1, gl, ATT_TILE), lambda bi, p, i: (bi, i, p, 0)),
        out_shape=jax.ShapeDtypeStruct((b, nq, d, ATT_TILE), BF16),
        scratch_shapes=[
            pltpu.VMEM((ATT_GROUP, 2, PAIR, ATT_TILE), BF16),
            pltpu.VMEM((ATT_GROUP, 2, 1, ATT_TILE), F32),
            pltpu.VMEM((ATT_GROUP, 2, 1, ATT_TILE), F32),
            pltpu.VMEM((ATT_GROUP, 2, HEAD_DIM if moba else PAIR, ATT_TILE), F32),
            pltpu.VMEM((ATT_GROUP, 2, 8, PAIR), F32),
            pltpu.VMEM((2, ATT_GROUP, 2, ATT_CHUNK, ATT_TILE), BF16),
            pltpu.VMEM((ATT_CHUNK, ATT_TILE), BF16),
            pltpu.VMEM((ATT_GROUP, 2, 1, ATT_TILE), F32),
            pltpu.VMEM((ATT_GROUP, KM_PIECES * AUG_BLOCKS, PAIR), BF16),
        ],
        compiler_params=pltpu.CompilerParams(
            dimension_semantics=("arbitrary", "arbitrary", "arbitrary"),
            vmem_limit_bytes=V7X_VMEM_LIMIT_BYTES),
        name="moba_attention" if moba else "diff_attention",
    )(slope_table, qt, ka.reshape(2, b, t, d), vt, km, lam_params,
      jnp.broadcast_to(subln_w[:, None], (PAIR, PAIR)))


def kernel(x, attn_norm, w_in, w_out, diff_lambda, diff_subln, mlp_norm, w_ff1, w_ff2, final_norm):
    b, t, d = x.shape
    h = x.reshape(b * t, d)
    q_scale = HEAD_DIM ** -0.5 * LOG2E
    for i in range(DEPTH):
        moba = i % N_MIXERS == 1
        li = i // N_MIXERS
        qt, ka, vt, *km = _qkv_projection(h, attn_norm[i], w_in, i, batch=b, q_scale=q_scale,
                                          block_means=moba)
        km = km[0] if moba else jnp.zeros((b, t // MOBA_BLOCK, d), F32)
        slope_table = jnp.asarray(_alibi_slope_pieces(MOBA_HEADS if moba else DIFF_HEADS))
        mix = _attention(qt, ka, vt, km, slope_table, diff_lambda[li], diff_subln[li],
                         moba=moba, lam_init=_lambda_init(i))
        h, u = _attn_out_mlp_up(mix, w_out, h, mlp_norm[i], w_ff1, i)
        h = _mlp_down(u, w_ff2, h, final_norm, i, final_norm=(i == DEPTH - 1))
    return h.reshape(b, t, d)
```
